```python
import jax, jax.numpy as jnp
from jax import lax
import numpy as np

D_MODEL = 1024
BATCH = 2
SEQ = 16384
DEPTH = 2
DEC_BATCH = 32
DEC_SEQ = 16
PAST_LEN = 4096

CHUNK = 64
N_MIXERS = 2
N_RET_LAYERS = (DEPTH + 1) // 2
N_CONV_LAYERS = DEPTH // 2
RET_HEADS = 4
RET_DK = D_MODEL // RET_HEADS
RET_DV = 2 * D_MODEL // RET_HEADS
RET_QK = RET_HEADS * RET_DK
RET_V = RET_HEADS * RET_DV
RET_IN = 2 * RET_QK + 2 * RET_V
CONV_WIDTH = 3
D_FF = 2816
N_EXPERTS = 8
TOP_K = 2
ROPE_BASE = 10000.0
LN_EPS = 1e-5
GN_EPS = 1e-6
DN_ALPHA = (2 * DEPTH) ** 0.25
DN_BETA = (8 * DEPTH) ** -0.25

kernel_name = 'retnet_shortconv_stream_step'


def layer_norm(x, g, b):
    xf = x.astype(jnp.float32)
    mu = xf.mean(-1, keepdims=True)
    var = jnp.square(xf - mu).mean(-1, keepdims=True)
    return ((xf - mu) * lax.rsqrt(var + LN_EPS)).astype(x.dtype) * g + b


def head_group_norm(o):
    of = o.astype(jnp.float32)
    mu = of.mean(-1, keepdims=True)
    var = jnp.square(of - mu).mean(-1, keepdims=True)
    return ((of - mu) * lax.rsqrt(var + GN_EPS)).astype(o.dtype)


def rotary(x, pos):
    half = RET_DK // 2
    inv = 1.0 / (ROPE_BASE ** jnp.linspace(0.0, 1.0, half, dtype=jnp.float32))
    ang = pos.astype(jnp.float32)[:, None] * inv[None, :]
    cos = jnp.cos(ang).astype(x.dtype)
    sin = jnp.sin(ang).astype(x.dtype)
    x1, x2 = x[..., :half], x[..., half:]
    return jnp.concatenate([x1 * cos - x2 * sin, x1 * sin + x2 * cos], axis=-1)


def retention_log_gamma():
    return jnp.log1p(-jnp.exp2(-5.0 - jnp.arange(RET_HEADS, dtype=jnp.float32)))


def retention_chunk(R, qkv, log_g):
    q, k, v = qkv
    L = q.shape[2]
    j = jnp.arange(L, dtype=jnp.float32)
    lg = log_g[:, None]
    inner_decay = jnp.exp(lg[:, :, None] * jnp.abs(j[:, None] - j[None, :]))
    cross_decay = jnp.exp(lg * (j + 1.0))
    state_decay = jnp.exp(lg * (L - 1.0 - j))
    chunk_decay = jnp.exp(log_g * L)
    s = jnp.einsum('bhld,bhmd->bhlm', q, k) * inner_decay.astype(q.dtype)
    o = (jnp.einsum('bhlm,bhme->bhle', s, v)
         + jnp.einsum('bhld,bhde->bhle', q, R) * cross_decay[..., None].astype(q.dtype))
    R_new = (R * chunk_decay[:, None, None].astype(R.dtype)
             + jnp.einsum('bhld,bhle->bhde', k * state_decay[..., None].astype(k.dtype), v))
    return R_new, o


def retention_mixer(x, pos, R0, w_in, w_out):
    B, L, _ = x.shape
    q, k, v, g = jnp.split(x @ w_in, [RET_QK, 2 * RET_QK, 2 * RET_QK + RET_V], axis=-1)
    def heads(t, d):
        return t.reshape(B, L, RET_HEADS, d).transpose(0, 2, 1, 3)
    q = rotary(heads(q, RET_DK), pos)
    k = rotary(heads(k, RET_DK), pos) * (RET_DK ** -0.5)
    v = heads(v, RET_DV)
    c = min(L, CHUNK)
    n = L // c
    def to_chunks(t):
        return t.reshape(B, RET_HEADS, n, c, t.shape[-1]).transpose(2, 0, 1, 3, 4)
    log_g = retention_log_gamma()
    R_final, o = lax.scan(lambda R, qkv: retention_chunk(R, qkv, log_g), R0,
                          (to_chunks(q), to_chunks(k), to_chunks(v)))
    o = o.transpose(1, 2, 0, 3, 4).reshape(B, RET_HEADS, L, RET_DV).transpose(0, 2, 1, 3)
    o = head_group_norm(o).reshape(B, L, RET_V)
    return (jax.nn.silu(g) * o) @ w_out, R_final


def short_conv_mixer(x, buf, w_in, w_conv, w_out):
    L = x.shape[1]
    b, c, h = jnp.split(x @ w_in, 3, axis=-1)
    u = c * h
    up = jnp.concatenate([buf, u], axis=1)
    conv = w_conv[0] * up[:, 0:L]
    for i in range(1, CONV_WIDTH):
        conv = conv + w_conv[i] * up[:, i:i + L]
    return (b * conv) @ w_out, up[:, -(CONV_WIDTH - 1):]


def swiglu(x, w_gu, w_down):
    g, u = jnp.split(x @ w_gu, 2, axis=-1)
    return (jax.nn.silu(g) * u) @ w_down


def moe_swiglu(x, w_router, w_gu, w_down):
    B, L, D = x.shape
    t = x.reshape(B * L, D)
    logits = (t @ w_router).astype(jnp.float32)
    top_v, top_i = lax.top_k(logits, TOP_K)
    top_w = jax.nn.softmax(top_v, axis=-1)
    gates = jnp.sum(jax.nn.one_hot(top_i, N_EXPERTS, dtype=jnp.float32) * top_w[..., None],
                    axis=1).astype(x.dtype)
    out = jnp.zeros_like(t)
    for e in range(N_EXPERTS):
        out = out + gates[:, e:e + 1] * swiglu(t, w_gu[e], w_down[e])
    return out.reshape(B, L, D)


def trunk(x, pos, ret_states, conv_states, ret_w_in, ret_w_out, conv_w_in, conv_w, conv_w_out,
          ffn_w_gu, ffn_w_down, moe_w_router, moe_w_gu, moe_w_down, ln_gain, ln_bias):
    new_ret = []
    new_conv = []
    for i in range(DEPTH):
        j = i // N_MIXERS
        if i % N_MIXERS == 0:
            m, s = retention_mixer(x, pos, ret_states[j], ret_w_in[j], ret_w_out[j])
            new_ret.append(s)
        else:
            m, s = short_conv_mixer(x, conv_states[j], conv_w_in[j], conv_w[j], conv_w_out[j])
            new_conv.append(s)
        x = layer_norm(DN_ALPHA * x + m, ln_gain[i, 0], ln_bias[i, 0])
        if i % 2 == 0:
            f = swiglu(x, ffn_w_gu[j], ffn_w_down[j])
        else:
            f = moe_swiglu(x, moe_w_router[j], moe_w_gu[j], moe_w_down[j])
        x = layer_norm(DN_ALPHA * x + f, ln_gain[i, 1], ln_bias[i, 1])
    return x, jnp.stack(new_ret), jnp.stack(new_conv)


def setup_inputs(seed: int = 0) -> dict:
    key = jax.random.key(seed)
    ks = jax.random.split(key, 16)
    def nrm(k, shape, s):
        return jax.random.normal(k, shape, jnp.float32) * s
    v_scale = jnp.concatenate([jnp.ones((2 * RET_QK,), jnp.float32),
                               jnp.full((RET_V,), DN_BETA, jnp.float32),
                               jnp.ones((RET_V,), jnp.float32)])
    return {
        'x_prompt': nrm(ks[0], (BATCH, SEQ, D_MODEL), 1.0),
        'x_sample': nrm(ks[1], (DEC_BATCH, DEC_SEQ, D_MODEL), 1.0),
        'state_ret': nrm(ks[2], (N_RET_LAYERS, DEC_BATCH, RET_HEADS, RET_DK, RET_DV), 0.1),
        'state_conv': nrm(ks[3], (N_CONV_LAYERS, DEC_BATCH, CONV_WIDTH - 1, D_MODEL), 1.0),
        'ret_w_in': nrm(ks[4], (N_RET_LAYERS, D_MODEL, RET_IN), D_MODEL ** -0.5) * v_scale,
        'ret_w_out': nrm(ks[5], (N_RET_LAYERS, RET_V, D_MODEL), DN_BETA * RET_V ** -0.5),
        'conv_w_in': nrm(ks[6], (N_CONV_LAYERS, D_MODEL, 3 * D_MODEL), D_MODEL ** -0.5),
        'conv_w': nrm(ks[7], (N_CONV_LAYERS, CONV_WIDTH, D_MODEL), CONV_WIDTH ** -0.5),
        'conv_w_out': nrm(ks[8], (N_CONV_LAYERS, D_MODEL, D_MODEL), DN_BETA * D_MODEL ** -0.5),
        'ffn_w_gu': nrm(ks[9], (N_RET_LAYERS, D_MODEL, 2 * D_FF), D_MODEL ** -0.5),
        'ffn_w_down': nrm(ks[10], (N_RET_LAYERS, D_FF, D_MODEL), DN_BETA * D_FF ** -0.5),
        'moe_w_router': nrm(ks[11], (N_CONV_LAYERS, D_MODEL, N_EXPERTS), D_MODEL ** -0.5),
        'moe_w_gu': nrm(ks[12], (N_CONV_LAYERS, N_EXPERTS, D_MODEL, 2 * D_FF), D_MODEL ** -0.5),
        'moe_w_down': nrm(ks[13], (N_CONV_LAYERS, N_EXPERTS, D_FF, D_MODEL), DN_BETA * D_FF ** -0.5),
        'ln_gain': 1.0 + nrm(ks[14], (DEPTH, 2, D_MODEL), 0.02),
        'ln_bias': nrm(ks[15], (DEPTH, 2, D_MODEL), 0.02),
    }


def reference(x_prompt, x_sample, state_ret, state_conv, ret_w_in, ret_w_out, conv_w_in, conv_w,
              conv_w_out, ffn_w_gu, ffn_w_down, moe_w_router, moe_w_gu, moe_w_down, ln_gain, ln_bias):
    B, S, _ = x_prompt.shape
    zero_ret = jnp.zeros((N_RET_LAYERS, B, RET_HEADS, RET_DK, RET_DV), x_prompt.dtype)
    zero_conv = jnp.zeros((N_CONV_LAYERS, B, CONV_WIDTH - 1, D_MODEL), x_prompt.dtype)
    pos_prompt = jnp.arange(S, dtype=jnp.int32)
    y_prompt, ret_state_prompt, conv_state_prompt = trunk(
        x_prompt, pos_prompt, zero_ret, zero_conv, ret_w_in, ret_w_out, conv_w_in, conv_w, conv_w_out,
        ffn_w_gu, ffn_w_down, moe_w_router, moe_w_gu, moe_w_down, ln_gain, ln_bias)
    pos_sample = PAST_LEN + jnp.arange(x_sample.shape[1], dtype=jnp.int32)
    y_sample, ret_state_sample, conv_state_sample = trunk(
        x_sample, pos_sample, state_ret, state_conv, ret_w_in, ret_w_out, conv_w_in, conv_w, conv_w_out,
        ffn_w_gu, ffn_w_down, moe_w_router, moe_w_gu, moe_w_down, ln_gain, ln_bias)
    return (y_prompt, y_sample, ret_state_prompt, ret_state_sample, conv_state_prompt, conv_state_sample)
```

```python
import functools
import math

import jax
import jax.numpy as jnp
from jax import lax
from jax.experimental import pallas as pl
from jax.experimental.pallas import tpu as pltpu

D_MODEL = 1024
CHUNK = 64
RET_HEADS = 4
RET_DK = D_MODEL // RET_HEADS
RET_DV = 2 * D_MODEL // RET_HEADS
RET_QK = RET_HEADS * RET_DK
RET_V = RET_HEADS * RET_DV
RET_IN = 2 * RET_QK + 2 * RET_V
ROPE_HALF = RET_DK // 2
CONV_WIDTH = 3
D_FF = 2816
N_EXPERTS = 8
ROPE_BASE = 10000.0
LN_EPS = 1e-5
GN_EPS = 1e-6
DEPTH = 2
DN_ALPHA = (2 * DEPTH) ** 0.25
PAST_LEN = 4096

V7X_VMEM_LIMIT_BYTES = 58 * 1024 * 1024
LANES = 128
SUBLANES = 8

FF_SPLIT = 2
FF_HALF = D_FF // FF_SPLIT
MOE_ROWS = 128
CUMSUM_CHUNK = 512

F32 = jnp.float32
BF16 = jnp.bfloat16


def _params(*semantics):
    return pltpu.CompilerParams(dimension_semantics=semantics,
                                vmem_limit_bytes=V7X_VMEM_LIMIT_BYTES)


def _resident(shape):
    zeros = (0,) * len(shape)
    return pl.BlockSpec(shape, lambda *_: zeros, pipeline_mode=pl.Buffered(1))


def _layer_norm(z, gain, bias):
    mu = jnp.mean(z, axis=-1, keepdims=True)
    zc = z - mu
    var = jnp.mean(zc * zc, axis=-1, keepdims=True)
    return zc * lax.rsqrt(var + LN_EPS) * gain + bias


def _silu(x):
    return x / (1.0 + jnp.exp(-x))


def _dot(a, b):
    return jnp.dot(a, b, preferred_element_type=F32)


def _ret_in_kernel(x_ref, w_ref, inv_ref, q_ref, k_ref, v_ref, g_ref, *, tm, seq_len, pos_base):
    i = pl.program_id(0)
    xb = x_ref[...].astype(BF16)
    row = i * tm + lax.broadcasted_iota(jnp.int32, (tm, 1), 0)
    pos = (pos_base + (row & (seq_len - 1))).astype(F32)
    ang = pos * inv_ref[...]
    cos = jnp.cos(ang)
    sin = jnp.sin(ang)
    for dst, col0, scale in ((q_ref, 0, 1.0), (k_ref, RET_QK, RET_DK ** -0.5)):
        t = _dot(xb, w_ref[:, col0:col0 + RET_QK])
        for h in range(RET_HEADS):
            lo = h * RET_DK
            x1 = t[:, lo:lo + ROPE_HALF]
            x2 = t[:, lo + ROPE_HALF:lo + RET_DK]
            r1 = x1 * cos - x2 * sin
            r2 = x1 * sin + x2 * cos
            if scale != 1.0:
                r1, r2 = r1 * scale, r2 * scale
            dst[:, lo:lo + ROPE_HALF] = r1.astype(BF16)
            dst[:, lo + ROPE_HALF:lo + RET_DK] = r2.astype(BF16)
    v_ref[...] = _dot(xb, w_ref[:, 2 * RET_QK:2 * RET_QK + RET_V]).astype(BF16)
    g_ref[...] = _dot(xb, w_ref[:, 2 * RET_QK + RET_V:]).astype(BF16)


def _ret_in(x, w_bf, inv, *, seq_len, pos_base, tm):
    t = x.shape[0]
    assert t % tm == 0 and seq_len & (seq_len - 1) == 0
    row = lambda n: pl.BlockSpec((tm, n), lambda i: (i, 0))
    return pl.pallas_call(
        functools.partial(_ret_in_kernel, tm=tm, seq_len=seq_len, pos_base=pos_base),
        name="ret_in",
        grid=(t // tm,),
        in_specs=[row(D_MODEL), _resident((D_MODEL, RET_IN)), _resident((1, ROPE_HALF))],
        out_specs=[row(RET_QK), row(RET_QK), row(RET_V), row(RET_V)],
        out_shape=[jax.ShapeDtypeStruct((t, RET_QK), BF16), jax.ShapeDtypeStruct((t, RET_QK), BF16),
                   jax.ShapeDtypeStruct((t, RET_V), BF16), jax.ShapeDtypeStruct((t, RET_V), BF16)],
        compiler_params=_params("parallel"),
    )(x, w_bf, inv)


def _retention_kernel(lg_ref, q_ref, k_ref, v_ref, g_ref, r0_ref, o_ref, rout_ref,
                      state, decay, *pads, lb, l_in, chunk, nblk):
    b = pl.program_id(0)
    n = pl.program_id(1)
    padded = l_in < lb

    @pl.when((b == 0) & (n == 0))
    def _():
        ii = lax.broadcasted_iota(jnp.int32, (lb, lb), 0)
        jj = lax.broadcasted_iota(jnp.int32, (lb, lb), 1)
        shift = int(math.log2(chunk))
        visible = (jj >> shift) <= (ii >> shift)
        dist = jnp.abs(ii - jj).astype(F32)
        for h in range(RET_HEADS):
            decay[h] = jnp.where(visible, jnp.exp(lg_ref[h] * dist), 0.0)
        for p in pads:
            p[...] = jnp.zeros_like(p)

    @pl.when(n == 0)
    def _():
        state[...] = r0_ref[...]

    if padded:
        qp, kp, vp = pads
        qp[0:l_in, :] = q_ref[...]
        kp[0:l_in, :] = k_ref[...]
        vp[0:l_in, :] = v_ref[...]
        q_src, k_src, v_src = qp, kp, vp
    else:
        q_src, k_src, v_src = q_ref, k_ref, v_ref

    jcol = lax.broadcasted_iota(jnp.int32, (lb, 1), 0).astype(F32)
    for h in range(RET_HEADS):
        lg = lg_ref[h]
        qh = q_src[:, h * RET_DK:(h + 1) * RET_DK]
        kh = k_src[:, h * RET_DK:(h + 1) * RET_DK]
        vh = v_src[:, h * RET_DV:(h + 1) * RET_DV]
        r_old = state[h]
        s = lax.dot_general(qh, kh, (((1,), (1,)), ((), ())), preferred_element_type=F32)
        p = (s * decay[h]).astype(BF16)
        cross = jnp.exp(lg * (jcol + 1.0))
        o = _dot(p, vh) + _dot(qh, r_old.astype(BF16)) * cross
        k_dec = kh.astype(F32) * jnp.exp(lg * (l_in - 1.0 - jcol))
        block_decay = jnp.exp(jnp.full((1, RET_DV), lg * l_in, F32))
        state[h] = r_old * block_decay + _dot(k_dec.T.astype(BF16), vh)

        o = o[0:l_in]
        mu = jnp.mean(o, axis=-1, keepdims=True)
        oc = o - mu
        var = jnp.mean(oc * oc, axis=-1, keepdims=True)
        on = oc * lax.rsqrt(var + GN_EPS)
        gh = g_ref[:, h * RET_DV:(h + 1) * RET_DV].astype(F32)
        o_ref[:, h * RET_DV:(h + 1) * RET_DV] = (_silu(gh) * on).astype(BF16)

    @pl.when(n == nblk - 1)
    def _():
        rout_ref[...] = state[...]


def _retention(q, k, v, g, r0, log_g, *, n_seq, seq_len, lb):
    l_in = min(seq_len, lb)
    chunk = min(seq_len, CHUNK)
    assert seq_len % l_in == 0 and lb % chunk == 0 and chunk & (chunk - 1) == 0
    nblk = seq_len // l_in
    t = n_seq * seq_len
    row = lambda width: pl.BlockSpec((l_in, width), lambda b, n: (b * nblk + n, 0))
    st = pl.BlockSpec((None, RET_HEADS, RET_DK, RET_DV), lambda b, n: (b, 0, 0, 0))
    scratch = [pltpu.VMEM((RET_HEADS, RET_DK, RET_DV), F32), pltpu.VMEM((RET_HEADS, lb, lb), F32)]
    if l_in < lb:
        scratch += [pltpu.VMEM((lb, RET_QK), BF16), pltpu.VMEM((lb, RET_QK), BF16),
                    pltpu.VMEM((lb, RET_V), BF16)]
    return pl.pallas_call(
        functools.partial(_retention_kernel, lb=lb, l_in=l_in, chunk=chunk, nblk=nblk),
        name="retention",
        grid=(n_seq, nblk),
        in_specs=[pl.BlockSpec(memory_space=pltpu.SMEM), row(RET_QK), row(RET_QK), row(RET_V),
                  row(RET_V), st],
        out_specs=[row(RET_V), st],
        out_shape=[jax.ShapeDtypeStruct((t, RET_V), BF16),
                   jax.ShapeDtypeStruct((n_seq, RET_HEADS, RET_DK, RET_DV), F32)],
        scratch_shapes=scratch,
        compiler_params=_params("arbitrary", "arbitrary"),
    )(log_g, q, k, v, g, r0)


def _proj_ln_kernel(a_ref, w_ref, x_ref, gain_ref, bias_ref, o_ref):
    z = DN_ALPHA * x_ref[...] + _dot(a_ref[...], w_ref[...])
    o_ref[...] = _layer_norm(z, gain_ref[...], bias_ref[...])


def _proj_ln(a, w_bf, x, gain, bias, *, tm):
    t, kdim = a.shape
    assert t % tm == 0
    return pl.pallas_call(
        _proj_ln_kernel,
        name="proj_ln",
        grid=(t // tm,),
        in_specs=[pl.BlockSpec((tm, kdim), lambda i: (i, 0)), _resident((kdim, D_MODEL)),
                  pl.BlockSpec((tm, D_MODEL), lambda i: (i, 0)),
                  _resident((1, D_MODEL)), _resident((1, D_MODEL))],
        out_specs=pl.BlockSpec((tm, D_MODEL), lambda i: (i, 0)),
        out_shape=jax.ShapeDtypeStruct((t, D_MODEL), F32),
        compiler_params=_params("parallel"),
    )(a, w_bf, x, gain, bias)


def _ffn_ln_kernel(x_ref, wgu_ref, wd_ref, gain_ref, bias_ref, o_ref):
    x = x_ref[...]
    xb = x.astype(BF16)
    gate = _dot(xb, wgu_ref[:, 0:D_FF])
    up = _dot(xb, wgu_ref[:, D_FF:2 * D_FF])
    hid = (_silu(gate) * up).astype(BF16)
    z = DN_ALPHA * x + _dot(hid, wd_ref[...])
    o_ref[...] = _layer_norm(z, gain_ref[...], bias_ref[...])


def _ffn_ln(x, wgu_bf, wd_bf, gain, bias, *, tm):
    t = x.shape[0]
    assert t % tm == 0
    return pl.pallas_call(
        _ffn_ln_kernel,
        name="ffn_ln",
        grid=(t // tm,),
        in_specs=[pl.BlockSpec((tm, D_MODEL), lambda i: (i, 0)),
                  _resident((D_MODEL, 2 * D_FF)), _resident((D_FF, D_MODEL)),
                  _resident((1, D_MODEL)), _resident((1, D_MODEL))],
        out_specs=pl.BlockSpec((tm, D_MODEL), lambda i: (i, 0)),
        out_shape=jax.ShapeDtypeStruct((t, D_MODEL), F32),
        compiler_params=_params("parallel"),
    )(x, wgu_bf, wd_bf, gain, bias)


def _conv_ln_kernel(x_ref, win_ref, wc_ref, wout_ref, buf_ref, gain_ref, bias_ref,
                    o_ref, st_ref, u_win, prev, mixed, *, tm, seg, tiles_per_seq):
    i = pl.program_id(0)
    x = x_ref[...]
    xb = x.astype(BF16)
    gate_b = _dot(xb, win_ref[:, 0:D_MODEL])
    u = _dot(xb, win_ref[:, D_MODEL:2 * D_MODEL]) * _dot(xb, win_ref[:, 2 * D_MODEL:])
    w0 = wc_ref[0:1, :]
    w1 = wc_ref[1:2, :]
    w2 = wc_ref[2:3, :]
    tail = CONV_WIDTH - 1
    for s in range(tm // seg):
        if tiles_per_seq == 1:
            prev[SUBLANES - tail:SUBLANES, :] = buf_ref[s]
        else:
            @pl.when(i % tiles_per_seq == 0)
            def _():
                prev[SUBLANES - tail:SUBLANES, :] = buf_ref[0]
        u_win[0:SUBLANES, :] = prev[...]
        u_win[SUBLANES:SUBLANES + seg, :] = u[s * seg:(s + 1) * seg]
        conv = (w0 * u_win[SUBLANES - 2:SUBLANES - 2 + seg, :]
                + w1 * u_win[SUBLANES - 1:SUBLANES - 1 + seg, :]
                + w2 * u_win[SUBLANES:SUBLANES + seg, :])
        mixed[s * seg:(s + 1) * seg, :] = (gate_b[s * seg:(s + 1) * seg] * conv).astype(BF16)
        prev[...] = u_win[seg:seg + SUBLANES, :]
        st_ref[s] = u_win[SUBLANES + seg - tail:SUBLANES + seg, :]
    z = DN_ALPHA * x + _dot(mixed[...], wout_ref[...])
    o_ref[...] = _layer_norm(z, gain_ref[...], bias_ref[...])


def _conv_ln(x, win_bf, wconv, wout_bf, buf, gain, bias, *, n_seq, seq_len, tm):
    t = x.shape[0]
    seg = min(seq_len, tm)
    assert t % tm == 0 and tm % seg == 0 and seq_len % seg == 0 and seg % SUBLANES == 0
    tiles_per_seq = seq_len // seg
    seq_per_tile = tm // seg
    tail = CONV_WIDTH - 1
    if tiles_per_seq == 1:
        st_map = lambda i: (i, 0, 0)
    else:
        st_map = lambda i: (i // tiles_per_seq, 0, 0)
    st_spec = pl.BlockSpec((seq_per_tile, tail, D_MODEL), st_map)
    return pl.pallas_call(
        functools.partial(_conv_ln_kernel, tm=tm, seg=seg, tiles_per_seq=tiles_per_seq),
        name="conv_ln",
        grid=(t // tm,),
        in_specs=[pl.BlockSpec((tm, D_MODEL), lambda i: (i, 0)),
                  _resident((D_MODEL, 3 * D_MODEL)), _resident((CONV_WIDTH, D_MODEL)),
                  _resident((D_MODEL, D_MODEL)), st_spec,
                  _resident((1, D_MODEL)), _resident((1, D_MODEL))],
        out_specs=[pl.BlockSpec((tm, D_MODEL), lambda i: (i, 0)), st_spec],
        out_shape=[jax.ShapeDtypeStruct((t, D_MODEL), F32),
                   jax.ShapeDtypeStruct((n_seq, tail, D_MODEL), F32)],
        scratch_shapes=[pltpu.VMEM((SUBLANES + seg, D_MODEL), F32),
                        pltpu.VMEM((SUBLANES, D_MODEL), F32),
                        pltpu.VMEM((tm, D_MODEL), BF16)],
        compiler_params=_params("arbitrary"),
    )(x, win_bf, wconv, wout_bf, buf, gain, bias)


def _split_bf16(a):
    hi = a.astype(BF16)
    lo = (a - hi.astype(F32)).astype(BF16)
    return hi, lo


def _router_kernel(x_ref, wr_ref, rank_ref, gate_ref, cnt_ref, *, tm):
    x_hi, x_lo = _split_bf16(x_ref[...])
    w_hi, w_lo = _split_bf16(wr_ref[...])
    logits = _dot(x_hi, w_hi) + (_dot(x_lo, w_hi) + _dot(x_hi, w_lo))
    lt = logits.T[0:N_EXPERTS, :]
    eidx = lax.broadcasted_iota(jnp.int32, (N_EXPERTS, tm), 0).astype(F32)
    neg = jnp.float32(-jnp.inf)
    m1 = jnp.max(lt, axis=0, keepdims=True)
    i1 = jnp.min(jnp.where(lt == m1, eidx, float(N_EXPERTS)), axis=0, keepdims=True)
    first = eidx == i1
    rest = jnp.where(first, neg, lt)
    m2 = jnp.max(rest, axis=0, keepdims=True)
    i2 = jnp.min(jnp.where(rest == m2, eidx, float(N_EXPERTS)), axis=0, keepdims=True)
    second = eidx == i2
    e2 = jnp.exp(m2 - m1)
    w1 = 1.0 / (1.0 + e2)
    w2 = e2 / (1.0 + e2)
    gate_ref[...] = jnp.where(first, w1, 0.0) + jnp.where(second, w2, 0.0)
    sel = first | second
    self32 = jnp.where(sel, 1.0, 0.0)

    c = CUMSUM_CHUNK if tm % CUMSUM_CHUNK == 0 else tm
    si = lax.broadcasted_iota(jnp.int32, (c, c), 0)
    ti = lax.broadcasted_iota(jnp.int32, (c, c), 1)
    tri = jnp.where(si < ti, 1.0, 0.0).astype(BF16)
    offset = jnp.zeros((N_EXPERTS, 1), F32)
    for j in range(tm // c):
        blk = self32[:, j * c:(j + 1) * c]
        rank = _dot(blk.astype(BF16), tri) + offset
        rank_ref[:, j * c:(j + 1) * c] = jnp.where(sel[:, j * c:(j + 1) * c],
                                                   rank.astype(jnp.int32), -1)
        offset = offset + jnp.sum(blk, axis=1, keepdims=True)
    cnt_ref[...] = jnp.broadcast_to(offset.astype(jnp.int32), (N_EXPERTS, LANES))


def _router(x, wr_pad, *, tm):
    t = x.shape[0]
    assert t % tm == 0
    nt = t // tm
    return pl.pallas_call(
        functools.partial(_router_kernel, tm=tm),
        name="router",
        grid=(nt,),
        in_specs=[pl.BlockSpec((tm, D_MODEL), lambda i: (i, 0)), _resident((D_MODEL, LANES))],
        out_specs=[pl.BlockSpec((N_EXPERTS, tm), lambda i: (0, i)),
                   pl.BlockSpec((N_EXPERTS, tm), lambda i: (0, i)),
                   pl.BlockSpec((None, N_EXPERTS, LANES), lambda i: (i, 0, 0))],
        out_shape=[jax.ShapeDtypeStruct((N_EXPERTS, t), jnp.int32),
                   jax.ShapeDtypeStruct((N_EXPERTS, t), F32),
                   jax.ShapeDtypeStruct((nt, N_EXPERTS, LANES), jnp.int32)],
        compiler_params=_params("parallel"),
    )(x, wr_pad)


def _moe_ln_kernel(cnt_ref, x_ref, rank_ref, gate_ref, wgu_ref, wd_ref, gain_ref, bias_ref,
                   o_ref, xb, *, tm):
    i = pl.program_id(0)
    j = pl.program_id(1)
    nj = pl.num_programs(1)

    @pl.when(j == 0)
    def _():
        xb[...] = x_ref[...].astype(BF16)
        o_ref[...] = jnp.zeros_like(o_ref)

    count = cnt_ref[i * N_EXPERTS + j // FF_SPLIT]
    rank = rank_ref[...]
    gate = gate_ref[...]
    rows = lax.broadcasted_iota(jnp.int32, (MOE_ROWS, 1), 0)

    def sub_block(r, carry):
        hit = rank == (rows + r * MOE_ROWS)
        onehot = jnp.where(hit, 1.0, 0.0)
        xg = _dot(onehot.astype(BF16), xb[...]).astype(BF16)
        gcol = jnp.sum(jnp.where(hit, gate, 0.0), axis=1, keepdims=True)
        gu = _dot(xg, wgu_ref[...])
        hid = (_silu(gu[:, 0:FF_HALF]) * gu[:, FF_HALF:]).astype(BF16)
        y = (_dot(hid, wd_ref[...]) * gcol).astype(BF16)
        o_ref[...] += _dot(onehot.T.astype(BF16), y)
        return carry

    lax.fori_loop(0, (count + MOE_ROWS - 1) // MOE_ROWS, sub_block, 0)

    @pl.when(j == nj - 1)
    def _():
        z = DN_ALPHA * x_ref[...] + o_ref[...]
        o_ref[...] = _layer_norm(z, gain_ref[...], bias_ref[...])


def _moe_ln(x, rank, gate, counts, wgu_r, wd_r, gain, bias, *, tm):
    t = x.shape[0]
    assert t % tm == 0
    nt = t // tm
    tok = pl.BlockSpec((tm, D_MODEL), lambda i, j, c: (i, 0), pipeline_mode=pl.Buffered(1))
    per_expert = pl.BlockSpec((None, 1, tm), lambda i, j, c: (j // FF_SPLIT, 0, i))
    const = lambda shape: pl.BlockSpec(shape, lambda i, j, c: (0,) * len(shape),
                                       pipeline_mode=pl.Buffered(1))
    grid_spec = pltpu.PrefetchScalarGridSpec(
        num_scalar_prefetch=1,
        grid=(nt, N_EXPERTS * FF_SPLIT),
        in_specs=[tok, per_expert, per_expert,
                  pl.BlockSpec((None, None, D_MODEL, 2 * FF_HALF),
                               lambda i, j, c: (j // FF_SPLIT, j % FF_SPLIT, 0, 0)),
                  pl.BlockSpec((None, None, FF_HALF, D_MODEL),
                               lambda i, j, c: (j // FF_SPLIT, j % FF_SPLIT, 0, 0)),
                  const((1, D_MODEL)), const((1, D_MODEL))],
        out_specs=pl.BlockSpec((tm, D_MODEL), lambda i, j, c: (i, 0)),
        scratch_shapes=[pltpu.VMEM((tm, D_MODEL), BF16)],
    )
    return pl.pallas_call(
        functools.partial(_moe_ln_kernel, tm=tm),
        name="moe_ln",
        grid_spec=grid_spec,
        out_shape=jax.ShapeDtypeStruct((t, D_MODEL), F32),
        compiler_params=_params("arbitrary", "arbitrary"),
    )(counts, x, rank, gate, wgu_r, wd_r, gain, bias)


def _trunk(x, n_seq, seq_len, pos_base, r0, conv_buf, w, *, tm, lb, tm_moe):
    q, k, v, g = _ret_in(x, w["ret_in"], w["inv"], seq_len=seq_len, pos_base=pos_base, tm=tm)
    gated, r_new = _retention(q, k, v, g, r0, w["log_g"], n_seq=n_seq, seq_len=seq_len, lb=lb)
    x = _proj_ln(gated, w["ret_out"], x, w["gain"][0], w["bias"][0], tm=tm)
    x = _ffn_ln(x, w["ffn_gu"], w["ffn_down"], w["gain"][1], w["bias"][1], tm=tm)
    x, conv_new = _conv_ln(x, w["conv_in"], w["conv_w"], w["conv_out"], conv_buf,
                           w["gain"][2], w["bias"][2], n_seq=n_seq, seq_len=seq_len, tm=tm)
    rank, gate, counts = _router(x, w["router"], tm=tm_moe)
    t = x.shape[0]
    y = _moe_ln(x, rank.reshape(N_EXPERTS, 1, t), gate.reshape(N_EXPERTS, 1, t),
                counts[:, :, 0].reshape(-1), w["moe_gu"], w["moe_down"],
                w["gain"][3], w["bias"][3], tm=tm_moe)
    return y, r_new, conv_new


def kernel(x_prompt, x_sample, state_ret, state_conv, ret_w_in, ret_w_out, conv_w_in, conv_w,
           conv_w_out, ffn_w_gu, ffn_w_down, moe_w_router, moe_w_gu, moe_w_down, ln_gain, ln_bias):
    batch, seq, _ = x_prompt.shape
    dec_batch, dec_seq, _ = x_sample.shape

    moe_gu = moe_w_gu[0].reshape(N_EXPERTS, D_MODEL, 2, FF_SPLIT, FF_HALF)
    moe_gu = moe_gu.transpose(0, 3, 1, 2, 4).reshape(N_EXPERTS, FF_SPLIT, D_MODEL, 2 * FF_HALF)
    w = {
        "ret_in": ret_w_in[0].astype(BF16),
        "ret_out": ret_w_out[0].astype(BF16),
        "conv_in": conv_w_in[0].astype(BF16),
        "conv_w": conv_w[0],
        "conv_out": conv_w_out[0].astype(BF16),
        "ffn_gu": ffn_w_gu[0].astype(BF16),
        "ffn_down": ffn_w_down[0].astype(BF16),
        "router": jnp.pad(moe_w_router[0], ((0, 0), (0, LANES - N_EXPERTS))),
        "moe_gu": moe_gu.astype(BF16),
        "moe_down": moe_w_down[0].astype(BF16).reshape(N_EXPERTS, FF_SPLIT, FF_HALF, D_MODEL),
        "gain": ln_gain.reshape(2 * DEPTH, 1, D_MODEL),
        "bias": ln_bias.reshape(2 * DEPTH, 1, D_MODEL),
        "inv": (1.0 / (ROPE_BASE ** jnp.linspace(0.0, 1.0, ROPE_HALF, dtype=F32))).reshape(1, ROPE_HALF),
        "log_g": jnp.log1p(-jnp.exp2(-5.0 - jnp.arange(RET_HEADS, dtype=F32))),
    }

    zero_ret = jnp.zeros((batch, RET_HEADS, RET_DK, RET_DV), F32)
    zero_conv = jnp.zeros((batch, CONV_WIDTH - 1, D_MODEL), F32)
    y_p, ret_p, conv_p = _trunk(x_prompt.reshape(batch * seq, D_MODEL), batch, seq, 0,
                                zero_ret, zero_conv, w, tm=512, lb=256, tm_moe=2048)
    y_s, ret_s, conv_s = _trunk(x_sample.reshape(dec_batch * dec_seq, D_MODEL), dec_batch, dec_seq,
                                PAST_LEN, state_ret[0], state_conv[0], w,
                                tm=dec_batch * dec_seq, lb=LANES, tm_moe=dec_batch * dec_seq)
    return (y_p.reshape(batch, seq, D_MODEL), y_s.reshape(dec_batch, dec_seq, D_MODEL),
            ret_p[None], ret_s[None], conv_p[None], conv_s[None])
```

```python
import functools
import math

import jax
import jax.numpy as jnp
from jax import lax
from jax.experimental import pallas as pl
from jax.experimental.pallas import tpu as pltpu

D_MODEL = 1024
CHUNK = 64
RET_HEADS = 4
RET_DK = D_MODEL // RET_HEADS
RET_DV = 2 * D_MODEL // RET_HEADS
RET_QK = RET_HEADS * RET_DK
RET_V = RET_HEADS * RET_DV
RET_IN = 2 * RET_QK + 2 * RET_V
ROPE_HALF = RET_DK // 2
CONV_WIDTH = 3
D_FF = 2816
N_EXPERTS = 8
ROPE_BASE = 10000.0
LN_EPS = 1e-5
GN_EPS = 1e-6
DEPTH = 2
DN_ALPHA = (2 * DEPTH) ** 0.25
PAST_LEN = 4096

V7X_VMEM_LIMIT_BYTES = 58 * 1024 * 1024
LANES = 128
SUBLANES = 8

MOE_ROWS_LOG2 = 8
MOE_ROWS = 1 << MOE_ROWS_LOG2
MOE_TAIL_ROWS = (64, 128, MOE_ROWS)
CUMSUM_CHUNK = 512

F32 = jnp.float32
BF16 = jnp.bfloat16


def _params(*semantics):
    return pltpu.CompilerParams(dimension_semantics=semantics,
                                vmem_limit_bytes=V7X_VMEM_LIMIT_BYTES)


def _resident(shape):
    zeros = (0,) * len(shape)
    return pl.BlockSpec(shape, lambda *_: zeros, pipeline_mode=pl.Buffered(1))


def _layer_norm(z, gain, bias):
    mu = jnp.mean(z, axis=-1, keepdims=True)
    zc = z - mu
    var = jnp.mean(zc * zc, axis=-1, keepdims=True)
    return zc * lax.rsqrt(var + LN_EPS) * gain + bias


def _silu(x):
    return x / (1.0 + jnp.exp(-x))


def _dot(a, b):
    return jnp.dot(a, b, preferred_element_type=F32)


def _ret_in_kernel(x_ref, w_ref, inv_ref, q_ref, k_ref, v_ref, g_ref, *, tm, seq_len, pos_base):
    i = pl.program_id(0)
    xb = x_ref[...].astype(BF16)
    row = i * tm + lax.broadcasted_iota(jnp.int32, (tm, 1), 0)
    pos = (pos_base + (row & (seq_len - 1))).astype(F32)
    ang = pos * inv_ref[...]
    cos = jnp.cos(ang)
    sin = jnp.sin(ang)
    for dst, col0, scale in ((q_ref, 0, 1.0), (k_ref, RET_QK, RET_DK ** -0.5)):
        t = _dot(xb, w_ref[:, col0:col0 + RET_QK])
        for h in range(RET_HEADS):
            lo = h * RET_DK
            x1 = t[:, lo:lo + ROPE_HALF]
            x2 = t[:, lo + ROPE_HALF:lo + RET_DK]
            r1 = x1 * cos - x2 * sin
            r2 = x1 * sin + x2 * cos
            if scale != 1.0:
                r1, r2 = r1 * scale, r2 * scale
            dst[:, lo:lo + ROPE_HALF] = r1.astype(BF16)
            dst[:, lo + ROPE_HALF:lo + RET_DK] = r2.astype(BF16)
    v_ref[...] = _dot(xb, w_ref[:, 2 * RET_QK:2 * RET_QK + RET_V]).astype(BF16)
    g_ref[...] = _dot(xb, w_ref[:, 2 * RET_QK + RET_V:]).astype(BF16)


def _ret_in(x, w_bf, inv, *, seq_len, pos_base, tm):
    t = x.shape[0]
    assert t % tm == 0 and seq_len & (seq_len - 1) == 0
    row = lambda n: pl.BlockSpec((tm, n), lambda i: (i, 0))
    return pl.pallas_call(
        functools.partial(_ret_in_kernel, tm=tm, seq_len=seq_len, pos_base=pos_base),
        name="ret_in",
        grid=(t // tm,),
        in_specs=[row(D_MODEL), _resident((D_MODEL, RET_IN)), _resident((1, ROPE_HALF))],
        out_specs=[row(RET_QK), row(RET_QK), row(RET_V), row(RET_V)],
        out_shape=[jax.ShapeDtypeStruct((t, RET_QK), BF16), jax.ShapeDtypeStruct((t, RET_QK), BF16),
                   jax.ShapeDtypeStruct((t, RET_V), BF16), jax.ShapeDtypeStruct((t, RET_V), BF16)],
        compiler_params=_params("parallel"),
    )(x, w_bf, inv)


def _retention_kernel(lg_ref, q_ref, k_ref, v_ref, g_ref, r0_ref, o_ref, rout_ref,
                      state, decay, *pads, lb, l_in, chunk, nblk):
    b = pl.program_id(0)
    n = pl.program_id(1)
    padded = l_in < lb

    @pl.when((b == 0) & (n == 0))
    def _():
        ii = lax.broadcasted_iota(jnp.int32, (lb, lb), 0)
        jj = lax.broadcasted_iota(jnp.int32, (lb, lb), 1)
        shift = int(math.log2(chunk))
        visible = (jj >> shift) <= (ii >> shift)
        dist = jnp.abs(ii - jj).astype(F32)
        for h in range(RET_HEADS):
            decay[h] = jnp.where(visible, jnp.exp(lg_ref[h] * dist), 0.0)
        for p in pads:
            p[...] = jnp.zeros_like(p)

    @pl.when(n == 0)
    def _():
        state[...] = r0_ref[...]

    if padded:
        qp, kp, vp = pads
        qp[0:l_in, :] = q_ref[...]
        kp[0:l_in, :] = k_ref[...]
        vp[0:l_in, :] = v_ref[...]
        q_src, k_src, v_src = qp, kp, vp
    else:
        q_src, k_src, v_src = q_ref, k_ref, v_ref

    jcol = lax.broadcasted_iota(jnp.int32, (lb, 1), 0).astype(F32)
    for h in range(RET_HEADS):
        lg = lg_ref[h]
        qh = q_src[:, h * RET_DK:(h + 1) * RET_DK]
        kh = k_src[:, h * RET_DK:(h + 1) * RET_DK]
        vh = v_src[:, h * RET_DV:(h + 1) * RET_DV]
        r_old = state[h]
        s = lax.dot_general(qh, kh, (((1,), (1,)), ((), ())), preferred_element_type=F32)
        p = (s * decay[h]).astype(BF16)
        cross = jnp.exp(lg * (jcol + 1.0))
        o = _dot(p, vh) + _dot(qh, r_old.astype(BF16)) * cross
        k_dec = kh.astype(F32) * jnp.exp(lg * (l_in - 1.0 - jcol))
        block_decay = jnp.exp(jnp.full((1, RET_DV), lg * l_in, F32))
        state[h] = r_old * block_decay + _dot(k_dec.T.astype(BF16), vh)

        o = o[0:l_in]
        mu = jnp.mean(o, axis=-1, keepdims=True)
        oc = o - mu
        var = jnp.mean(oc * oc, axis=-1, keepdims=True)
        on = oc * lax.rsqrt(var + GN_EPS)
        gh = g_ref[:, h * RET_DV:(h + 1) * RET_DV].astype(F32)
        o_ref[:, h * RET_DV:(h + 1) * RET_DV] = (_silu(gh) * on).astype(BF16)

    @pl.when(n == nblk - 1)
    def _():
        rout_ref[...] = state[...]


def _retention(q, k, v, g, r0, log_g, *, n_seq, seq_len, lb):
    l_in = min(seq_len, lb)
    chunk = min(seq_len, CHUNK)
    assert seq_len % l_in == 0 and lb % chunk == 0 and chunk & (chunk - 1) == 0
    nblk = seq_len // l_in
    t = n_seq * seq_len
    row = lambda width: pl.BlockSpec((l_in, width), lambda b, n: (b * nblk + n, 0))
    st = pl.BlockSpec((None, RET_HEADS, RET_DK, RET_DV), lambda b, n: (b, 0, 0, 0))
    scratch = [pltpu.VMEM((RET_HEADS, RET_DK, RET_DV), F32), pltpu.VMEM((RET_HEADS, lb, lb), F32)]
    if l_in < lb:
        scratch += [pltpu.VMEM((lb, RET_QK), BF16), pltpu.VMEM((lb, RET_QK), BF16),
                    pltpu.VMEM((lb, RET_V), BF16)]
    return pl.pallas_call(
        functools.partial(_retention_kernel, lb=lb, l_in=l_in, chunk=chunk, nblk=nblk),
        name="retention",
        grid=(n_seq, nblk),
        in_specs=[pl.BlockSpec(memory_space=pltpu.SMEM), row(RET_QK), row(RET_QK), row(RET_V),
                  row(RET_V), st],
        out_specs=[row(RET_V), st],
        out_shape=[jax.ShapeDtypeStruct((t, RET_V), BF16),
                   jax.ShapeDtypeStruct((n_seq, RET_HEADS, RET_DK, RET_DV), F32)],
        scratch_shapes=scratch,
        compiler_params=_params("arbitrary", "arbitrary"),
    )(log_g, q, k, v, g, r0)


def _proj_ln_kernel(a_ref, w_ref, x_ref, gain_ref, bias_ref, o_ref):
    z = DN_ALPHA * x_ref[...] + _dot(a_ref[...], w_ref[...])
    o_ref[...] = _layer_norm(z, gain_ref[...], bias_ref[...])


def _proj_ln(a, w_bf, x, gain, bias, *, tm):
    t, kdim = a.shape
    assert t % tm == 0
    return pl.pallas_call(
        _proj_ln_kernel,
        name="proj_ln",
        grid=(t // tm,),
        in_specs=[pl.BlockSpec((tm, kdim), lambda i: (i, 0)), _resident((kdim, D_MODEL)),
                  pl.BlockSpec((tm, D_MODEL), lambda i: (i, 0)),
                  _resident((1, D_MODEL)), _resident((1, D_MODEL))],
        out_specs=pl.BlockSpec((tm, D_MODEL), lambda i: (i, 0)),
        out_shape=jax.ShapeDtypeStruct((t, D_MODEL), F32),
        compiler_params=_params("parallel"),
    )(a, w_bf, x, gain, bias)


def _ffn_ln_kernel(x_ref, wgu_ref, wd_ref, gain_ref, bias_ref, o_ref):
    x = x_ref[...]
    xb = x.astype(BF16)
    gate = _dot(xb, wgu_ref[:, 0:D_FF])
    up = _dot(xb, wgu_ref[:, D_FF:2 * D_FF])
    hid = (_silu(gate) * up).astype(BF16)
    z = DN_ALPHA * x + _dot(hid, wd_ref[...])
    o_ref[...] = _layer_norm(z, gain_ref[...], bias_ref[...])


def _ffn_ln(x, wgu_bf, wd_bf, gain, bias, *, tm):
    t = x.shape[0]
    assert t % tm == 0
    return pl.pallas_call(
        _ffn_ln_kernel,
        name="ffn_ln",
        grid=(t // tm,),
        in_specs=[pl.BlockSpec((tm, D_MODEL), lambda i: (i, 0)),
                  _resident((D_MODEL, 2 * D_FF)), _resident((D_FF, D_MODEL)),
                  _resident((1, D_MODEL)), _resident((1, D_MODEL))],
        out_specs=pl.BlockSpec((tm, D_MODEL), lambda i: (i, 0)),
        out_shape=jax.ShapeDtypeStruct((t, D_MODEL), F32),
        compiler_params=_params("parallel"),
    )(x, wgu_bf, wd_bf, gain, bias)


def _conv_ln_kernel(x_ref, win_ref, wc_ref, wout_ref, buf_ref, gain_ref, bias_ref,
                    o_ref, ob_ref, st_ref, u_win, prev, mixed, *, tm, seg, tiles_per_seq):
    i = pl.program_id(0)
    x = x_ref[...]
    xb = x.astype(BF16)
    gate_b = _dot(xb, win_ref[:, 0:D_MODEL])
    u = _dot(xb, win_ref[:, D_MODEL:2 * D_MODEL]) * _dot(xb, win_ref[:, 2 * D_MODEL:])
    w0 = wc_ref[0:1, :]
    w1 = wc_ref[1:2, :]
    w2 = wc_ref[2:3, :]
    tail = CONV_WIDTH - 1
    for s in range(tm // seg):
        if tiles_per_seq == 1:
            prev[SUBLANES - tail:SUBLANES, :] = buf_ref[s]
        else:
            @pl.when(i % tiles_per_seq == 0)
            def _():
                prev[SUBLANES - tail:SUBLANES, :] = buf_ref[0]
        u_win[0:SUBLANES, :] = prev[...]
        u_win[SUBLANES:SUBLANES + seg, :] = u[s * seg:(s + 1) * seg]
        conv = (w0 * u_win[SUBLANES - 2:SUBLANES - 2 + seg, :]
                + w1 * u_win[SUBLANES - 1:SUBLANES - 1 + seg, :]
                + w2 * u_win[SUBLANES:SUBLANES + seg, :])
        mixed[s * seg:(s + 1) * seg, :] = (gate_b[s * seg:(s + 1) * seg] * conv).astype(BF16)
        prev[...] = u_win[seg:seg + SUBLANES, :]
        st_ref[s] = u_win[SUBLANES + seg - tail:SUBLANES + seg, :]
    z = DN_ALPHA * x + _dot(mixed[...], wout_ref[...])
    out = _layer_norm(z, gain_ref[...], bias_ref[...])
    o_ref[...] = out
    ob_ref[...] = out.astype(BF16)


def _conv_ln(x, win_bf, wconv, wout_bf, buf, gain, bias, *, n_seq, seq_len, tm):
    t = x.shape[0]
    seg = min(seq_len, tm)
    assert t % tm == 0 and tm % seg == 0 and seq_len % seg == 0 and seg % SUBLANES == 0
    tiles_per_seq = seq_len // seg
    seq_per_tile = tm // seg
    tail = CONV_WIDTH - 1
    if tiles_per_seq == 1:
        st_map = lambda i: (i, 0, 0)
    else:
        st_map = lambda i: (i // tiles_per_seq, 0, 0)
    st_spec = pl.BlockSpec((seq_per_tile, tail, D_MODEL), st_map)
    return pl.pallas_call(
        functools.partial(_conv_ln_kernel, tm=tm, seg=seg, tiles_per_seq=tiles_per_seq),
        name="conv_ln",
        grid=(t // tm,),
        in_specs=[pl.BlockSpec((tm, D_MODEL), lambda i: (i, 0)),
                  _resident((D_MODEL, 3 * D_MODEL)), _resident((CONV_WIDTH, D_MODEL)),
                  _resident((D_MODEL, D_MODEL)), st_spec,
                  _resident((1, D_MODEL)), _resident((1, D_MODEL))],
        out_specs=[pl.BlockSpec((tm, D_MODEL), lambda i: (i, 0)),
                   pl.BlockSpec((tm, D_MODEL), lambda i: (i, 0)), st_spec],
        out_shape=[jax.ShapeDtypeStruct((t, D_MODEL), F32),
                   jax.ShapeDtypeStruct((t, D_MODEL), BF16),
                   jax.ShapeDtypeStruct((n_seq, tail, D_MODEL), F32)],
        scratch_shapes=[pltpu.VMEM((SUBLANES + seg, D_MODEL), F32),
                        pltpu.VMEM((SUBLANES, D_MODEL), F32),
                        pltpu.VMEM((tm, D_MODEL), BF16)],
        compiler_params=_params("arbitrary"),
    )(x, win_bf, wconv, wout_bf, buf, gain, bias)


def _split_bf16(a):
    hi = a.astype(BF16)
    lo = (a - hi.astype(F32)).astype(BF16)
    return hi, lo


def _router_kernel(x_ref, wr_ref, rank_ref, gate_ref, cnt_ref, *, tm):
    x_hi, x_lo = _split_bf16(x_ref[...])
    w_hi, w_lo = _split_bf16(wr_ref[...])
    logits = _dot(x_hi, w_hi) + (_dot(x_lo, w_hi) + _dot(x_hi, w_lo))
    lt = logits.T[0:N_EXPERTS, :]
    eidx = lax.broadcasted_iota(jnp.int32, (N_EXPERTS, tm), 0).astype(F32)
    neg = jnp.float32(-jnp.inf)
    m1 = jnp.max(lt, axis=0, keepdims=True)
    i1 = jnp.min(jnp.where(lt == m1, eidx, float(N_EXPERTS)), axis=0, keepdims=True)
    first = eidx == i1
    rest = jnp.where(first, neg, lt)
    m2 = jnp.max(rest, axis=0, keepdims=True)
    i2 = jnp.min(jnp.where(rest == m2, eidx, float(N_EXPERTS)), axis=0, keepdims=True)
    second = eidx == i2
    e2 = jnp.exp(m2 - m1)
    w1 = 1.0 / (1.0 + e2)
    w2 = e2 / (1.0 + e2)
    gate_ref[...] = jnp.where(first, w1, 0.0) + jnp.where(second, w2, 0.0)
    sel = first | second
    self32 = jnp.where(sel, 1.0, 0.0)

    c = CUMSUM_CHUNK if tm % CUMSUM_CHUNK == 0 else tm
    si = lax.broadcasted_iota(jnp.int32, (c, c), 0)
    ti = lax.broadcasted_iota(jnp.int32, (c, c), 1)
    tri = jnp.where(si < ti, 1.0, 0.0).astype(BF16)
    offset = jnp.zeros((N_EXPERTS, 1), F32)
    for j in range(tm // c):
        blk = self32[:, j * c:(j + 1) * c]
        rank = _dot(blk.astype(BF16), tri) + offset
        rank_ref[:, j * c:(j + 1) * c] = jnp.where(sel[:, j * c:(j + 1) * c],
                                                   rank.astype(jnp.int32), -1)
        offset = offset + jnp.sum(blk, axis=1, keepdims=True)
    cnt_ref[...] = jnp.broadcast_to(offset.astype(jnp.int32), (N_EXPERTS, LANES))


def _router(x, wr_pad, *, tm):
    t = x.shape[0]
    assert t % tm == 0
    nt = t // tm
    return pl.pallas_call(
        functools.partial(_router_kernel, tm=tm),
        name="router",
        grid=(nt,),
        in_specs=[pl.BlockSpec((tm, D_MODEL), lambda i: (i, 0)), _resident((D_MODEL, LANES))],
        out_specs=[pl.BlockSpec((N_EXPERTS, tm), lambda i: (0, i)),
                   pl.BlockSpec((N_EXPERTS, tm), lambda i: (0, i)),
                   pl.BlockSpec((None, N_EXPERTS, LANES), lambda i: (i, 0, 0))],
        out_shape=[jax.ShapeDtypeStruct((N_EXPERTS, t), jnp.int32),
                   jax.ShapeDtypeStruct((N_EXPERTS, t), F32),
                   jax.ShapeDtypeStruct((nt, N_EXPERTS, LANES), jnp.int32)],
        compiler_params=_params("parallel"),
    )(x, wr_pad)


def _moe_ln_kernel(cnt_ref, xb_ref, x_ref, acc_ref, rank_ref, gate_ref, wgu_ref, wd_ref,
                   gain_ref, bias_ref, o_ref, *, tm, single_tile):
    e = pl.program_id(0)
    i = pl.program_id(1)
    ne = pl.num_programs(0)

    @pl.when(e == 0)
    def _():
        o_ref[...] = jnp.zeros_like(o_ref)

    if not single_tile:
        @pl.when(e > 0)
        def _():
            o_ref[...] = acc_ref[...]

    count = cnt_ref[i * N_EXPERTS + e]
    rank = rank_ref[pl.ds(e, 1), :]
    gate = gate_ref[pl.ds(e, 1), :]

    def expert_rows(row0, nrows):
        rows = row0 + lax.broadcasted_iota(jnp.int32, (nrows, 1), 0)
        hit = rank == rows
        onehot = jnp.where(hit, 1.0, 0.0)
        xg = _dot(onehot.astype(BF16), xb_ref[...]).astype(BF16)
        gcol = jnp.sum(jnp.where(hit, gate, 0.0), axis=1, keepdims=True)
        hid = (_silu(_dot(xg, wgu_ref[:, 0:D_FF])) * _dot(xg, wgu_ref[:, D_FF:])).astype(BF16)
        y = (_dot(hid, wd_ref[...]) * gcol).astype(BF16)
        o_ref[...] += _dot(onehot.T.astype(BF16), y)

    def full_block(r, carry):
        expert_rows(r * MOE_ROWS, MOE_ROWS)
        return carry

    n_full = lax.shift_right_logical(count, MOE_ROWS_LOG2)
    lax.fori_loop(0, n_full, full_block, 0)
    base = n_full * MOE_ROWS
    rem = count - base
    lo = 0
    for nrows in MOE_TAIL_ROWS:
        @pl.when((rem > lo) & (rem <= nrows))
        def _(nrows=nrows):
            expert_rows(base, nrows)
        lo = nrows

    @pl.when(e == ne - 1)
    def _():
        z = DN_ALPHA * x_ref[...] + o_ref[...]
        o_ref[...] = _layer_norm(z, gain_ref[...], bias_ref[...])


def _moe_ln(xb, x, rank, gate, counts, wgu_bf, wd_bf, gain, bias, *, tm):
    t = x.shape[0]
    assert t % tm == 0
    nt = t // tm
    assert nt == 1 or nt >= 3
    last_e = N_EXPERTS - 1
    tile = lambda e, i, c: (i, 0)
    x_map = lambda e, i, c: (jnp.where(e == last_e, i, 0), 0)
    acc_map = lambda e, i, c: (jnp.where(e == 0, nt - 1, i), 0)
    per_tile = pl.BlockSpec((N_EXPERTS, tm), lambda e, i, c: (0, i))
    const = lambda shape: pl.BlockSpec(shape, lambda e, i, c: (0,) * len(shape),
                                       pipeline_mode=pl.Buffered(1))
    grid_spec = pltpu.PrefetchScalarGridSpec(
        num_scalar_prefetch=1,
        grid=(N_EXPERTS, nt),
        in_specs=[pl.BlockSpec((tm, D_MODEL), tile),
                  pl.BlockSpec((tm, D_MODEL), x_map, pipeline_mode=pl.Buffered(1)),
                  pl.BlockSpec((tm, D_MODEL), acc_map), per_tile, per_tile,
                  pl.BlockSpec((None, D_MODEL, 2 * D_FF), lambda e, i, c: (e, 0, 0),
                               pipeline_mode=pl.Buffered(1)),
                  pl.BlockSpec((None, D_FF, D_MODEL), lambda e, i, c: (e, 0, 0),
                               pipeline_mode=pl.Buffered(1)),
                  const((1, D_MODEL)), const((1, D_MODEL))],
        out_specs=pl.BlockSpec((tm, D_MODEL), tile),
    )
    acc0 = jnp.zeros((t, D_MODEL), F32)
    return pl.pallas_call(
        functools.partial(_moe_ln_kernel, tm=tm, single_tile=nt == 1),
        name="moe_ln",
        grid_spec=grid_spec,
        out_shape=jax.ShapeDtypeStruct((t, D_MODEL), F32),
        input_output_aliases={3: 0},
        compiler_params=_params("arbitrary", "arbitrary"),
    )(counts, xb, x, acc0, rank, gate, wgu_bf, wd_bf, gain, bias)


def _trunk(x, n_seq, seq_len, pos_base, r0, conv_buf, w, *, tm, lb, tm_moe):
    q, k, v, g = _ret_in(x, w["ret_in"], w["inv"], seq_len=seq_len, pos_base=pos_base, tm=tm)
    gated, r_new = _retention(q, k, v, g, r0, w["log_g"], n_seq=n_seq, seq_len=seq_len, lb=lb)
    x = _proj_ln(gated, w["ret_out"], x, w["gain"][0], w["bias"][0], tm=tm)
    x = _ffn_ln(x, w["ffn_gu"], w["ffn_down"], w["gain"][1], w["bias"][1], tm=tm)
    x, xb, conv_new = _conv_ln(x, w["conv_in"], w["conv_w"], w["conv_out"], conv_buf,
                               w["gain"][2], w["bias"][2], n_seq=n_seq, seq_len=seq_len, tm=tm)
    rank, gate, counts = _router(x, w["router"], tm=tm_moe)
    y = _moe_ln(xb, x, rank, gate, counts[:, :, 0].reshape(-1), w["moe_gu"], w["moe_down"],
                w["gain"][3], w["bias"][3], tm=tm_moe)
    return y, r_new, conv_new


def kernel(x_prompt, x_sample, state_ret, state_conv, ret_w_in, ret_w_out, conv_w_in, conv_w,
           conv_w_out, ffn_w_gu, ffn_w_down, moe_w_router, moe_w_gu, moe_w_down, ln_gain, ln_bias):
    batch, seq, _ = x_prompt.shape
    dec_batch, dec_seq, _ = x_sample.shape

    w = {
        "ret_in": ret_w_in[0].astype(BF16),
        "ret_out": ret_w_out[0].astype(BF16),
        "conv_in": conv_w_in[0].astype(BF16),
        "conv_w": conv_w[0],
        "conv_out": conv_w_out[0].astype(BF16),
        "ffn_gu": ffn_w_gu[0].astype(BF16),
        "ffn_down": ffn_w_down[0].astype(BF16),
        "router": jnp.pad(moe_w_router[0], ((0, 0), (0, LANES - N_EXPERTS))),
        "moe_gu": moe_w_gu[0].astype(BF16),
        "moe_down": moe_w_down[0].astype(BF16),
        "gain": ln_gain.reshape(2 * DEPTH, 1, D_MODEL),
        "bias": ln_bias.reshape(2 * DEPTH, 1, D_MODEL),
        "inv": (1.0 / (ROPE_BASE ** jnp.linspace(0.0, 1.0, ROPE_HALF, dtype=F32))).reshape(1, ROPE_HALF),
        "log_g": jnp.log1p(-jnp.exp2(-5.0 - jnp.arange(RET_HEADS, dtype=F32))),
    }

    zero_ret = jnp.zeros((batch, RET_HEADS, RET_DK, RET_DV), F32)
    zero_conv = jnp.zeros((batch, CONV_WIDTH - 1, D_MODEL), F32)
    y_p, ret_p, conv_p = _trunk(x_prompt.reshape(batch * seq, D_MODEL), batch, seq, 0,
                                zero_ret, zero_conv, w, tm=512, lb=256, tm_moe=1024)
    y_s, ret_s, conv_s = _trunk(x_sample.reshape(dec_batch * dec_seq, D_MODEL), dec_batch, dec_seq,
                                PAST_LEN, state_ret[0], state_conv[0], w,
                                tm=dec_batch * dec_seq, lb=LANES, tm_moe=dec_batch * dec_seq)
    return (y_p.reshape(batch, seq, D_MODEL), y_s.reshape(dec_batch, dec_seq, D_MODEL),
            ret_p[None], ret_s[None], conv_p[None], conv_s[None])
```

```python
import functools
import math

import jax
import jax.numpy as jnp
from jax import lax
from jax.experimental import pallas as pl
from jax.experimental.pallas import tpu as pltpu

D_MODEL = 1024
CHUNK = 64
RET_HEADS = 4
RET_DK = D_MODEL // RET_HEADS
RET_DV = 2 * D_MODEL // RET_HEADS
RET_QK = RET_HEADS * RET_DK
RET_V = RET_HEADS * RET_DV
RET_IN = 2 * RET_QK + 2 * RET_V
ROPE_HALF = RET_DK // 2
CONV_WIDTH = 3
D_FF = 2816
N_EXPERTS = 8
ROPE_BASE = 10000.0
LN_EPS = 1e-5
GN_EPS = 1e-6
DEPTH = 2
DN_ALPHA = (2 * DEPTH) ** 0.25
PAST_LEN = 4096

V7X_VMEM_LIMIT_BYTES = 58 * 1024 * 1024
LANES = 128
SUBLANES = 8

MOE_ROWS_LOG2 = 8
MOE_ROWS = 1 << MOE_ROWS_LOG2
MOE_TAIL_ROWS = (128, MOE_ROWS)
MOE_SINGLE_ROWS = (128, MOE_ROWS, MOE_ROWS + 64)
CUMSUM_CHUNK = 512

F32 = jnp.float32
BF16 = jnp.bfloat16


def _params(*semantics):
    return pltpu.CompilerParams(dimension_semantics=semantics,
                                vmem_limit_bytes=V7X_VMEM_LIMIT_BYTES)


def _resident(shape):
    zeros = (0,) * len(shape)
    return pl.BlockSpec(shape, lambda *_: zeros, pipeline_mode=pl.Buffered(1))


def _layer_norm(z, gain, bias):
    mu = jnp.mean(z, axis=-1, keepdims=True)
    zc = z - mu
    var = jnp.mean(zc * zc, axis=-1, keepdims=True)
    return zc * lax.rsqrt(var + LN_EPS) * gain + bias


def _silu(x):
    return x / (1.0 + jnp.exp(-x))


def _dot(a, b):
    return jnp.dot(a, b, preferred_element_type=F32)


def _ret_in_kernel(x_ref, w_ref, inv_ref, q_ref, k_ref, v_ref, g_ref, *, tm, seq_len, pos_base):
    i = pl.program_id(0)
    xb = x_ref[...].astype(BF16)
    row = i * tm + lax.broadcasted_iota(jnp.int32, (tm, 1), 0)
    pos = (pos_base + (row & (seq_len - 1))).astype(F32)
    ang = pos * inv_ref[...]
    cos = jnp.cos(ang)
    sin = jnp.sin(ang)
    for dst, col0, scale in ((q_ref, 0, 1.0), (k_ref, RET_QK, RET_DK ** -0.5)):
        t = _dot(xb, w_ref[:, col0:col0 + RET_QK])
        for h in range(RET_HEADS):
            lo = h * RET_DK
            x1 = t[:, lo:lo + ROPE_HALF]
            x2 = t[:, lo + ROPE_HALF:lo + RET_DK]
            r1 = x1 * cos - x2 * sin
            r2 = x1 * sin + x2 * cos
            if scale != 1.0:
                r1, r2 = r1 * scale, r2 * scale
            dst[:, lo:lo + ROPE_HALF] = r1.astype(BF16)
            dst[:, lo + ROPE_HALF:lo + RET_DK] = r2.astype(BF16)
    v_ref[...] = _dot(xb, w_ref[:, 2 * RET_QK:2 * RET_QK + RET_V]).astype(BF16)
    g_ref[...] = _dot(xb, w_ref[:, 2 * RET_QK + RET_V:]).astype(BF16)


def _ret_in(x, w_bf, inv, *, seq_len, pos_base, tm):
    t = x.shape[0]
    assert t % tm == 0 and seq_len & (seq_len - 1) == 0
    row = lambda n: pl.BlockSpec((tm, n), lambda i: (i, 0))
    return pl.pallas_call(
        functools.partial(_ret_in_kernel, tm=tm, seq_len=seq_len, pos_base=pos_base),
        name="ret_in",
        grid=(t // tm,),
        in_specs=[row(D_MODEL), _resident((D_MODEL, RET_IN)), _resident((1, ROPE_HALF))],
        out_specs=[row(RET_QK), row(RET_QK), row(RET_V), row(RET_V)],
        out_shape=[jax.ShapeDtypeStruct((t, RET_QK), BF16), jax.ShapeDtypeStruct((t, RET_QK), BF16),
                   jax.ShapeDtypeStruct((t, RET_V), BF16), jax.ShapeDtypeStruct((t, RET_V), BF16)],
        compiler_params=_params("parallel"),
    )(x, w_bf, inv)


def _retention_kernel(lg_ref, q_ref, k_ref, v_ref, g_ref, r0_ref, o_ref, rout_ref,
                      state, decay, *pads, lb, l_in, chunk, nblk):
    b = pl.program_id(0)
    n = pl.program_id(1)
    padded = l_in < lb

    @pl.when((b == 0) & (n == 0))
    def _():
        ii = lax.broadcasted_iota(jnp.int32, (lb, lb), 0)
        jj = lax.broadcasted_iota(jnp.int32, (lb, lb), 1)
        shift = int(math.log2(chunk))
        visible = (jj >> shift) <= (ii >> shift)
        dist = jnp.abs(ii - jj).astype(F32)
        for h in range(RET_HEADS):
            decay[h] = jnp.where(visible, jnp.exp(lg_ref[h] * dist), 0.0)
        for p in pads:
            p[...] = jnp.zeros_like(p)

    @pl.when(n == 0)
    def _():
        state[...] = r0_ref[...]

    if padded:
        qp, kp, vp = pads
        qp[0:l_in, :] = q_ref[...]
        kp[0:l_in, :] = k_ref[...]
        vp[0:l_in, :] = v_ref[...]
        q_src, k_src, v_src = qp, kp, vp
    else:
        q_src, k_src, v_src = q_ref, k_ref, v_ref

    jcol = lax.broadcasted_iota(jnp.int32, (lb, 1), 0).astype(F32)
    for h in range(RET_HEADS):
        lg = lg_ref[h]
        qh = q_src[:, h * RET_DK:(h + 1) * RET_DK]
        kh = k_src[:, h * RET_DK:(h + 1) * RET_DK]
        vh = v_src[:, h * RET_DV:(h + 1) * RET_DV]
        r_old = state[h]
        s = lax.dot_general(qh, kh, (((1,), (1,)), ((), ())), preferred_element_type=F32)
        p = (s * decay[h]).astype(BF16)
        cross = jnp.exp(lg * (jcol + 1.0))
        o = _dot(p, vh) + _dot(qh, r_old.astype(BF16)) * cross
        k_dec = kh.astype(F32) * jnp.exp(lg * (l_in - 1.0 - jcol))
        block_decay = jnp.exp(jnp.full((1, RET_DV), lg * l_in, F32))
        state[h] = r_old * block_decay + _dot(k_dec.T.astype(BF16), vh)

        o = o[0:l_in]
        mu = jnp.mean(o, axis=-1, keepdims=True)
        oc = o - mu
        var = jnp.mean(oc * oc, axis=-1, keepdims=True)
        on = oc * lax.rsqrt(var + GN_EPS)
        gh = g_ref[:, h * RET_DV:(h + 1) * RET_DV].astype(F32)
        o_ref[:, h * RET_DV:(h + 1) * RET_DV] = (_silu(gh) * on).astype(BF16)

    @pl.when(n == nblk - 1)
    def _():
        rout_ref[...] = state[...]


def _retention(q, k, v, g, r0, log_g, *, n_seq, seq_len, lb):
    l_in = min(seq_len, lb)
    chunk = min(seq_len, CHUNK)
    assert seq_len % l_in == 0 and lb % chunk == 0 and chunk & (chunk - 1) == 0
    nblk = seq_len // l_in
    t = n_seq * seq_len
    row = lambda width: pl.BlockSpec((l_in, width), lambda b, n: (b * nblk + n, 0))
    st = pl.BlockSpec((None, RET_HEADS, RET_DK, RET_DV), lambda b, n: (b, 0, 0, 0))
    scratch = [pltpu.VMEM((RET_HEADS, RET_DK, RET_DV), F32), pltpu.VMEM((RET_HEADS, lb, lb), F32)]
    if l_in < lb:
        scratch += [pltpu.VMEM((lb, RET_QK), BF16), pltpu.VMEM((lb, RET_QK), BF16),
                    pltpu.VMEM((lb, RET_V), BF16)]
    return pl.pallas_call(
        functools.partial(_retention_kernel, lb=lb, l_in=l_in, chunk=chunk, nblk=nblk),
        name="retention",
        grid=(n_seq, nblk),
        in_specs=[pl.BlockSpec(memory_space=pltpu.SMEM), row(RET_QK), row(RET_QK), row(RET_V),
                  row(RET_V), st],
        out_specs=[row(RET_V), st],
        out_shape=[jax.ShapeDtypeStruct((t, RET_V), BF16),
                   jax.ShapeDtypeStruct((n_seq, RET_HEADS, RET_DK, RET_DV), F32)],
        scratch_shapes=scratch,
        compiler_params=_params("arbitrary", "arbitrary"),
    )(log_g, q, k, v, g, r0)


def _proj_ln_kernel(a_ref, w_ref, x_ref, gain_ref, bias_ref, o_ref):
    z = DN_ALPHA * x_ref[...] + _dot(a_ref[...], w_ref[...])
    o_ref[...] = _layer_norm(z, gain_ref[...], bias_ref[...])


def _proj_ln(a, w_bf, x, gain, bias, *, tm):
    t, kdim = a.shape
    assert t % tm == 0
    return pl.pallas_call(
        _proj_ln_kernel,
        name="proj_ln",
        grid=(t // tm,),
        in_specs=[pl.BlockSpec((tm, kdim), lambda i: (i, 0)), _resident((kdim, D_MODEL)),
                  pl.BlockSpec((tm, D_MODEL), lambda i: (i, 0)),
                  _resident((1, D_MODEL)), _resident((1, D_MODEL))],
        out_specs=pl.BlockSpec((tm, D_MODEL), lambda i: (i, 0)),
        out_shape=jax.ShapeDtypeStruct((t, D_MODEL), F32),
        compiler_params=_params("parallel"),
    )(a, w_bf, x, gain, bias)


def _ffn_ln_kernel(x_ref, wgu_ref, wd_ref, gain_ref, bias_ref, o_ref):
    x = x_ref[...]
    xb = x.astype(BF16)
    gate = _dot(xb, wgu_ref[:, 0:D_FF])
    up = _dot(xb, wgu_ref[:, D_FF:2 * D_FF])
    hid = (_silu(gate) * up).astype(BF16)
    z = DN_ALPHA * x + _dot(hid, wd_ref[...])
    o_ref[...] = _layer_norm(z, gain_ref[...], bias_ref[...])


def _ffn_ln(x, wgu_bf, wd_bf, gain, bias, *, tm):
    t = x.shape[0]
    assert t % tm == 0
    return pl.pallas_call(
        _ffn_ln_kernel,
        name="ffn_ln",
        grid=(t // tm,),
        in_specs=[pl.BlockSpec((tm, D_MODEL), lambda i: (i, 0)),
                  _resident((D_MODEL, 2 * D_FF)), _resident((D_FF, D_MODEL)),
                  _resident((1, D_MODEL)), _resident((1, D_MODEL))],
        out_specs=pl.BlockSpec((tm, D_MODEL), lambda i: (i, 0)),
        out_shape=jax.ShapeDtypeStruct((t, D_MODEL), F32),
        compiler_params=_params("parallel"),
    )(x, wgu_bf, wd_bf, gain, bias)


def _conv_ln_kernel(x_ref, win_ref, wc_ref, wout_ref, buf_ref, gain_ref, bias_ref,
                    o_ref, ob_ref, st_ref, u_win, prev, mixed, *, tm, seg, tiles_per_seq):
    i = pl.program_id(0)
    x = x_ref[...]
    xb = x.astype(BF16)
    gate_b = _dot(xb, win_ref[:, 0:D_MODEL])
    u = _dot(xb, win_ref[:, D_MODEL:2 * D_MODEL]) * _dot(xb, win_ref[:, 2 * D_MODEL:])
    w0 = wc_ref[0:1, :]
    w1 = wc_ref[1:2, :]
    w2 = wc_ref[2:3, :]
    tail = CONV_WIDTH - 1
    for s in range(tm // seg):
        if tiles_per_seq == 1:
            prev[SUBLANES - tail:SUBLANES, :] = buf_ref[s]
        else:
            @pl.when(i % tiles_per_seq == 0)
            def _():
                prev[SUBLANES - tail:SUBLANES, :] = buf_ref[0]
        u_win[0:SUBLANES, :] = prev[...]
        u_win[SUBLANES:SUBLANES + seg, :] = u[s * seg:(s + 1) * seg]
        conv = (w0 * u_win[SUBLANES - 2:SUBLANES - 2 + seg, :]
                + w1 * u_win[SUBLANES - 1:SUBLANES - 1 + seg, :]
                + w2 * u_win[SUBLANES:SUBLANES + seg, :])
        mixed[s * seg:(s + 1) * seg, :] = (gate_b[s * seg:(s + 1) * seg] * conv).astype(BF16)
        prev[...] = u_win[seg:seg + SUBLANES, :]
        st_ref[s] = u_win[SUBLANES + seg - tail:SUBLANES + seg, :]
    z = DN_ALPHA * x + _dot(mixed[...], wout_ref[...])
    out = _layer_norm(z, gain_ref[...], bias_ref[...])
    o_ref[...] = out
    ob_ref[...] = out.astype(BF16)


def _conv_ln(x, win_bf, wconv, wout_bf, buf, gain, bias, *, n_seq, seq_len, tm):
    t = x.shape[0]
    seg = min(seq_len, tm)
    assert t % tm == 0 and tm % seg == 0 and seq_len % seg == 0 and seg % SUBLANES == 0
    tiles_per_seq = seq_len // seg
    seq_per_tile = tm // seg
    tail = CONV_WIDTH - 1
    if tiles_per_seq == 1:
        st_map = lambda i: (i, 0, 0)
    else:
        st_map = lambda i: (i // tiles_per_seq, 0, 0)
    st_spec = pl.BlockSpec((seq_per_tile, tail, D_MODEL), st_map)
    return pl.pallas_call(
        functools.partial(_conv_ln_kernel, tm=tm, seg=seg, tiles_per_seq=tiles_per_seq),
        name="conv_ln",
        grid=(t // tm,),
        in_specs=[pl.BlockSpec((tm, D_MODEL), lambda i: (i, 0)),
                  _resident((D_MODEL, 3 * D_MODEL)), _resident((CONV_WIDTH, D_MODEL)),
                  _resident((D_MODEL, D_MODEL)), st_spec,
                  _resident((1, D_MODEL)), _resident((1, D_MODEL))],
        out_specs=[pl.BlockSpec((tm, D_MODEL), lambda i: (i, 0)),
                   pl.BlockSpec((tm, D_MODEL), lambda i: (i, 0)), st_spec],
        out_shape=[jax.ShapeDtypeStruct((t, D_MODEL), F32),
                   jax.ShapeDtypeStruct((t, D_MODEL), BF16),
                   jax.ShapeDtypeStruct((n_seq, tail, D_MODEL), F32)],
        scratch_shapes=[pltpu.VMEM((SUBLANES + seg, D_MODEL), F32),
                        pltpu.VMEM((SUBLANES, D_MODEL), F32),
                        pltpu.VMEM((tm, D_MODEL), BF16)],
        compiler_params=_params("arbitrary"),
    )(x, win_bf, wconv, wout_bf, buf, gain, bias)


def _split_bf16(a):
    hi = a.astype(BF16)
    lo = (a - hi.astype(F32)).astype(BF16)
    return hi, lo


def _router_kernel(x_ref, wr_ref, rank_ref, gate_ref, cnt_ref, *, tm):
    x_hi, x_lo = _split_bf16(x_ref[...])
    w_hi, w_lo = _split_bf16(wr_ref[...])
    logits = _dot(x_hi, w_hi) + (_dot(x_lo, w_hi) + _dot(x_hi, w_lo))
    lt = logits.T[0:N_EXPERTS, :]
    eidx = lax.broadcasted_iota(jnp.int32, (N_EXPERTS, tm), 0).astype(F32)
    neg = jnp.float32(-jnp.inf)
    m1 = jnp.max(lt, axis=0, keepdims=True)
    i1 = jnp.min(jnp.where(lt == m1, eidx, float(N_EXPERTS)), axis=0, keepdims=True)
    first = eidx == i1
    rest = jnp.where(first, neg, lt)
    m2 = jnp.max(rest, axis=0, keepdims=True)
    i2 = jnp.min(jnp.where(rest == m2, eidx, float(N_EXPERTS)), axis=0, keepdims=True)
    second = eidx == i2
    e2 = jnp.exp(m2 - m1)
    w1 = 1.0 / (1.0 + e2)
    w2 = e2 / (1.0 + e2)
    gate_ref[...] = jnp.where(first, w1, 0.0) + jnp.where(second, w2, 0.0)
    sel = first | second
    self32 = jnp.where(sel, 1.0, 0.0)

    c = CUMSUM_CHUNK if tm % CUMSUM_CHUNK == 0 else tm
    si = lax.broadcasted_iota(jnp.int32, (c, c), 0)
    ti = lax.broadcasted_iota(jnp.int32, (c, c), 1)
    tri = jnp.where(si < ti, 1.0, 0.0).astype(BF16)
    offset = jnp.zeros((N_EXPERTS, 1), F32)
    for j in range(tm // c):
        blk = self32[:, j * c:(j + 1) * c]
        rank = _dot(blk.astype(BF16), tri) + offset
        rank_ref[:, j * c:(j + 1) * c] = jnp.where(sel[:, j * c:(j + 1) * c],
                                                   rank.astype(jnp.int32), -1)
        offset = offset + jnp.sum(blk, axis=1, keepdims=True)
    cnt_ref[...] = jnp.broadcast_to(offset.astype(jnp.int32), (N_EXPERTS, LANES))


def _router(x, wr_pad, *, tm):
    t = x.shape[0]
    assert t % tm == 0
    nt = t // tm
    return pl.pallas_call(
        functools.partial(_router_kernel, tm=tm),
        name="router",
        grid=(nt,),
        in_specs=[pl.BlockSpec((tm, D_MODEL), lambda i: (i, 0)), _resident((D_MODEL, LANES))],
        out_specs=[pl.BlockSpec((N_EXPERTS, tm), lambda i: (0, i)),
                   pl.BlockSpec((N_EXPERTS, tm), lambda i: (0, i)),
                   pl.BlockSpec((None, N_EXPERTS, LANES), lambda i: (i, 0, 0))],
        out_shape=[jax.ShapeDtypeStruct((N_EXPERTS, t), jnp.int32),
                   jax.ShapeDtypeStruct((N_EXPERTS, t), F32),
                   jax.ShapeDtypeStruct((nt, N_EXPERTS, LANES), jnp.int32)],
        compiler_params=_params("parallel"),
    )(x, wr_pad)


def _moe_ln_kernel(cnt_ref, xb_ref, x_ref, acc_ref, rank_ref, gate_ref, wgu_ref, wd_ref,
                   gain_ref, bias_ref, o_ref, *, tm, single_tile):
    e = pl.program_id(0)
    i = pl.program_id(1)
    ne = pl.num_programs(0)

    if single_tile:
        @pl.when(e == 0)
        def _():
            o_ref[...] = jnp.zeros_like(o_ref)

    def earlier_experts():
        if single_tile:
            return o_ref[...]
        return jnp.where(e > 0, acc_ref[...], 0.0)

    count = cnt_ref[i * N_EXPERTS + e]
    rank = rank_ref[pl.ds(e, 1), :]
    gate = gate_ref[pl.ds(e, 1), :]

    def expert_rows(row0, nrows, first):
        rows = row0 + lax.broadcasted_iota(jnp.int32, (nrows, 1), 0)
        hit = rank == rows
        onehot = jnp.where(hit, 1.0, 0.0)
        xg = _dot(onehot.astype(BF16), xb_ref[...]).astype(BF16)
        gcol = jnp.sum(jnp.where(hit, gate, 0.0), axis=1, keepdims=True)
        hid = (_silu(_dot(xg, wgu_ref[:, 0:D_FF])) * _dot(xg, wgu_ref[:, D_FF:])).astype(BF16)
        y = (_dot(hid, wd_ref[...]) * gcol).astype(BF16)
        scattered = _dot(onehot.T.astype(BF16), y)
        o_ref[...] = (earlier_experts() if first else o_ref[...]) + scattered

    lo = 0
    for nrows in MOE_SINGLE_ROWS:
        @pl.when((count > lo) & (count <= nrows))
        def _(nrows=nrows):
            expert_rows(0, nrows, True)
        lo = nrows

    @pl.when(count == 0)
    def _():
        o_ref[...] = earlier_experts()

    @pl.when(count > MOE_SINGLE_ROWS[-1])
    def _():
        o_ref[...] = earlier_experts()

        def full_block(r, carry):
            expert_rows(r * MOE_ROWS, MOE_ROWS, False)
            return carry

        n_full = lax.shift_right_logical(count, MOE_ROWS_LOG2)
        lax.fori_loop(0, n_full, full_block, 0)
        base = n_full * MOE_ROWS
        rem = count - base
        lo = 0
        for nrows in MOE_TAIL_ROWS:
            @pl.when((rem > lo) & (rem <= nrows))
            def _(nrows=nrows, lo=lo):
                expert_rows(base, nrows, False)
            lo = nrows

    @pl.when(e == ne - 1)
    def _():
        z = DN_ALPHA * x_ref[...] + o_ref[...]
        o_ref[...] = _layer_norm(z, gain_ref[...], bias_ref[...])


def _moe_ln(xb, x, acc, rank, gate, counts, wgu_bf, wd_bf, gain, bias, *, tm):
    t = x.shape[0]
    assert t % tm == 0
    nt = t // tm
    assert nt == 1 or nt >= 3
    last_e = N_EXPERTS - 1
    tile = lambda e, i, c: (i, 0)
    x_map = lambda e, i, c: (jnp.where(e == last_e, i, 0), 0)
    acc_map = lambda e, i, c: (jnp.where(e == 0, nt - 1, i), 0)
    per_tile = pl.BlockSpec((N_EXPERTS, tm), lambda e, i, c: (0, i))
    const = lambda shape: pl.BlockSpec(shape, lambda e, i, c: (0,) * len(shape),
                                       pipeline_mode=pl.Buffered(1))
    grid_spec = pltpu.PrefetchScalarGridSpec(
        num_scalar_prefetch=1,
        grid=(N_EXPERTS, nt),
        in_specs=[pl.BlockSpec((tm, D_MODEL), tile),
                  pl.BlockSpec((tm, D_MODEL), x_map, pipeline_mode=pl.Buffered(1)),
                  pl.BlockSpec((tm, D_MODEL), acc_map), per_tile, per_tile,
                  pl.BlockSpec((None, D_MODEL, 2 * D_FF), lambda e, i, c: (e, 0, 0),
                               pipeline_mode=pl.Buffered(1)),
                  pl.BlockSpec((None, D_FF, D_MODEL), lambda e, i, c: (e, 0, 0),
                               pipeline_mode=pl.Buffered(1)),
                  const((1, D_MODEL)), const((1, D_MODEL))],
        out_specs=pl.BlockSpec((tm, D_MODEL), tile),
    )
    return pl.pallas_call(
        functools.partial(_moe_ln_kernel, tm=tm, single_tile=nt == 1),
        name="moe_ln",
        grid_spec=grid_spec,
        out_shape=jax.ShapeDtypeStruct((t, D_MODEL), F32),
        input_output_aliases={3: 0},
        compiler_params=_params("arbitrary", "arbitrary"),
    )(counts, xb, x, acc, rank, gate, wgu_bf, wd_bf, gain, bias)


def _trunk(x, n_seq, seq_len, pos_base, r0, conv_buf, w, *, tm, lb, tm_moe):
    q, k, v, g = _ret_in(x, w["ret_in"], w["inv"], seq_len=seq_len, pos_base=pos_base, tm=tm)
    gated, r_new = _retention(q, k, v, g, r0, w["log_g"], n_seq=n_seq, seq_len=seq_len, lb=lb)
    x = _proj_ln(gated, w["ret_out"], x, w["gain"][0], w["bias"][0], tm=tm)
    x_ffn = _ffn_ln(x, w["ffn_gu"], w["ffn_down"], w["gain"][1], w["bias"][1], tm=tm)
    x, xb, conv_new = _conv_ln(x_ffn, w["conv_in"], w["conv_w"], w["conv_out"], conv_buf,
                               w["gain"][2], w["bias"][2], n_seq=n_seq, seq_len=seq_len, tm=tm)
    rank, gate, counts = _router(x, w["router"], tm=tm_moe)
    y = _moe_ln(xb, x, x_ffn, rank, gate, counts[:, :, 0].reshape(-1), w["moe_gu"],
                w["moe_down"], w["gain"][3], w["bias"][3], tm=tm_moe)
    return y, r_new, conv_new


def kernel(x_prompt, x_sample, state_ret, state_conv, ret_w_in, ret_w_out, conv_w_in, conv_w,
           conv_w_out, ffn_w_gu, ffn_w_down, moe_w_router, moe_w_gu, moe_w_down, ln_gain, ln_bias):
    batch, seq, _ = x_prompt.shape
    dec_batch, dec_seq, _ = x_sample.shape

    w = {
        "ret_in": ret_w_in[0].astype(BF16),
        "ret_out": ret_w_out[0].astype(BF16),
        "conv_in": conv_w_in[0].astype(BF16),
        "conv_w": conv_w[0],
        "conv_out": conv_w_out[0].astype(BF16),
        "ffn_gu": ffn_w_gu[0].astype(BF16),
        "ffn_down": ffn_w_down[0].astype(BF16),
        "router": jnp.pad(moe_w_router[0], ((0, 0), (0, LANES - N_EXPERTS))),
        "moe_gu": moe_w_gu[0].astype(BF16),
        "moe_down": moe_w_down[0].astype(BF16),
        "gain": ln_gain.reshape(2 * DEPTH, 1, D_MODEL),
        "bias": ln_bias.reshape(2 * DEPTH, 1, D_MODEL),
        "inv": (1.0 / (ROPE_BASE ** jnp.linspace(0.0, 1.0, ROPE_HALF, dtype=F32))).reshape(1, ROPE_HALF),
        "log_g": jnp.log1p(-jnp.exp2(-5.0 - jnp.arange(RET_HEADS, dtype=F32))),
    }

    zero_ret = jnp.zeros((batch, RET_HEADS, RET_DK, RET_DV), F32)
    zero_conv = jnp.zeros((batch, CONV_WIDTH - 1, D_MODEL), F32)
    y_p, ret_p, conv_p = _trunk(x_prompt.reshape(batch * seq, D_MODEL), batch, seq, 0,
                                zero_ret, zero_conv, w, tm=512, lb=256, tm_moe=1024)
    y_s, ret_s, conv_s = _trunk(x_sample.reshape(dec_batch * dec_seq, D_MODEL), dec_batch, dec_seq,
                                PAST_LEN, state_ret[0], state_conv[0], w,
                                tm=dec_batch * dec_seq, lb=LANES, tm_moe=dec_batch * dec_seq)
    return (y_p.reshape(batch, seq, D_MODEL), y_s.reshape(dec_batch, dec_seq, D_MODEL),
            ret_p[None], ret_s[None], conv_p[None], conv_s[None])
```

```python
import functools
import math

import jax
import jax.numpy as jnp
from jax import lax
from jax.experimental import pallas as pl
from jax.experimental.pallas import tpu as pltpu

D_MODEL = 1024
CHUNK = 64
RET_HEADS = 4
RET_DK = D_MODEL // RET_HEADS
RET_DV = 2 * D_MODEL // RET_HEADS
RET_QK = RET_HEADS * RET_DK
RET_V = RET_HEADS * RET_DV
RET_IN = 2 * RET_QK + 2 * RET_V
ROPE_HALF = RET_DK // 2
CONV_WIDTH = 3
D_FF = 2816
N_EXPERTS = 8
ROPE_BASE = 10000.0
LN_EPS = 1e-5
GN_EPS = 1e-6
DEPTH = 2
DN_ALPHA = (2 * DEPTH) ** 0.25
PAST_LEN = 4096

V7X_VMEM_LIMIT_BYTES = 58 * 1024 * 1024
LANES = 128
SUBLANES = 8

MOE_ROWS_LOG2 = 8
MOE_ROWS = 1 << MOE_ROWS_LOG2
MOE_TAIL_ROWS = (128, MOE_ROWS)
MOE_SINGLE_ROWS = (128, MOE_ROWS, MOE_ROWS + 32, MOE_ROWS + 64)
CUMSUM_CHUNK = 512

F32 = jnp.float32
BF16 = jnp.bfloat16


def _params(*semantics):
    return pltpu.CompilerParams(dimension_semantics=semantics,
                                vmem_limit_bytes=V7X_VMEM_LIMIT_BYTES)


def _resident(shape):
    zeros = (0,) * len(shape)
    return pl.BlockSpec(shape, lambda *_: zeros, pipeline_mode=pl.Buffered(1))


def _layer_norm(z, gain, bias):
    mu = jnp.mean(z, axis=-1, keepdims=True)
    zc = z - mu
    var = jnp.mean(zc * zc, axis=-1, keepdims=True)
    return zc * lax.rsqrt(var + LN_EPS) * gain + bias


def _silu(x):
    return x / (1.0 + jnp.exp(-x))


def _dot(a, b):
    return jnp.dot(a, b, preferred_element_type=F32)


def _rotary(t, cos, sin):
    x1 = t[:, 0:ROPE_HALF]
    x2 = t[:, ROPE_HALF:RET_DK]
    return jnp.concatenate([x1 * cos - x2 * sin, x1 * sin + x2 * cos], axis=-1)


def _ret_mixer_kernel(lg_ref, x_ref, win_ref, inv_ref, r0_ref, wout_ref, gain_ref, bias_ref,
                      o_ref, rout_ref, state, decay, gated, *pads,
                      lb, l_in, chunk, nsub, nstep, pos_base):
    b = pl.program_id(0)
    n = pl.program_id(1)
    padded = l_in < lb

    @pl.when((b == 0) & (n == 0))
    def _():
        ii = lax.broadcasted_iota(jnp.int32, (lb, lb), 0)
        jj = lax.broadcasted_iota(jnp.int32, (lb, lb), 1)
        shift = int(math.log2(chunk))
        visible = (jj >> shift) <= (ii >> shift)
        dist = jnp.abs(ii - jj).astype(F32)
        for h in range(RET_HEADS):
            decay[h] = jnp.where(visible, jnp.exp(lg_ref[h] * dist), 0.0)
        for p in pads:
            p[...] = jnp.zeros_like(p)

    @pl.when(n == 0)
    def _():
        state[...] = r0_ref[...]

    jrow = lax.broadcasted_iota(jnp.int32, (lb, 1), 0).astype(F32)
    for sb in range(nsub):
        rows = slice(sb * l_in, (sb + 1) * l_in)
        x = x_ref[rows, :]
        if padded:
            xpad, = pads
            xpad[0:l_in, :] = x.astype(BF16)
            xb = xpad[...]
        else:
            xb = x.astype(BF16)
        row = (n * nsub + sb) * l_in + lax.broadcasted_iota(jnp.int32, (lb, 1), 0)
        ang = (pos_base + row).astype(F32) * inv_ref[...]
        cos = jnp.cos(ang)
        sin = jnp.sin(ang)
        q_all = _dot(xb, win_ref[:, 0:RET_QK])
        k_all = _dot(xb, win_ref[:, RET_QK:2 * RET_QK])
        v_all = _dot(xb, win_ref[:, 2 * RET_QK:2 * RET_QK + RET_V]).astype(BF16)
        g_all = _dot(xb, win_ref[:, 2 * RET_QK + RET_V:])[0:l_in]

        for h in range(RET_HEADS):
            lg = lg_ref[h]
            qk = slice(h * RET_DK, (h + 1) * RET_DK)
            vv = slice(h * RET_DV, (h + 1) * RET_DV)
            q_h = _rotary(q_all[:, qk], cos, sin).astype(BF16)
            k_h = _rotary(k_all[:, qk], cos, sin) * (RET_DK ** -0.5)
            k_dec = k_h * jnp.exp(lg * (l_in - 1.0 - jrow))
            v_h = v_all[:, vv]
            r_old = state[h]
            s = lax.dot_general(q_h, k_h.astype(BF16), (((1,), (1,)), ((), ())),
                                preferred_element_type=F32)
            p = (s * decay[h]).astype(BF16)
            cross = jnp.exp(lg * (jrow + 1.0))
            o = (_dot(p, v_h) + _dot(q_h, r_old.astype(BF16)) * cross)[0:l_in]
            block_decay = jnp.exp(jnp.full((1, RET_DV), lg * l_in, F32))
            state[h] = r_old * block_decay + _dot(k_dec.T.astype(BF16), v_h)

            mu = jnp.mean(o, axis=-1, keepdims=True)
            oc = o - mu
            var = jnp.mean(oc * oc, axis=-1, keepdims=True)
            on = oc * lax.rsqrt(var + GN_EPS)
            gated[rows, vv] = (_silu(g_all[:, vv]) * on).astype(BF16)

        z = DN_ALPHA * x + _dot(gated[rows, :], wout_ref[...])
        o_ref[rows, :] = _layer_norm(z, gain_ref[...], bias_ref[...])

    @pl.when(n == nstep - 1)
    def _():
        rout_ref[...] = state[...]


def _ret_mixer(x, win_bf, inv, r0, log_g, wout_bf, gain, bias,
               *, n_seq, seq_len, pos_base, lb, nsub):
    l_in = min(seq_len, lb)
    chunk = min(seq_len, CHUNK)
    assert seq_len % (nsub * l_in) == 0 and lb % chunk == 0 and chunk & (chunk - 1) == 0
    nstep = seq_len // (nsub * l_in)
    t = n_seq * seq_len
    row = pl.BlockSpec((nsub * l_in, D_MODEL), lambda b, n: (b * nstep + n, 0))
    st = pl.BlockSpec((None, RET_HEADS, RET_DK, RET_DV), lambda b, n: (b, 0, 0, 0))
    scratch = [pltpu.VMEM((RET_HEADS, RET_DK, RET_DV), F32), pltpu.VMEM((RET_HEADS, lb, lb), F32),
               pltpu.VMEM((nsub * l_in, RET_V), BF16)]
    if l_in < lb:
        scratch += [pltpu.VMEM((lb, D_MODEL), BF16)]
    return pl.pallas_call(
        functools.partial(_ret_mixer_kernel, lb=lb, l_in=l_in, chunk=chunk, nsub=nsub,
                          nstep=nstep, pos_base=pos_base),
        name="ret_mixer",
        grid=(n_seq, nstep),
        in_specs=[pl.BlockSpec(memory_space=pltpu.SMEM), row, _resident((D_MODEL, RET_IN)),
                  _resident((1, ROPE_HALF)), st, _resident((RET_V, D_MODEL)),
                  _resident((1, D_MODEL)), _resident((1, D_MODEL))],
        out_specs=[row, st],
        out_shape=[jax.ShapeDtypeStruct((t, D_MODEL), F32),
                   jax.ShapeDtypeStruct((n_seq, RET_HEADS, RET_DK, RET_DV), F32)],
        scratch_shapes=scratch,
        compiler_params=_params("arbitrary", "arbitrary"),
    )(log_g, x, win_bf, inv, r0, wout_bf, gain, bias)


def _ffn_ln_kernel(x_ref, wgu_ref, wd_ref, gain_ref, bias_ref, o_ref):
    x = x_ref[...]
    xb = x.astype(BF16)
    gate = _dot(xb, wgu_ref[:, 0:D_FF])
    up = _dot(xb, wgu_ref[:, D_FF:2 * D_FF])
    hid = (_silu(gate) * up).astype(BF16)
    z = DN_ALPHA * x + _dot(hid, wd_ref[...])
    o_ref[...] = _layer_norm(z, gain_ref[...], bias_ref[...])


def _ffn_ln(x, wgu_bf, wd_bf, gain, bias, *, tm):
    t = x.shape[0]
    assert t % tm == 0
    return pl.pallas_call(
        _ffn_ln_kernel,
        name="ffn_ln",
        grid=(t // tm,),
        in_specs=[pl.BlockSpec((tm, D_MODEL), lambda i: (i, 0)),
                  _resident((D_MODEL, 2 * D_FF)), _resident((D_FF, D_MODEL)),
                  _resident((1, D_MODEL)), _resident((1, D_MODEL))],
        out_specs=pl.BlockSpec((tm, D_MODEL), lambda i: (i, 0)),
        out_shape=jax.ShapeDtypeStruct((t, D_MODEL), F32),
        compiler_params=_params("parallel"),
    )(x, wgu_bf, wd_bf, gain, bias)


def _conv_ln_kernel(x_ref, win_ref, wc_ref, wout_ref, buf_ref, gain_ref, bias_ref,
                    o_ref, ob_ref, st_ref, u_win, prev, mixed,
                    *, tm, mc, seg, whole_seqs, tiles_per_seq):
    i = pl.program_id(0)
    w0 = wc_ref[0:1, :]
    w1 = wc_ref[1:2, :]
    w2 = wc_ref[2:3, :]
    tail = CONV_WIDTH - 1
    segs = mc // seg
    for c in range(tm // mc):
        rows = slice(c * mc, (c + 1) * mc)
        x = x_ref[rows, :]
        xb = x.astype(BF16)
        gate_b = _dot(xb, win_ref[:, 0:D_MODEL])
        u = _dot(xb, win_ref[:, D_MODEL:2 * D_MODEL]) * _dot(xb, win_ref[:, 2 * D_MODEL:])
        win = u_win.at[c]
        for s in range(segs):
            sg = c * segs + s
            if whole_seqs:
                prev[SUBLANES - tail:SUBLANES, :] = buf_ref[sg]
            elif sg == 0:
                @pl.when(i % tiles_per_seq == 0)
                def _():
                    prev[SUBLANES - tail:SUBLANES, :] = buf_ref[0]
            win[0:SUBLANES, :] = prev[...]
            win[SUBLANES:SUBLANES + seg, :] = u[s * seg:(s + 1) * seg]
            conv = (w0 * win[SUBLANES - 2:SUBLANES - 2 + seg, :]
                    + w1 * win[SUBLANES - 1:SUBLANES - 1 + seg, :]
                    + w2 * win[SUBLANES:SUBLANES + seg, :])
            mixed[c * mc + s * seg:c * mc + (s + 1) * seg, :] = (
                gate_b[s * seg:(s + 1) * seg] * conv).astype(BF16)
            prev[...] = win[seg:seg + SUBLANES, :]
            if whole_seqs:
                st_ref[sg] = win[SUBLANES + seg - tail:SUBLANES + seg, :]
            elif sg == tm // seg - 1:
                st_ref[0] = win[SUBLANES + seg - tail:SUBLANES + seg, :]
        z = DN_ALPHA * x + _dot(mixed[rows, :], wout_ref[...])
        out = _layer_norm(z, gain_ref[...], bias_ref[...])
        o_ref[rows, :] = out
        ob_ref[rows, :] = out.astype(BF16)


def _conv_ln(x, win_bf, wconv, wout_bf, buf, gain, bias, *, n_seq, seq_len, tm, mc):
    t = x.shape[0]
    seg = min(seq_len, mc)
    assert t % tm == 0 and tm % mc == 0 and mc % seg == 0 and seg % SUBLANES == 0
    assert seg == seq_len or seq_len % tm == 0
    tiles_per_seq = max(seq_len // tm, 1)
    seq_per_tile = tm // seg if seg == seq_len else 1
    tail = CONV_WIDTH - 1
    st_spec = pl.BlockSpec((seq_per_tile, tail, D_MODEL), lambda i: (i // tiles_per_seq, 0, 0))
    return pl.pallas_call(
        functools.partial(_conv_ln_kernel, tm=tm, mc=mc, seg=seg, whole_seqs=seg == seq_len,
                          tiles_per_seq=tiles_per_seq),
        name="conv_ln",
        grid=(t // tm,),
        in_specs=[pl.BlockSpec((tm, D_MODEL), lambda i: (i, 0)),
                  _resident((D_MODEL, 3 * D_MODEL)), _resident((CONV_WIDTH, D_MODEL)),
                  _resident((D_MODEL, D_MODEL)), st_spec,
                  _resident((1, D_MODEL)), _resident((1, D_MODEL))],
        out_specs=[pl.BlockSpec((tm, D_MODEL), lambda i: (i, 0)),
                   pl.BlockSpec((tm, D_MODEL), lambda i: (i, 0)), st_spec],
        out_shape=[jax.ShapeDtypeStruct((t, D_MODEL), F32),
                   jax.ShapeDtypeStruct((t, D_MODEL), BF16),
                   jax.ShapeDtypeStruct((n_seq, tail, D_MODEL), F32)],
        scratch_shapes=[pltpu.VMEM((tm // mc, SUBLANES + seg, D_MODEL), F32),
                        pltpu.VMEM((SUBLANES, D_MODEL), F32),
                        pltpu.VMEM((tm, D_MODEL), BF16)],
        compiler_params=_params("arbitrary"),
    )(x, win_bf, wconv, wout_bf, buf, gain, bias)


def _split_bf16(a):
    hi = a.astype(BF16)
    lo = (a - hi.astype(F32)).astype(BF16)
    return hi, lo


def _router_kernel(x_ref, wr_ref, rank_ref, gate_ref, cnt_ref, *, tm):
    x_hi, x_lo = _split_bf16(x_ref[...])
    w_hi, w_lo = _split_bf16(wr_ref[...])
    logits = _dot(x_hi, w_hi) + (_dot(x_lo, w_hi) + _dot(x_hi, w_lo))
    lt = logits.T[0:N_EXPERTS, :]
    eidx = lax.broadcasted_iota(jnp.int32, (N_EXPERTS, tm), 0).astype(F32)
    neg = jnp.float32(-jnp.inf)
    m1 = jnp.max(lt, axis=0, keepdims=True)
    i1 = jnp.min(jnp.where(lt == m1, eidx, float(N_EXPERTS)), axis=0, keepdims=True)
    first = eidx == i1
    rest = jnp.where(first, neg, lt)
    m2 = jnp.max(rest, axis=0, keepdims=True)
    i2 = jnp.min(jnp.where(rest == m2, eidx, float(N_EXPERTS)), axis=0, keepdims=True)
    second = eidx == i2
    e2 = jnp.exp(m2 - m1)
    w1 = 1.0 / (1.0 + e2)
    w2 = e2 / (1.0 + e2)
    gate_ref[...] = jnp.where(first, w1, 0.0) + jnp.where(second, w2, 0.0)
    sel = first | second
    self32 = jnp.where(sel, 1.0, 0.0)

    c = CUMSUM_CHUNK if tm % CUMSUM_CHUNK == 0 else tm
    si = lax.broadcasted_iota(jnp.int32, (c, c), 0)
    ti = lax.broadcasted_iota(jnp.int32, (c, c), 1)
    tri = jnp.where(si < ti, 1.0, 0.0).astype(BF16)
    offset = jnp.zeros((N_EXPERTS, 1), F32)
    for j in range(tm // c):
        blk = self32[:, j * c:(j + 1) * c]
        rank = _dot(blk.astype(BF16), tri) + offset
        rank_ref[:, j * c:(j + 1) * c] = jnp.where(sel[:, j * c:(j + 1) * c],
                                                   rank.astype(jnp.int32), -1)
        offset = offset + jnp.sum(blk, axis=1, keepdims=True)
    cnt_ref[...] = jnp.broadcast_to(offset.astype(jnp.int32), (N_EXPERTS, LANES))


def _router(x, wr_pad, *, tm):
    t = x.shape[0]
    assert t % tm == 0
    nt = t // tm
    return pl.pallas_call(
        functools.partial(_router_kernel, tm=tm),
        name="router",
        grid=(nt,),
        in_specs=[pl.BlockSpec((tm, D_MODEL), lambda i: (i, 0)), _resident((D_MODEL, LANES))],
        out_specs=[pl.BlockSpec((N_EXPERTS, tm), lambda i: (0, i)),
                   pl.BlockSpec((N_EXPERTS, tm), lambda i: (0, i)),
                   pl.BlockSpec((None, N_EXPERTS, LANES), lambda i: (i, 0, 0))],
        out_shape=[jax.ShapeDtypeStruct((N_EXPERTS, t), jnp.int32),
                   jax.ShapeDtypeStruct((N_EXPERTS, t), F32),
                   jax.ShapeDtypeStruct((nt, N_EXPERTS, LANES), jnp.int32)],
        compiler_params=_params("parallel"),
    )(x, wr_pad)


def _moe_ln_kernel(cnt_ref, xb_ref, x_ref, acc_ref, rank_ref, gate_ref, wgu_ref, wd_ref,
                   gain_ref, bias_ref, o_ref, *, tm, single_tile):
    e = pl.program_id(0)
    i = pl.program_id(1)
    ne = pl.num_programs(0)

    if single_tile:
        @pl.when(e == 0)
        def _():
            o_ref[...] = jnp.zeros_like(o_ref)

    def earlier_experts():
        if single_tile:
            return o_ref[...]
        return jnp.where(e > 0, acc_ref[...], 0.0)

    count = cnt_ref[i * N_EXPERTS + e]
    rank = rank_ref[pl.ds(e, 1), :]
    gate = gate_ref[pl.ds(e, 1), :]

    def expert_rows(row0, nrows, first):
        rows = row0 + lax.broadcasted_iota(jnp.int32, (nrows, 1), 0)
        hit = rank == rows
        onehot = jnp.where(hit, 1.0, 0.0)
        xg = _dot(onehot.astype(BF16), xb_ref[...]).astype(BF16)
        gcol = jnp.sum(jnp.where(hit, gate, 0.0), axis=1, keepdims=True)
        hid = (_silu(_dot(xg, wgu_ref[:, 0:D_FF])) * _dot(xg, wgu_ref[:, D_FF:])).astype(BF16)
        y = (_dot(hid, wd_ref[...]) * gcol).astype(BF16)
        scattered = _dot(onehot.T.astype(BF16), y)
        o_ref[...] = (earlier_experts() if first else o_ref[...]) + scattered

    lo = 0
    for nrows in MOE_SINGLE_ROWS:
        @pl.when((count > lo) & (count <= nrows))
        def _(nrows=nrows):
            expert_rows(0, nrows, True)
        lo = nrows

    @pl.when(count == 0)
    def _():
        o_ref[...] = earlier_experts()

    @pl.when(count > MOE_SINGLE_ROWS[-1])
    def _():
        o_ref[...] = earlier_experts()

        def full_block(r, carry):
            expert_rows(r * MOE_ROWS, MOE_ROWS, False)
            return carry

        n_full = lax.shift_right_logical(count, MOE_ROWS_LOG2)
        lax.fori_loop(0, n_full, full_block, 0)
        base = n_full * MOE_ROWS
        rem = count - base
        lo = 0
        for nrows in MOE_TAIL_ROWS:
            @pl.when((rem > lo) & (rem <= nrows))
            def _(nrows=nrows):
                expert_rows(base, nrows, False)
            lo = nrows

    @pl.when(e == ne - 1)
    def _():
        z = DN_ALPHA * x_ref[...] + o_ref[...]
        o_ref[...] = _layer_norm(z, gain_ref[...], bias_ref[...])


def _moe_ln(xb, x, acc, rank, gate, counts, wgu_bf, wd_bf, gain, bias, *, tm):
    t = x.shape[0]
    assert t % tm == 0
    nt = t // tm
    assert nt == 1 or nt >= 3
    last_e = N_EXPERTS - 1
    tile = lambda e, i, c: (i, 0)
    x_map = lambda e, i, c: (jnp.where(e == last_e, i, 0), 0)
    acc_map = lambda e, i, c: (jnp.where(e == 0, nt - 1, i), 0)
    per_tile = pl.BlockSpec((N_EXPERTS, tm), lambda e, i, c: (0, i))
    const = lambda shape: pl.BlockSpec(shape, lambda e, i, c: (0,) * len(shape),
                                       pipeline_mode=pl.Buffered(1))
    grid_spec = pltpu.PrefetchScalarGridSpec(
        num_scalar_prefetch=1,
        grid=(N_EXPERTS, nt),
        in_specs=[pl.BlockSpec((tm, D_MODEL), tile),
                  pl.BlockSpec((tm, D_MODEL), x_map, pipeline_mode=pl.Buffered(1)),
                  pl.BlockSpec((tm, D_MODEL), acc_map), per_tile, per_tile,
                  pl.BlockSpec((None, D_MODEL, 2 * D_FF), lambda e, i, c: (e, 0, 0),
                               pipeline_mode=pl.Buffered(1)),
                  pl.BlockSpec((None, D_FF, D_MODEL), lambda e, i, c: (e, 0, 0),
                               pipeline_mode=pl.Buffered(1)),
                  const((1, D_MODEL)), const((1, D_MODEL))],
        out_specs=pl.BlockSpec((tm, D_MODEL), tile),
    )
    return pl.pallas_call(
        functools.partial(_moe_ln_kernel, tm=tm, single_tile=nt == 1),
        name="moe_ln",
        grid_spec=grid_spec,
        out_shape=jax.ShapeDtypeStruct((t, D_MODEL), F32),
        input_output_aliases={3: 0},
        compiler_params=_params("arbitrary", "arbitrary"),
    )(counts, xb, x, acc, rank, gate, wgu_bf, wd_bf, gain, bias)


def _trunk(x, n_seq, seq_len, pos_base, r0, conv_buf, w, *, tm, mc, lb, nsub, tm_moe):
    x, r_new = _ret_mixer(x, w["ret_in"], w["inv"], r0, w["log_g"], w["ret_out"], w["gain"][0],
                          w["bias"][0], n_seq=n_seq, seq_len=seq_len, pos_base=pos_base, lb=lb,
                          nsub=nsub)
    x_ffn = _ffn_ln(x, w["ffn_gu"], w["ffn_down"], w["gain"][1], w["bias"][1], tm=tm)
    x, xb, conv_new = _conv_ln(x_ffn, w["conv_in"], w["conv_w"], w["conv_out"], conv_buf,
                               w["gain"][2], w["bias"][2], n_seq=n_seq, seq_len=seq_len, tm=tm,
                               mc=mc)
    rank, gate, counts = _router(x, w["router"], tm=tm_moe)
    y = _moe_ln(xb, x, x_ffn, rank, gate, counts[:, :, 0].reshape(-1), w["moe_gu"],
                w["moe_down"], w["gain"][3], w["bias"][3], tm=tm_moe)
    return y, r_new, conv_new


def kernel(x_prompt, x_sample, state_ret, state_conv, ret_w_in, ret_w_out, conv_w_in, conv_w,
           conv_w_out, ffn_w_gu, ffn_w_down, moe_w_router, moe_w_gu, moe_w_down, ln_gain, ln_bias):
    batch, seq, _ = x_prompt.shape
    dec_batch, dec_seq, _ = x_sample.shape

    w = {
        "ret_in": ret_w_in[0].astype(BF16),
        "ret_out": ret_w_out[0].astype(BF16),
        "conv_in": conv_w_in[0].astype(BF16),
        "conv_w": conv_w[0],
        "conv_out": conv_w_out[0].astype(BF16),
        "ffn_gu": ffn_w_gu[0].astype(BF16),
        "ffn_down": ffn_w_down[0].astype(BF16),
        "router": jnp.pad(moe_w_router[0], ((0, 0), (0, LANES - N_EXPERTS))),
        "moe_gu": moe_w_gu[0].astype(BF16),
        "moe_down": moe_w_down[0].astype(BF16),
        "gain": ln_gain.reshape(2 * DEPTH, 1, D_MODEL),
        "bias": ln_bias.reshape(2 * DEPTH, 1, D_MODEL),
        "inv": (1.0 / (ROPE_BASE ** jnp.linspace(0.0, 1.0, ROPE_HALF, dtype=F32))).reshape(1, ROPE_HALF),
        "log_g": jnp.log1p(-jnp.exp2(-5.0 - jnp.arange(RET_HEADS, dtype=F32))),
    }

    zero_ret = jnp.zeros((batch, RET_HEADS, RET_DK, RET_DV), F32)
    zero_conv = jnp.zeros((batch, CONV_WIDTH - 1, D_MODEL), F32)
    y_p, ret_p, conv_p = _trunk(x_prompt.reshape(batch * seq, D_MODEL), batch, seq, 0,
                                zero_ret, zero_conv, w, tm=512, mc=256, lb=256, nsub=2,
                                tm_moe=1024)
    y_s, ret_s, conv_s = _trunk(x_sample.reshape(dec_batch * dec_seq, D_MODEL), dec_batch, dec_seq,
                                PAST_LEN, state_ret[0], state_conv[0], w,
                                tm=dec_batch * dec_seq, mc=dec_batch * dec_seq, lb=LANES, nsub=1,
                                tm_moe=dec_batch * dec_seq)
    return (y_p.reshape(batch, seq, D_MODEL), y_s.reshape(dec_batch, dec_seq, D_MODEL),
            ret_p[None], ret_s[None], conv_p[None], conv_s[None])
```

```python
import functools
import math

import jax
import jax.numpy as jnp
from jax import lax
from jax.experimental import pallas as pl
from jax.experimental.pallas import tpu as pltpu

D_MODEL = 1024
CHUNK = 64
RET_HEADS = 4
RET_DK = D_MODEL // RET_HEADS
RET_DV = 2 * D_MODEL // RET_HEADS
RET_QK = RET_HEADS * RET_DK
RET_V = RET_HEADS * RET_DV
RET_IN = 2 * RET_QK + 2 * RET_V
ROPE_HALF = RET_DK // 2
CONV_WIDTH = 3
D_FF = 2816
N_EXPERTS = 8
ROPE_BASE = 10000.0
LN_EPS = 1e-5
GN_EPS = 1e-6
DEPTH = 2
DN_ALPHA = (2 * DEPTH) ** 0.25
PAST_LEN = 4096

V7X_VMEM_LIMIT_BYTES = 58 * 1024 * 1024
LANES = 128
SUBLANES = 8

MOE_ROWS_LOG2 = 8
MOE_ROWS = 1 << MOE_ROWS_LOG2
MOE_SINGLE_ROWS = (MOE_ROWS, MOE_ROWS + 48)
MOE_BUFFERS = 3
CUMSUM_CHUNK = 512

F32 = jnp.float32
BF16 = jnp.bfloat16


def _params(*semantics):
    return pltpu.CompilerParams(dimension_semantics=semantics,
                                vmem_limit_bytes=V7X_VMEM_LIMIT_BYTES)


def _resident(shape):
    zeros = (0,) * len(shape)
    return pl.BlockSpec(shape, lambda *_: zeros, pipeline_mode=pl.Buffered(1))


def _layer_norm(z, gain, bias):
    mu = jnp.mean(z, axis=-1, keepdims=True)
    zc = z - mu
    var = jnp.mean(zc * zc, axis=-1, keepdims=True)
    return zc * lax.rsqrt(var + LN_EPS) * gain + bias


def _silu(x):
    return x / (1.0 + jnp.exp(-x))


def _dot(a, b):
    return jnp.dot(a, b, preferred_element_type=F32)


def _rotary(t, cos, sin):
    x1 = t[:, 0:ROPE_HALF]
    x2 = t[:, ROPE_HALF:RET_DK]
    return jnp.concatenate([x1 * cos - x2 * sin, x1 * sin + x2 * cos], axis=-1)


def _ret_mixer_kernel(lg_ref, x_ref, win_ref, inv_ref, r0_ref, wout_ref, gain_ref, bias_ref,
                      o_ref, rout_ref, state, decay, gated, *pads,
                      lb, l_in, chunk, nsub, nstep, pos_base):
    b = pl.program_id(0)
    n = pl.program_id(1)
    padded = l_in < lb

    @pl.when((b == 0) & (n == 0))
    def _():
        ii = lax.broadcasted_iota(jnp.int32, (lb, lb), 0)
        jj = lax.broadcasted_iota(jnp.int32, (lb, lb), 1)
        shift = int(math.log2(chunk))
        visible = (jj >> shift) <= (ii >> shift)
        dist = jnp.abs(ii - jj).astype(F32)
        for h in range(RET_HEADS):
            decay[h] = jnp.where(visible, jnp.exp(lg_ref[h] * dist), 0.0)
        for p in pads:
            p[...] = jnp.zeros_like(p)

    @pl.when(n == 0)
    def _():
        state[...] = r0_ref[...]

    jrow = lax.broadcasted_iota(jnp.int32, (lb, 1), 0).astype(F32)
    for sb in range(nsub):
        rows = slice(sb * l_in, (sb + 1) * l_in)
        x = x_ref[rows, :]
        if padded:
            xpad, = pads
            xpad[0:l_in, :] = x.astype(BF16)
            xb = xpad[...]
        else:
            xb = x.astype(BF16)
        row = (n * nsub + sb) * l_in + lax.broadcasted_iota(jnp.int32, (lb, 1), 0)
        ang = (pos_base + row).astype(F32) * inv_ref[...]
        cos = jnp.cos(ang)
        sin = jnp.sin(ang)
        q_all = _dot(xb, win_ref[:, 0:RET_QK])
        k_all = _dot(xb, win_ref[:, RET_QK:2 * RET_QK])
        v_all = _dot(xb, win_ref[:, 2 * RET_QK:2 * RET_QK + RET_V]).astype(BF16)
        g_all = _dot(xb, win_ref[:, 2 * RET_QK + RET_V:])[0:l_in]

        for h in range(RET_HEADS):
            lg = lg_ref[h]
            qk = slice(h * RET_DK, (h + 1) * RET_DK)
            vv = slice(h * RET_DV, (h + 1) * RET_DV)
            q_h = _rotary(q_all[:, qk], cos, sin).astype(BF16)
            k_h = _rotary(k_all[:, qk], cos, sin) * (RET_DK ** -0.5)
            k_dec = k_h * jnp.exp(lg * (l_in - 1.0 - jrow))
            v_h = v_all[:, vv]
            r_old = state[h]
            s = lax.dot_general(q_h, k_h.astype(BF16), (((1,), (1,)), ((), ())),
                                preferred_element_type=F32)
            p = (s * decay[h]).astype(BF16)
            cross = jnp.exp(lg * (jrow + 1.0))
            o = (_dot(p, v_h) + _dot(q_h, r_old.astype(BF16)) * cross)[0:l_in]
            block_decay = jnp.exp(jnp.full((1, RET_DV), lg * l_in, F32))
            state[h] = r_old * block_decay + _dot(k_dec.T.astype(BF16), v_h)

            mu = jnp.mean(o, axis=-1, keepdims=True)
            oc = o - mu
            var = jnp.mean(oc * oc, axis=-1, keepdims=True)
            on = oc * lax.rsqrt(var + GN_EPS)
            gated[rows, vv] = (_silu(g_all[:, vv]) * on).astype(BF16)

        z = DN_ALPHA * x + _dot(gated[rows, :], wout_ref[...])
        o_ref[rows, :] = _layer_norm(z, gain_ref[...], bias_ref[...])

    @pl.when(n == nstep - 1)
    def _():
        rout_ref[...] = state[...]


def _ret_mixer(x, win_bf, inv, r0, log_g, wout_bf, gain, bias,
               *, n_seq, seq_len, pos_base, lb, nsub):
    l_in = min(seq_len, lb)
    chunk = min(seq_len, CHUNK)
    assert seq_len % (nsub * l_in) == 0 and lb % chunk == 0 and chunk & (chunk - 1) == 0
    nstep = seq_len // (nsub * l_in)
    t = n_seq * seq_len
    row = pl.BlockSpec((nsub * l_in, D_MODEL), lambda b, n: (b * nstep + n, 0))
    st = pl.BlockSpec((None, RET_HEADS, RET_DK, RET_DV), lambda b, n: (b, 0, 0, 0))
    scratch = [pltpu.VMEM((RET_HEADS, RET_DK, RET_DV), F32), pltpu.VMEM((RET_HEADS, lb, lb), F32),
               pltpu.VMEM((nsub * l_in, RET_V), BF16)]
    if l_in < lb:
        scratch += [pltpu.VMEM((lb, D_MODEL), BF16)]
    return pl.pallas_call(
        functools.partial(_ret_mixer_kernel, lb=lb, l_in=l_in, chunk=chunk, nsub=nsub,
                          nstep=nstep, pos_base=pos_base),
        name="ret_mixer",
        grid=(n_seq, nstep),
        in_specs=[pl.BlockSpec(memory_space=pltpu.SMEM), row, _resident((D_MODEL, RET_IN)),
                  _resident((1, ROPE_HALF)), st, _resident((RET_V, D_MODEL)),
                  _resident((1, D_MODEL)), _resident((1, D_MODEL))],
        out_specs=[row, st],
        out_shape=[jax.ShapeDtypeStruct((t, D_MODEL), F32),
                   jax.ShapeDtypeStruct((n_seq, RET_HEADS, RET_DK, RET_DV), F32)],
        scratch_shapes=scratch,
        compiler_params=_params("arbitrary", "arbitrary"),
    )(log_g, x, win_bf, inv, r0, wout_bf, gain, bias)


def _ffn_ln_kernel(x_ref, wgu_ref, wd_ref, gain_ref, bias_ref, o_ref):
    x = x_ref[...]
    xb = x.astype(BF16)
    gate = _dot(xb, wgu_ref[:, 0:D_FF])
    up = _dot(xb, wgu_ref[:, D_FF:2 * D_FF])
    hid = (_silu(gate) * up).astype(BF16)
    z = DN_ALPHA * x + _dot(hid, wd_ref[...])
    o_ref[...] = _layer_norm(z, gain_ref[...], bias_ref[...])


def _ffn_ln(x, wgu_bf, wd_bf, gain, bias, *, tm):
    t = x.shape[0]
    assert t % tm == 0
    return pl.pallas_call(
        _ffn_ln_kernel,
        name="ffn_ln",
        grid=(t // tm,),
        in_specs=[pl.BlockSpec((tm, D_MODEL), lambda i: (i, 0)),
                  _resident((D_MODEL, 2 * D_FF)), _resident((D_FF, D_MODEL)),
                  _resident((1, D_MODEL)), _resident((1, D_MODEL))],
        out_specs=pl.BlockSpec((tm, D_MODEL), lambda i: (i, 0)),
        out_shape=jax.ShapeDtypeStruct((t, D_MODEL), F32),
        compiler_params=_params("parallel"),
    )(x, wgu_bf, wd_bf, gain, bias)


def _conv_ln_kernel(x_ref, win_ref, wc_ref, wout_ref, buf_ref, gain_ref, bias_ref,
                    o_ref, ob_ref, st_ref, u_win, prev, mixed,
                    *, tm, mc, seg, whole_seqs, tiles_per_seq):
    i = pl.program_id(0)
    w0 = wc_ref[0:1, :]
    w1 = wc_ref[1:2, :]
    w2 = wc_ref[2:3, :]
    tail = CONV_WIDTH - 1
    segs = mc // seg
    for c in range(tm // mc):
        rows = slice(c * mc, (c + 1) * mc)
        x = x_ref[rows, :]
        xb = x.astype(BF16)
        gate_b = _dot(xb, win_ref[:, 0:D_MODEL])
        u = _dot(xb, win_ref[:, D_MODEL:2 * D_MODEL]) * _dot(xb, win_ref[:, 2 * D_MODEL:])
        win = u_win.at[c]
        for s in range(segs):
            sg = c * segs + s
            if whole_seqs:
                prev[SUBLANES - tail:SUBLANES, :] = buf_ref[sg]
            elif sg == 0:
                @pl.when(i % tiles_per_seq == 0)
                def _():
                    prev[SUBLANES - tail:SUBLANES, :] = buf_ref[0]
            win[0:SUBLANES, :] = prev[...]
            win[SUBLANES:SUBLANES + seg, :] = u[s * seg:(s + 1) * seg]
            conv = (w0 * win[SUBLANES - 2:SUBLANES - 2 + seg, :]
                    + w1 * win[SUBLANES - 1:SUBLANES - 1 + seg, :]
                    + w2 * win[SUBLANES:SUBLANES + seg, :])
            mixed[c * mc + s * seg:c * mc + (s + 1) * seg, :] = (
                gate_b[s * seg:(s + 1) * seg] * conv).astype(BF16)
            prev[...] = win[seg:seg + SUBLANES, :]
            if whole_seqs:
                st_ref[sg] = win[SUBLANES + seg - tail:SUBLANES + seg, :]
            elif sg == tm // seg - 1:
                st_ref[0] = win[SUBLANES + seg - tail:SUBLANES + seg, :]
        z = DN_ALPHA * x + _dot(mixed[rows, :], wout_ref[...])
        out = _layer_norm(z, gain_ref[...], bias_ref[...])
        o_ref[rows, :] = out
        ob_ref[rows, :] = out.astype(BF16)


def _conv_ln(x, win_bf, wconv, wout_bf, buf, gain, bias, *, n_seq, seq_len, tm, mc):
    t = x.shape[0]
    seg = min(seq_len, mc)
    assert t % tm == 0 and tm % mc == 0 and mc % seg == 0 and seg % SUBLANES == 0
    assert seg == seq_len or seq_len % tm == 0
    tiles_per_seq = max(seq_len // tm, 1)
    seq_per_tile = tm // seg if seg == seq_len else 1
    tail = CONV_WIDTH - 1
    st_spec = pl.BlockSpec((seq_per_tile, tail, D_MODEL), lambda i: (i // tiles_per_seq, 0, 0))
    return pl.pallas_call(
        functools.partial(_conv_ln_kernel, tm=tm, mc=mc, seg=seg, whole_seqs=seg == seq_len,
                          tiles_per_seq=tiles_per_seq),
        name="conv_ln",
        grid=(t // tm,),
        in_specs=[pl.BlockSpec((tm, D_MODEL), lambda i: (i, 0)),
                  _resident((D_MODEL, 3 * D_MODEL)), _resident((CONV_WIDTH, D_MODEL)),
                  _resident((D_MODEL, D_MODEL)), st_spec,
                  _resident((1, D_MODEL)), _resident((1, D_MODEL))],
        out_specs=[pl.BlockSpec((tm, D_MODEL), lambda i: (i, 0)),
                   pl.BlockSpec((tm, D_MODEL), lambda i: (i, 0)), st_spec],
        out_shape=[jax.ShapeDtypeStruct((t, D_MODEL), F32),
                   jax.ShapeDtypeStruct((t, D_MODEL), BF16),
                   jax.ShapeDtypeStruct((n_seq, tail, D_MODEL), F32)],
        scratch_shapes=[pltpu.VMEM((tm // mc, SUBLANES + seg, D_MODEL), F32),
                        pltpu.VMEM((SUBLANES, D_MODEL), F32),
                        pltpu.VMEM((tm, D_MODEL), BF16)],
        compiler_params=_params("arbitrary"),
    )(x, win_bf, wconv, wout_bf, buf, gain, bias)


def _split_bf16(a):
    hi = a.astype(BF16)
    lo = (a - hi.astype(F32)).astype(BF16)
    return hi, lo


def _router_kernel(x_ref, wr_ref, rank_ref, gate_ref, cnt_ref, *, tm):
    x_hi, x_lo = _split_bf16(x_ref[...])
    w_hi, w_lo = _split_bf16(wr_ref[...])
    logits = _dot(x_hi, w_hi) + (_dot(x_lo, w_hi) + _dot(x_hi, w_lo))
    lt = logits.T[0:N_EXPERTS, :]
    eidx = lax.broadcasted_iota(jnp.int32, (N_EXPERTS, tm), 0).astype(F32)
    neg = jnp.float32(-jnp.inf)
    m1 = jnp.max(lt, axis=0, keepdims=True)
    i1 = jnp.min(jnp.where(lt == m1, eidx, float(N_EXPERTS)), axis=0, keepdims=True)
    first = eidx == i1
    rest = jnp.where(first, neg, lt)
    m2 = jnp.max(rest, axis=0, keepdims=True)
    i2 = jnp.min(jnp.where(rest == m2, eidx, float(N_EXPERTS)), axis=0, keepdims=True)
    second = eidx == i2
    e2 = jnp.exp(m2 - m1)
    w1 = 1.0 / (1.0 + e2)
    w2 = e2 / (1.0 + e2)
    gate_ref[...] = jnp.where(first, w1, 0.0) + jnp.where(second, w2, 0.0)
    sel = first | second
    self32 = jnp.where(sel, 1.0, 0.0)

    c = CUMSUM_CHUNK if tm % CUMSUM_CHUNK == 0 else tm
    si = lax.broadcasted_iota(jnp.int32, (c, c), 0)
    ti = lax.broadcasted_iota(jnp.int32, (c, c), 1)
    tri = jnp.where(si < ti, 1.0, 0.0).astype(BF16)
    offset = jnp.zeros((N_EXPERTS, 1), F32)
    for j in range(tm // c):
        blk = self32[:, j * c:(j + 1) * c]
        rank = _dot(blk.astype(BF16), tri) + offset
        rank_ref[:, j * c:(j + 1) * c] = jnp.where(sel[:, j * c:(j + 1) * c],
                                                   rank.astype(jnp.int32), -1)
        offset = offset + jnp.sum(blk, axis=1, keepdims=True)
    cnt_ref[...] = jnp.broadcast_to(offset.astype(jnp.int32), (N_EXPERTS, LANES))


def _router(x, wr_pad, *, tm):
    t = x.shape[0]
    assert t % tm == 0
    nt = t // tm
    return pl.pallas_call(
        functools.partial(_router_kernel, tm=tm),
        name="router",
        grid=(nt,),
        in_specs=[pl.BlockSpec((tm, D_MODEL), lambda i: (i, 0)), _resident((D_MODEL, LANES))],
        out_specs=[pl.BlockSpec((N_EXPERTS, tm), lambda i: (0, i)),
                   pl.BlockSpec((N_EXPERTS, tm), lambda i: (0, i)),
                   pl.BlockSpec((None, N_EXPERTS, LANES), lambda i: (i, 0, 0))],
        out_shape=[jax.ShapeDtypeStruct((N_EXPERTS, t), jnp.int32),
                   jax.ShapeDtypeStruct((N_EXPERTS, t), F32),
                   jax.ShapeDtypeStruct((nt, N_EXPERTS, LANES), jnp.int32)],
        compiler_params=_params("parallel"),
    )(x, wr_pad)


def _moe_ln_kernel(cnt_ref, xb_ref, x_ref, rank_ref, gate_ref, wgu_ref, wd_ref,
                   gain_ref, bias_ref, out_hbm, buf, rsem, wsem, *, tm, nt):
    e = pl.program_id(0)
    i = pl.program_id(1)
    ne = pl.num_programs(0)
    single_tile = nt == 1
    s = e * nt + i
    slot = 0 if single_tile else lax.rem(s, MOE_BUFFERS)
    o_ref = buf.at[slot]

    def tile_rows(tile):
        start = tile * tm
        if not isinstance(start, int):
            start = pl.multiple_of(start, tm)
        return out_hbm.at[pl.ds(start, tm), :]

    def read_sum(tile, dst):
        return pltpu.make_async_copy(tile_rows(tile), buf.at[dst], rsem.at[dst])

    def write_sum(tile, src):
        return pltpu.make_async_copy(buf.at[src], tile_rows(tile), wsem.at[src])

    if not single_tile:
        next_slot = lax.rem(s + 1, MOE_BUFFERS)

        @pl.when(s >= 2)
        def _():
            write_sum(jnp.where(i >= 2, i - 2, i - 2 + nt), next_slot).wait()

        @pl.when((s + 1 >= nt) & (s + 1 < ne * nt))
        def _():
            read_sum(jnp.where(i + 1 < nt, i + 1, 0), next_slot).start()

        @pl.when(e > 0)
        def _():
            read_sum(i, slot).wait()

    @pl.when(e == 0)
    def _():
        o_ref[...] = jnp.zeros_like(o_ref)

    count = cnt_ref[i * N_EXPERTS + e]
    rank = rank_ref[pl.ds(e, 1), :]
    gate = gate_ref[pl.ds(e, 1), :]

    def expert_rows(row0, nrows):
        rows = row0 + lax.broadcasted_iota(jnp.int32, (nrows, 1), 0)
        hit = rank == rows
        onehot = jnp.where(hit, 1.0, 0.0)
        xg = _dot(onehot.astype(BF16), xb_ref[...]).astype(BF16)
        gcol = jnp.sum(jnp.where(hit, gate, 0.0), axis=1, keepdims=True)
        hid = (_silu(_dot(xg, wgu_ref[:, 0:D_FF])) * _dot(xg, wgu_ref[:, D_FF:])).astype(BF16)
        y = (_dot(hid, wd_ref[...]) * gcol).astype(BF16)
        o_ref[...] += _dot(onehot.T.astype(BF16), y)

    lo = 0
    for nrows in MOE_SINGLE_ROWS:
        @pl.when((count > lo) & (count <= nrows))
        def _(nrows=nrows):
            expert_rows(0, nrows)
        lo = nrows

    @pl.when(count > MOE_SINGLE_ROWS[-1])
    def _():
        def full_block(r, carry):
            expert_rows(r * MOE_ROWS, MOE_ROWS)
            return carry

        lax.fori_loop(0, lax.shift_right_logical(count + (MOE_ROWS - 1), MOE_ROWS_LOG2),
                      full_block, 0)

    @pl.when(e == ne - 1)
    def _():
        z = DN_ALPHA * x_ref[...] + o_ref[...]
        o_ref[...] = _layer_norm(z, gain_ref[...], bias_ref[...])

    if single_tile:
        @pl.when(e == ne - 1)
        def _():
            done = write_sum(0, 0)
            done.start()
            done.wait()
    else:
        write_sum(i, slot).start()

        @pl.when(s == ne * nt - 1)
        def _():
            write_sum(i - 1, lax.rem(s + MOE_BUFFERS - 1, MOE_BUFFERS)).wait()
            write_sum(i, slot).wait()


def _moe_ln(xb, x, rank, gate, counts, wgu_bf, wd_bf, gain, bias, *, tm):
    t = x.shape[0]
    assert t % tm == 0
    nt = t // tm
    assert nt == 1 or nt >= MOE_BUFFERS
    last_e = N_EXPERTS - 1
    x_map = lambda e, i, c: (jnp.where(e == last_e, i, 0), 0)
    per_tile = pl.BlockSpec((N_EXPERTS, tm), lambda e, i, c: (0, i))
    const = lambda shape: pl.BlockSpec(shape, lambda e, i, c: (0,) * len(shape),
                                       pipeline_mode=pl.Buffered(1))
    grid_spec = pltpu.PrefetchScalarGridSpec(
        num_scalar_prefetch=1,
        grid=(N_EXPERTS, nt),
        in_specs=[pl.BlockSpec((tm, D_MODEL), lambda e, i, c: (i, 0)),
                  pl.BlockSpec((tm, D_MODEL), x_map, pipeline_mode=pl.Buffered(1)),
                  per_tile, per_tile,
                  pl.BlockSpec((None, D_MODEL, 2 * D_FF), lambda e, i, c: (e, 0, 0),
                               pipeline_mode=pl.Buffered(1)),
                  pl.BlockSpec((None, D_FF, D_MODEL), lambda e, i, c: (e, 0, 0),
                               pipeline_mode=pl.Buffered(1)),
                  const((1, D_MODEL)), const((1, D_MODEL))],
        out_specs=pl.BlockSpec(memory_space=pl.ANY),
        scratch_shapes=[pltpu.VMEM((1 if nt == 1 else MOE_BUFFERS, tm, D_MODEL), F32),
                        pltpu.SemaphoreType.DMA((MOE_BUFFERS,)),
                        pltpu.SemaphoreType.DMA((MOE_BUFFERS,))],
    )
    return pl.pallas_call(
        functools.partial(_moe_ln_kernel, tm=tm, nt=nt),
        name="moe_ln",
        grid_spec=grid_spec,
        out_shape=jax.ShapeDtypeStruct((t, D_MODEL), F32),
        compiler_params=_params("arbitrary", "arbitrary"),
    )(counts, xb, x, rank, gate, wgu_bf, wd_bf, gain, bias)


def _trunk(x, n_seq, seq_len, pos_base, r0, conv_buf, w, *, tm, mc, lb, nsub, tm_moe):
    x, r_new = _ret_mixer(x, w["ret_in"], w["inv"], r0, w["log_g"], w["ret_out"], w["gain"][0],
                          w["bias"][0], n_seq=n_seq, seq_len=seq_len, pos_base=pos_base, lb=lb,
                          nsub=nsub)
    x_ffn = _ffn_ln(x, w["ffn_gu"], w["ffn_down"], w["gain"][1], w["bias"][1], tm=tm)
    x, xb, conv_new = _conv_ln(x_ffn, w["conv_in"], w["conv_w"], w["conv_out"], conv_buf,
                               w["gain"][2], w["bias"][2], n_seq=n_seq, seq_len=seq_len, tm=tm,
                               mc=mc)
    rank, gate, counts = _router(x, w["router"], tm=tm_moe)
    y = _moe_ln(xb, x, rank, gate, counts[:, :, 0].reshape(-1), w["moe_gu"], w["moe_down"],
                w["gain"][3], w["bias"][3], tm=tm_moe)
    return y, r_new, conv_new


def kernel(x_prompt, x_sample, state_ret, state_conv, ret_w_in, ret_w_out, conv_w_in, conv_w,
           conv_w_out, ffn_w_gu, ffn_w_down, moe_w_router, moe_w_gu, moe_w_down, ln_gain, ln_bias):
    batch, seq, _ = x_prompt.shape
    dec_batch, dec_seq, _ = x_sample.shape

    w = {
        "ret_in": ret_w_in[0].astype(BF16),
        "ret_out": ret_w_out[0].astype(BF16),
        "conv_in": conv_w_in[0].astype(BF16),
        "conv_w": conv_w[0],
        "conv_out": conv_w_out[0].astype(BF16),
        "ffn_gu": ffn_w_gu[0].astype(BF16),
        "ffn_down": ffn_w_down[0].astype(BF16),
        "router": jnp.pad(moe_w_router[0], ((0, 0), (0, LANES - N_EXPERTS))),
        "moe_gu": moe_w_gu[0].astype(BF16),
        "moe_down": moe_w_down[0].astype(BF16),
        "gain": ln_gain.reshape(2 * DEPTH, 1, D_MODEL),
        "bias": ln_bias.reshape(2 * DEPTH, 1, D_MODEL),
        "inv": (1.0 / (ROPE_BASE ** jnp.linspace(0.0, 1.0, ROPE_HALF, dtype=F32))).reshape(1, ROPE_HALF),
        "log_g": jnp.log1p(-jnp.exp2(-5.0 - jnp.arange(RET_HEADS, dtype=F32))),
    }

    zero_ret = jnp.zeros((batch, RET_HEADS, RET_DK, RET_DV), F32)
    zero_conv = jnp.zeros((batch, CONV_WIDTH - 1, D_MODEL), F32)
    y_p, ret_p, conv_p = _trunk(x_prompt.reshape(batch * seq, D_MODEL), batch, seq, 0,
                                zero_ret, zero_conv, w, tm=512, mc=256, lb=256, nsub=2,
                                tm_moe=1024)
    y_s, ret_s, conv_s = _trunk(x_sample.reshape(dec_batch * dec_seq, D_MODEL), dec_batch, dec_seq,
                                PAST_LEN, state_ret[0], state_conv[0], w,
                                tm=dec_batch * dec_seq, mc=dec_batch * dec_seq, lb=LANES, nsub=1,
                                tm_moe=dec_batch * dec_seq)
    return (y_p.reshape(batch, seq, D_MODEL), y_s.reshape(dec_batch, dec_seq, D_MODEL),
            ret_p[None], ret_s[None], conv_p[None], conv_s[None])
```

```python
import functools
import math

import jax
import jax.numpy as jnp
from jax import lax
from jax.experimental import pallas as pl
from jax.experimental.pallas import tpu as pltpu

D_MODEL = 1024
CHUNK = 64
RET_HEADS = 4
RET_DK = D_MODEL // RET_HEADS
RET_DV = 2 * D_MODEL // RET_HEADS
RET_QK = RET_HEADS * RET_DK
RET_V = RET_HEADS * RET_DV
RET_IN = 2 * RET_QK + 2 * RET_V
ROPE_HALF = RET_DK // 2
CONV_WIDTH = 3
D_FF = 2816
N_EXPERTS = 8
ROPE_BASE = 10000.0
LN_EPS = 1e-5
GN_EPS = 1e-6
DEPTH = 2
DN_ALPHA = (2 * DEPTH) ** 0.25
PAST_LEN = 4096

V7X_VMEM_LIMIT_BYTES = 58 * 1024 * 1024
LANES = 128
SUBLANES = 8

MOE_ROWS_LOG2 = 8
MOE_ROWS = 1 << MOE_ROWS_LOG2
MOE_SINGLE_ROWS = (MOE_ROWS, MOE_ROWS + 48)
MOE_BUFFERS = 3
CUMSUM_CHUNK = 512

F32 = jnp.float32
BF16 = jnp.bfloat16


def _params(*semantics):
    return pltpu.CompilerParams(dimension_semantics=semantics,
                                vmem_limit_bytes=V7X_VMEM_LIMIT_BYTES)


def _resident(shape):
    zeros = (0,) * len(shape)
    return pl.BlockSpec(shape, lambda *_: zeros, pipeline_mode=pl.Buffered(1))


def _layer_norm(z, gain, bias):
    mu = jnp.mean(z, axis=-1, keepdims=True)
    zc = z - mu
    var = jnp.mean(zc * zc, axis=-1, keepdims=True)
    return zc * lax.rsqrt(var + LN_EPS) * gain + bias


def _silu(x):
    return x / (1.0 + jnp.exp(-x))


def _dot(a, b):
    return jnp.dot(a, b, preferred_element_type=F32)


def _rotary(t, cos, sin):
    x1 = t[:, 0:ROPE_HALF]
    x2 = t[:, ROPE_HALF:RET_DK]
    return jnp.concatenate([x1 * cos - x2 * sin, x1 * sin + x2 * cos], axis=-1)


def _ret_mixer_kernel(lg_ref, x_ref, win_ref, inv_ref, r0_ref, wout_ref, gain_ref, bias_ref,
                      o_ref, rout_ref, state, decay, gated, *pads,
                      lb, l_in, chunk, nsub, nstep, pos_base):
    b = pl.program_id(0)
    n = pl.program_id(1)
    padded = l_in < lb

    @pl.when((b == 0) & (n == 0))
    def _():
        ii = lax.broadcasted_iota(jnp.int32, (lb, lb), 0)
        jj = lax.broadcasted_iota(jnp.int32, (lb, lb), 1)
        shift = int(math.log2(chunk))
        visible = (jj >> shift) <= (ii >> shift)
        dist = jnp.abs(ii - jj).astype(F32)
        for h in range(RET_HEADS):
            decay[h] = jnp.where(visible, jnp.exp(lg_ref[h] * dist), 0.0)
        for p in pads:
            p[...] = jnp.zeros_like(p)

    @pl.when(n == 0)
    def _():
        state[...] = r0_ref[...]

    jrow = lax.broadcasted_iota(jnp.int32, (lb, 1), 0).astype(F32)
    for sb in range(nsub):
        rows = slice(sb * l_in, (sb + 1) * l_in)
        x = x_ref[rows, :]
        if padded:
            xpad, = pads
            xpad[0:l_in, :] = x.astype(BF16)
            xb = xpad[...]
        else:
            xb = x.astype(BF16)
        row = (n * nsub + sb) * l_in + lax.broadcasted_iota(jnp.int32, (lb, 1), 0)
        ang = (pos_base + row).astype(F32) * inv_ref[...]
        cos = jnp.cos(ang)
        sin = jnp.sin(ang)
        q_all = _dot(xb, win_ref[:, 0:RET_QK])
        k_all = _dot(xb, win_ref[:, RET_QK:2 * RET_QK])
        v_all = _dot(xb, win_ref[:, 2 * RET_QK:2 * RET_QK + RET_V]).astype(BF16)
        g_all = _dot(xb, win_ref[:, 2 * RET_QK + RET_V:])[0:l_in]

        for h in range(RET_HEADS):
            lg = lg_ref[h]
            qk = slice(h * RET_DK, (h + 1) * RET_DK)
            vv = slice(h * RET_DV, (h + 1) * RET_DV)
            q_h = _rotary(q_all[:, qk], cos, sin).astype(BF16)
            k_h = _rotary(k_all[:, qk], cos, sin) * (RET_DK ** -0.5)
            k_dec = k_h * jnp.exp(lg * (l_in - 1.0 - jrow))
            v_h = v_all[:, vv]
            r_old = state[h]
            s = lax.dot_general(q_h, k_h.astype(BF16), (((1,), (1,)), ((), ())),
                                preferred_element_type=F32)
            p = (s * decay[h]).astype(BF16)
            cross = jnp.exp(lg * (jrow + 1.0))
            o = (_dot(p, v_h) + _dot(q_h, r_old.astype(BF16)) * cross)[0:l_in]
            block_decay = jnp.exp(jnp.full((1, RET_DV), lg * l_in, F32))
            state[h] = r_old * block_decay + _dot(k_dec.T.astype(BF16), v_h)

            mu = jnp.mean(o, axis=-1, keepdims=True)
            oc = o - mu
            var = jnp.mean(oc * oc, axis=-1, keepdims=True)
            on = oc * lax.rsqrt(var + GN_EPS)
            gated[rows, vv] = (_silu(g_all[:, vv]) * on).astype(BF16)

        z = DN_ALPHA * x + _dot(gated[rows, :], wout_ref[...])
        o_ref[rows, :] = _layer_norm(z, gain_ref[...], bias_ref[...])

    @pl.when(n == nstep - 1)
    def _():
        rout_ref[...] = state[...]


def _ret_mixer(x, win_bf, inv, r0, log_g, wout_bf, gain, bias,
               *, n_seq, seq_len, pos_base, lb, nsub):
    l_in = min(seq_len, lb)
    chunk = min(seq_len, CHUNK)
    assert seq_len % (nsub * l_in) == 0 and lb % chunk == 0 and chunk & (chunk - 1) == 0
    nstep = seq_len // (nsub * l_in)
    t = n_seq * seq_len
    row = pl.BlockSpec((nsub * l_in, D_MODEL), lambda b, n: (b * nstep + n, 0))
    st = pl.BlockSpec((None, RET_HEADS, RET_DK, RET_DV), lambda b, n: (b, 0, 0, 0))
    scratch = [pltpu.VMEM((RET_HEADS, RET_DK, RET_DV), F32), pltpu.VMEM((RET_HEADS, lb, lb), F32),
               pltpu.VMEM((nsub * l_in, RET_V), BF16)]
    if l_in < lb:
        scratch += [pltpu.VMEM((lb, D_MODEL), BF16)]
    return pl.pallas_call(
        functools.partial(_ret_mixer_kernel, lb=lb, l_in=l_in, chunk=chunk, nsub=nsub,
                          nstep=nstep, pos_base=pos_base),
        name="ret_mixer",
        grid=(n_seq, nstep),
        in_specs=[pl.BlockSpec(memory_space=pltpu.SMEM), row, _resident((D_MODEL, RET_IN)),
                  _resident((1, ROPE_HALF)), st, _resident((RET_V, D_MODEL)),
                  _resident((1, D_MODEL)), _resident((1, D_MODEL))],
        out_specs=[row, st],
        out_shape=[jax.ShapeDtypeStruct((t, D_MODEL), F32),
                   jax.ShapeDtypeStruct((n_seq, RET_HEADS, RET_DK, RET_DV), F32)],
        scratch_shapes=scratch,
        compiler_params=_params("arbitrary", "arbitrary"),
    )(log_g, x, win_bf, inv, r0, wout_bf, gain, bias)


def _ffn_ln_kernel(x_ref, wgu_ref, wd_ref, gain_ref, bias_ref, o_ref, *, tm, mc):
    for c in range(tm // mc):
        rows = slice(c * mc, (c + 1) * mc)
        x = x_ref[rows, :]
        xb = x.astype(BF16)
        gate = _dot(xb, wgu_ref[:, 0:D_FF])
        up = _dot(xb, wgu_ref[:, D_FF:2 * D_FF])
        hid = (_silu(gate) * up).astype(BF16)
        z = DN_ALPHA * x + _dot(hid, wd_ref[...])
        o_ref[rows, :] = _layer_norm(z, gain_ref[...], bias_ref[...])


def _ffn_ln(x, wgu_bf, wd_bf, gain, bias, *, tm, mc):
    t = x.shape[0]
    assert t % tm == 0 and tm % mc == 0
    return pl.pallas_call(
        functools.partial(_ffn_ln_kernel, tm=tm, mc=mc),
        name="ffn_ln",
        grid=(t // tm,),
        in_specs=[pl.BlockSpec((tm, D_MODEL), lambda i: (i, 0)),
                  _resident((D_MODEL, 2 * D_FF)), _resident((D_FF, D_MODEL)),
                  _resident((1, D_MODEL)), _resident((1, D_MODEL))],
        out_specs=pl.BlockSpec((tm, D_MODEL), lambda i: (i, 0)),
        out_shape=jax.ShapeDtypeStruct((t, D_MODEL), F32),
        compiler_params=_params("parallel"),
    )(x, wgu_bf, wd_bf, gain, bias)


def _conv_ln_kernel(x_ref, win_ref, wc_ref, wout_ref, buf_ref, gain_ref, bias_ref, *rest,
                    tm, mc, seg, whole_seqs, tiles_per_seq, n_cast):
    cast_in = rest[:n_cast]
    o_ref, ob_ref, st_ref = rest[n_cast:n_cast + 3]
    cast_out = rest[n_cast + 3:2 * n_cast + 3]
    u_win, prev, mixed = rest[2 * n_cast + 3:]
    for src, dst in zip(cast_in, cast_out):
        dst[...] = src[...].astype(BF16)
    i = pl.program_id(0)
    w0 = wc_ref[0:1, :]
    w1 = wc_ref[1:2, :]
    w2 = wc_ref[2:3, :]
    tail = CONV_WIDTH - 1
    segs = mc // seg
    for c in range(tm // mc):
        rows = slice(c * mc, (c + 1) * mc)
        x = x_ref[rows, :]
        xb = x.astype(BF16)
        gate_b = _dot(xb, win_ref[:, 0:D_MODEL])
        u = _dot(xb, win_ref[:, D_MODEL:2 * D_MODEL]) * _dot(xb, win_ref[:, 2 * D_MODEL:])
        win = u_win.at[c]
        for s in range(segs):
            sg = c * segs + s
            if whole_seqs:
                prev[SUBLANES - tail:SUBLANES, :] = buf_ref[sg]
            elif sg == 0:
                @pl.when(i % tiles_per_seq == 0)
                def _():
                    prev[SUBLANES - tail:SUBLANES, :] = buf_ref[0]
            win[0:SUBLANES, :] = prev[...]
            win[SUBLANES:SUBLANES + seg, :] = u[s * seg:(s + 1) * seg]
            conv = (w0 * win[SUBLANES - 2:SUBLANES - 2 + seg, :]
                    + w1 * win[SUBLANES - 1:SUBLANES - 1 + seg, :]
                    + w2 * win[SUBLANES:SUBLANES + seg, :])
            mixed[c * mc + s * seg:c * mc + (s + 1) * seg, :] = (
                gate_b[s * seg:(s + 1) * seg] * conv).astype(BF16)
            prev[...] = win[seg:seg + SUBLANES, :]
            if whole_seqs:
                st_ref[sg] = win[SUBLANES + seg - tail:SUBLANES + seg, :]
            elif sg == tm // seg - 1:
                st_ref[0] = win[SUBLANES + seg - tail:SUBLANES + seg, :]
        z = DN_ALPHA * x + _dot(mixed[rows, :], wout_ref[...])
        out = _layer_norm(z, gain_ref[...], bias_ref[...])
        o_ref[rows, :] = out
        ob_ref[rows, :] = out.astype(BF16)


def _conv_ln(x, win_bf, wconv, wout_bf, buf, gain, bias, cast=(), *, n_seq, seq_len, tm, mc):
    t = x.shape[0]
    seg = min(seq_len, mc)
    assert t % tm == 0 and tm % mc == 0 and mc % seg == 0 and seg % SUBLANES == 0
    assert seg == seq_len or seq_len % tm == 0
    tiles_per_seq = max(seq_len // tm, 1)
    seq_per_tile = tm // seg if seg == seq_len else 1
    tail = CONV_WIDTH - 1
    steps = t // tm
    st_spec = pl.BlockSpec((seq_per_tile, tail, D_MODEL), lambda i: (i // tiles_per_seq, 0, 0))
    cast_specs = []
    for a in cast:
        n_lead, n_rows, n_cols = a.shape
        slabs = steps // n_lead
        assert steps % n_lead == 0 and n_rows % (slabs * 2 * SUBLANES) == 0
        cast_specs.append(pl.BlockSpec((None, n_rows // slabs, n_cols),
                                       lambda i, slabs=slabs: (i // slabs, i % slabs, 0)))
    outs = pl.pallas_call(
        functools.partial(_conv_ln_kernel, tm=tm, mc=mc, seg=seg, whole_seqs=seg == seq_len,
                          tiles_per_seq=tiles_per_seq, n_cast=len(cast)),
        name="conv_ln",
        grid=(steps,),
        in_specs=[pl.BlockSpec((tm, D_MODEL), lambda i: (i, 0)),
                  _resident((D_MODEL, 3 * D_MODEL)), _resident((CONV_WIDTH, D_MODEL)),
                  _resident((D_MODEL, D_MODEL)), st_spec,
                  _resident((1, D_MODEL)), _resident((1, D_MODEL))] + cast_specs,
        out_specs=[pl.BlockSpec((tm, D_MODEL), lambda i: (i, 0)),
                   pl.BlockSpec((tm, D_MODEL), lambda i: (i, 0)), st_spec] + cast_specs,
        out_shape=[jax.ShapeDtypeStruct((t, D_MODEL), F32),
                   jax.ShapeDtypeStruct((t, D_MODEL), BF16),
                   jax.ShapeDtypeStruct((n_seq, tail, D_MODEL), F32)]
                  + [jax.ShapeDtypeStruct(a.shape, BF16) for a in cast],
        scratch_shapes=[pltpu.VMEM((tm // mc, SUBLANES + seg, D_MODEL), F32),
                        pltpu.VMEM((SUBLANES, D_MODEL), F32),
                        pltpu.VMEM((tm, D_MODEL), BF16)],
        compiler_params=_params("arbitrary"),
    )(x, win_bf, wconv, wout_bf, buf, gain, bias, *cast)
    return outs[0], outs[1], outs[2], tuple(outs[3:])


def _split_bf16(a):
    hi = a.astype(BF16)
    lo = (a - hi.astype(F32)).astype(BF16)
    return hi, lo


def _router_kernel(x_ref, wr_ref, rank_ref, gate_ref, cnt_ref, *, tm):
    x_hi, x_lo = _split_bf16(x_ref[...])
    w_hi, w_lo = _split_bf16(wr_ref[...])
    logits = _dot(x_hi, w_hi) + (_dot(x_lo, w_hi) + _dot(x_hi, w_lo))
    lt = logits.T[0:N_EXPERTS, :]
    eidx = lax.broadcasted_iota(jnp.int32, (N_EXPERTS, tm), 0).astype(F32)
    neg = jnp.float32(-jnp.inf)
    m1 = jnp.max(lt, axis=0, keepdims=True)
    i1 = jnp.min(jnp.where(lt == m1, eidx, float(N_EXPERTS)), axis=0, keepdims=True)
    first = eidx == i1
    rest = jnp.where(first, neg, lt)
    m2 = jnp.max(rest, axis=0, keepdims=True)
    i2 = jnp.min(jnp.where(rest == m2, eidx, float(N_EXPERTS)), axis=0, keepdims=True)
    second = eidx == i2
    e2 = jnp.exp(m2 - m1)
    w1 = 1.0 / (1.0 + e2)
    w2 = e2 / (1.0 + e2)
    gate_ref[...] = jnp.where(first, w1, 0.0) + jnp.where(second, w2, 0.0)
    sel = first | second
    self32 = jnp.where(sel, 1.0, 0.0)

    c = CUMSUM_CHUNK if tm % CUMSUM_CHUNK == 0 else tm
    si = lax.broadcasted_iota(jnp.int32, (c, c), 0)
    ti = lax.broadcasted_iota(jnp.int32, (c, c), 1)
    tri = jnp.where(si < ti, 1.0, 0.0).astype(BF16)
    offset = jnp.zeros((N_EXPERTS, 1), F32)
    for j in range(tm // c):
        blk = self32[:, j * c:(j + 1) * c]
        rank = _dot(blk.astype(BF16), tri) + offset
        rank_ref[:, j * c:(j + 1) * c] = jnp.where(sel[:, j * c:(j + 1) * c],
                                                   rank.astype(jnp.int32), -1)
        offset = offset + jnp.sum(blk, axis=1, keepdims=True)
    cnt_ref[...] = jnp.broadcast_to(offset.astype(jnp.int32), (N_EXPERTS, LANES))


def _router(x, wr_pad, *, tm):
    t = x.shape[0]
    assert t % tm == 0
    nt = t // tm
    return pl.pallas_call(
        functools.partial(_router_kernel, tm=tm),
        name="router",
        grid=(nt,),
        in_specs=[pl.BlockSpec((tm, D_MODEL), lambda i: (i, 0)), _resident((D_MODEL, LANES))],
        out_specs=[pl.BlockSpec((N_EXPERTS, tm), lambda i: (0, i)),
                   pl.BlockSpec((N_EXPERTS, tm), lambda i: (0, i)),
                   pl.BlockSpec((None, N_EXPERTS, LANES), lambda i: (i, 0, 0))],
        out_shape=[jax.ShapeDtypeStruct((N_EXPERTS, t), jnp.int32),
                   jax.ShapeDtypeStruct((N_EXPERTS, t), F32),
                   jax.ShapeDtypeStruct((nt, N_EXPERTS, LANES), jnp.int32)],
        compiler_params=_params("parallel"),
    )(x, wr_pad)


def _moe_ln_kernel(cnt_ref, xb_ref, x_ref, rank_ref, gate_ref, wgu_ref, wd_ref,
                   gain_ref, bias_ref, out_hbm, buf, rsem, wsem, *, tm, nt):
    e = pl.program_id(0)
    i = pl.program_id(1)
    ne = pl.num_programs(0)
    single_tile = nt == 1
    s = e * nt + i
    slot = 0 if single_tile else lax.rem(s, MOE_BUFFERS)
    o_ref = buf.at[slot]

    def tile_rows(tile):
        start = tile * tm
        if not isinstance(start, int):
            start = pl.multiple_of(start, tm)
        return out_hbm.at[pl.ds(start, tm), :]

    def read_sum(tile, dst):
        return pltpu.make_async_copy(tile_rows(tile), buf.at[dst], rsem.at[dst])

    def write_sum(tile, src):
        return pltpu.make_async_copy(buf.at[src], tile_rows(tile), wsem.at[src])

    if not single_tile:
        next_slot = lax.rem(s + 1, MOE_BUFFERS)

        @pl.when(s >= 2)
        def _():
            write_sum(jnp.where(i >= 2, i - 2, i - 2 + nt), next_slot).wait()

        @pl.when((s + 1 >= nt) & (s + 1 < ne * nt))
        def _():
            read_sum(jnp.where(i + 1 < nt, i + 1, 0), next_slot).start()

        @pl.when(e > 0)
        def _():
            read_sum(i, slot).wait()

    @pl.when(e == 0)
    def _():
        o_ref[...] = jnp.zeros_like(o_ref)

    count = cnt_ref[i * N_EXPERTS + e]
    rank = rank_ref[pl.ds(e, 1), :]
    gate = gate_ref[pl.ds(e, 1), :]

    def expert_rows(row0, nrows):
        rows = row0 + lax.broadcasted_iota(jnp.int32, (nrows, 1), 0)
        hit = rank == rows
        onehot = jnp.where(hit, 1.0, 0.0)
        xg = _dot(onehot.astype(BF16), xb_ref[...]).astype(BF16)
        gcol = jnp.sum(jnp.where(hit, gate, 0.0), axis=1, keepdims=True)
        hid = (_silu(_dot(xg, wgu_ref[:, 0:D_FF])) * _dot(xg, wgu_ref[:, D_FF:])).astype(BF16)
        y = (_dot(hid, wd_ref[...]) * gcol).astype(BF16)
        o_ref[...] += _dot(onehot.T.astype(BF16), y)

    lo = 0
    for nrows in MOE_SINGLE_ROWS:
        @pl.when((count > lo) & (count <= nrows))
        def _(nrows=nrows):
            expert_rows(0, nrows)
        lo = nrows

    @pl.when(count > MOE_SINGLE_ROWS[-1])
    def _():
        def full_block(r, carry):
            expert_rows(r * MOE_ROWS, MOE_ROWS)
            return carry

        lax.fori_loop(0, lax.shift_right_logical(count + (MOE_ROWS - 1), MOE_ROWS_LOG2),
                      full_block, 0)

    @pl.when(e == ne - 1)
    def _():
        z = DN_ALPHA * x_ref[...] + o_ref[...]
        o_ref[...] = _layer_norm(z, gain_ref[...], bias_ref[...])

    if single_tile:
        @pl.when(e == ne - 1)
        def _():
            done = write_sum(0, 0)
            done.start()
            done.wait()
    else:
        write_sum(i, slot).start()

        @pl.when(s == ne * nt - 1)
        def _():
            write_sum(i - 1, lax.rem(s + MOE_BUFFERS - 1, MOE_BUFFERS)).wait()
            write_sum(i, slot).wait()


def _moe_ln(xb, x, rank, gate, counts, wgu_bf, wd_bf, gain, bias, *, tm):
    t = x.shape[0]
    assert t % tm == 0
    nt = t // tm
    assert nt == 1 or nt >= MOE_BUFFERS
    last_e = N_EXPERTS - 1
    x_map = lambda e, i, c: (jnp.where(e == last_e, i, 0), 0)
    per_tile = pl.BlockSpec((N_EXPERTS, tm), lambda e, i, c: (0, i))
    const = lambda shape: pl.BlockSpec(shape, lambda e, i, c: (0,) * len(shape),
                                       pipeline_mode=pl.Buffered(1))
    grid_spec = pltpu.PrefetchScalarGridSpec(
        num_scalar_prefetch=1,
        grid=(N_EXPERTS, nt),
        in_specs=[pl.BlockSpec((tm, D_MODEL), lambda e, i, c: (i, 0)),
                  pl.BlockSpec((tm, D_MODEL), x_map, pipeline_mode=pl.Buffered(1)),
                  per_tile, per_tile,
                  pl.BlockSpec((None, D_MODEL, 2 * D_FF), lambda e, i, c: (e, 0, 0),
                               pipeline_mode=pl.Buffered(1)),
                  pl.BlockSpec((None, D_FF, D_MODEL), lambda e, i, c: (e, 0, 0),
                               pipeline_mode=pl.Buffered(1)),
                  const((1, D_MODEL)), const((1, D_MODEL))],
        out_specs=pl.BlockSpec(memory_space=pl.ANY),
        scratch_shapes=[pltpu.VMEM((1 if nt == 1 else MOE_BUFFERS, tm, D_MODEL), F32),
                        pltpu.SemaphoreType.DMA((MOE_BUFFERS,)),
                        pltpu.SemaphoreType.DMA((MOE_BUFFERS,))],
    )
    return pl.pallas_call(
        functools.partial(_moe_ln_kernel, tm=tm, nt=nt),
        name="moe_ln",
        grid_spec=grid_spec,
        out_shape=jax.ShapeDtypeStruct((t, D_MODEL), F32),
        compiler_params=_params("arbitrary", "arbitrary"),
    )(counts, xb, x, rank, gate, wgu_bf, wd_bf, gain, bias)


def _trunk(x, n_seq, seq_len, pos_base, r0, conv_buf, w, moe_w, *, tm, mc, lb, nsub, tm_moe):
    x, r_new = _ret_mixer(x, w["ret_in"], w["inv"], r0, w["log_g"], w["ret_out"], w["gain"][0],
                          w["bias"][0], n_seq=n_seq, seq_len=seq_len, pos_base=pos_base, lb=lb,
                          nsub=nsub)
    x = _ffn_ln(x, w["ffn_gu"], w["ffn_down"], w["gain"][1], w["bias"][1], tm=tm, mc=mc)
    cast = tuple(moe_w) if moe_w[0].dtype == F32 else ()
    x, xb, conv_new, cast_out = _conv_ln(x, w["conv_in"], w["conv_w"], w["conv_out"], conv_buf,
                                         w["gain"][2], w["bias"][2], cast, n_seq=n_seq,
                                         seq_len=seq_len, tm=tm, mc=mc)
    moe_gu, moe_down = cast_out if cast else moe_w
    rank, gate, counts = _router(x, w["router"], tm=tm_moe)
    y = _moe_ln(xb, x, rank, gate, counts[:, :, 0].reshape(-1), moe_gu, moe_down,
                w["gain"][3], w["bias"][3], tm=tm_moe)
    return y, r_new, conv_new, (moe_gu, moe_down)


def kernel(x_prompt, x_sample, state_ret, state_conv, ret_w_in, ret_w_out, conv_w_in, conv_w,
           conv_w_out, ffn_w_gu, ffn_w_down, moe_w_router, moe_w_gu, moe_w_down, ln_gain, ln_bias):
    batch, seq, _ = x_prompt.shape
    dec_batch, dec_seq, _ = x_sample.shape

    w = {
        "ret_in": ret_w_in[0].astype(BF16),
        "ret_out": ret_w_out[0].astype(BF16),
        "conv_in": conv_w_in[0].astype(BF16),
        "conv_w": conv_w[0],
        "conv_out": conv_w_out[0].astype(BF16),
        "ffn_gu": ffn_w_gu[0].astype(BF16),
        "ffn_down": ffn_w_down[0].astype(BF16),
        "router": jnp.pad(moe_w_router[0], ((0, 0), (0, LANES - N_EXPERTS))),
        "gain": ln_gain.reshape(2 * DEPTH, 1, D_MODEL),
        "bias": ln_bias.reshape(2 * DEPTH, 1, D_MODEL),
        "inv": (1.0 / (ROPE_BASE ** jnp.linspace(0.0, 1.0, ROPE_HALF, dtype=F32))).reshape(1, ROPE_HALF),
        "log_g": jnp.log1p(-jnp.exp2(-5.0 - jnp.arange(RET_HEADS, dtype=F32))),
    }

    zero_ret = jnp.zeros((batch, RET_HEADS, RET_DK, RET_DV), F32)
    zero_conv = jnp.zeros((batch, CONV_WIDTH - 1, D_MODEL), F32)
    y_p, ret_p, conv_p, moe_bf = _trunk(
        x_prompt.reshape(batch * seq, D_MODEL), batch, seq, 0, zero_ret, zero_conv, w,
        (moe_w_gu[0], moe_w_down[0]), tm=512, mc=256, lb=256, nsub=2, tm_moe=1024)
    y_s, ret_s, conv_s, _ = _trunk(
        x_sample.reshape(dec_batch * dec_seq, D_MODEL), dec_batch, dec_seq, PAST_LEN,
        state_ret[0], state_conv[0], w, moe_bf, tm=dec_batch * dec_seq, mc=dec_batch * dec_seq,
        lb=LANES, nsub=1, tm_moe=dec_batch * dec_seq)
    return (y_p.reshape(batch, seq, D_MODEL), y_s.reshape(dec_batch, dec_seq, D_MODEL),
            ret_p[None], ret_s[None], conv_p[None], conv_s[None])
```

```python
import functools
import math

import jax
import jax.numpy as jnp
from jax import lax
from jax.experimental import pallas as pl
from jax.experimental.pallas import tpu as pltpu

D_MODEL = 1024
CHUNK = 64
RET_HEADS = 4
RET_DK = D_MODEL // RET_HEADS
RET_DV = 2 * D_MODEL // RET_HEADS
RET_QK = RET_HEADS * RET_DK
RET_V = RET_HEADS * RET_DV
RET_IN = 2 * RET_QK + 2 * RET_V
ROPE_HALF = RET_DK // 2
CONV_WIDTH = 3
D_FF = 2816
N_EXPERTS = 8
ROPE_BASE = 10000.0
LN_EPS = 1e-5
GN_EPS = 1e-6
DEPTH = 2
DN_ALPHA = (2 * DEPTH) ** 0.25
PAST_LEN = 4096

V7X_VMEM_LIMIT_BYTES = 58 * 1024 * 1024
LANES = 128
SUBLANES = 8

MOE_ROWS_LOG2 = 8
MOE_ROWS = 1 << MOE_ROWS_LOG2
TOP_K = 2
MOE_SPARE_ROWS = 48
MOE_BUFFERS = 3
CUMSUM_CHUNK = 512

F32 = jnp.float32
BF16 = jnp.bfloat16


def _params(*semantics):
    return pltpu.CompilerParams(dimension_semantics=semantics,
                                vmem_limit_bytes=V7X_VMEM_LIMIT_BYTES)


def _resident(shape):
    zeros = (0,) * len(shape)
    return pl.BlockSpec(shape, lambda *_: zeros, pipeline_mode=pl.Buffered(1))


def _layer_norm(z, gain, bias):
    mu = jnp.mean(z, axis=-1, keepdims=True)
    zc = z - mu
    var = jnp.mean(zc * zc, axis=-1, keepdims=True)
    return zc * lax.rsqrt(var + LN_EPS) * gain + bias


def _silu(x):
    return x / (1.0 + jnp.exp(-x))


def _dot(a, b):
    return jnp.dot(a, b, preferred_element_type=F32)


def _rotary(t, cos, sin):
    x1 = t[:, 0:ROPE_HALF]
    x2 = t[:, ROPE_HALF:RET_DK]
    return jnp.concatenate([x1 * cos - x2 * sin, x1 * sin + x2 * cos], axis=-1)


def _ret_mixer_kernel(lg_ref, x_ref, win_ref, inv_ref, r0_ref, wout_ref, gain_ref, bias_ref,
                      o_ref, rout_ref, state, decay, gated, *pads,
                      lb, l_in, chunk, nsub, nstep, pos_base):
    b = pl.program_id(0)
    n = pl.program_id(1)
    padded = l_in < lb

    @pl.when((b == 0) & (n == 0))
    def _():
        ii = lax.broadcasted_iota(jnp.int32, (lb, lb), 0)
        jj = lax.broadcasted_iota(jnp.int32, (lb, lb), 1)
        shift = int(math.log2(chunk))
        visible = (jj >> shift) <= (ii >> shift)
        dist = jnp.abs(ii - jj).astype(F32)
        for h in range(RET_HEADS):
            decay[h] = jnp.where(visible, jnp.exp(lg_ref[h] * dist), 0.0)
        for p in pads:
            p[...] = jnp.zeros_like(p)

    @pl.when(n == 0)
    def _():
        state[...] = r0_ref[...]

    jrow = lax.broadcasted_iota(jnp.int32, (lb, 1), 0).astype(F32)
    for sb in range(nsub):
        rows = slice(sb * l_in, (sb + 1) * l_in)
        x = x_ref[rows, :]
        if padded:
            xpad, = pads
            xpad[0:l_in, :] = x.astype(BF16)
            xb = xpad[...]
        else:
            xb = x.astype(BF16)
        row = (n * nsub + sb) * l_in + lax.broadcasted_iota(jnp.int32, (lb, 1), 0)
        ang = (pos_base + row).astype(F32) * inv_ref[...]
        cos = jnp.cos(ang)
        sin = jnp.sin(ang)
        q_all = _dot(xb, win_ref[:, 0:RET_QK])
        k_all = _dot(xb, win_ref[:, RET_QK:2 * RET_QK])
        v_all = _dot(xb, win_ref[:, 2 * RET_QK:2 * RET_QK + RET_V]).astype(BF16)
        g_all = _dot(xb, win_ref[:, 2 * RET_QK + RET_V:])[0:l_in]

        for h in range(RET_HEADS):
            lg = lg_ref[h]
            qk = slice(h * RET_DK, (h + 1) * RET_DK)
            vv = slice(h * RET_DV, (h + 1) * RET_DV)
            q_h = _rotary(q_all[:, qk], cos, sin).astype(BF16)
            k_h = _rotary(k_all[:, qk], cos, sin) * (RET_DK ** -0.5)
            k_dec = k_h * jnp.exp(lg * (l_in - 1.0 - jrow))
            v_h = v_all[:, vv]
            r_old = state[h]
            s = lax.dot_general(q_h, k_h.astype(BF16), (((1,), (1,)), ((), ())),
                                preferred_element_type=F32)
            p = (s * decay[h]).astype(BF16)
            cross = jnp.exp(lg * (jrow + 1.0))
            o = (_dot(p, v_h) + _dot(q_h, r_old.astype(BF16)) * cross)[0:l_in]
            block_decay = jnp.exp(jnp.full((1, RET_DV), lg * l_in, F32))
            state[h] = r_old * block_decay + _dot(k_dec.T.astype(BF16), v_h)

            mu = jnp.mean(o, axis=-1, keepdims=True)
            oc = o - mu
            var = jnp.mean(oc * oc, axis=-1, keepdims=True)
            on = oc * lax.rsqrt(var + GN_EPS)
            gated[rows, vv] = (_silu(g_all[:, vv]) * on).astype(BF16)

        z = DN_ALPHA * x + _dot(gated[rows, :], wout_ref[...])
        o_ref[rows, :] = _layer_norm(z, gain_ref[...], bias_ref[...])

    @pl.when(n == nstep - 1)
    def _():
        rout_ref[...] = state[...]


def _ret_mixer(x, win_bf, inv, r0, log_g, wout_bf, gain, bias,
               *, n_seq, seq_len, pos_base, lb, nsub):
    l_in = min(seq_len, lb)
    chunk = min(seq_len, CHUNK)
    assert seq_len % (nsub * l_in) == 0 and lb % chunk == 0 and chunk & (chunk - 1) == 0
    nstep = seq_len // (nsub * l_in)
    t = n_seq * seq_len
    row = pl.BlockSpec((nsub * l_in, D_MODEL), lambda b, n: (b * nstep + n, 0))
    st = pl.BlockSpec((None, RET_HEADS, RET_DK, RET_DV), lambda b, n: (b, 0, 0, 0))
    scratch = [pltpu.VMEM((RET_HEADS, RET_DK, RET_DV), F32), pltpu.VMEM((RET_HEADS, lb, lb), F32),
               pltpu.VMEM((nsub * l_in, RET_V), BF16)]
    if l_in < lb:
        scratch += [pltpu.VMEM((lb, D_MODEL), BF16)]
    return pl.pallas_call(
        functools.partial(_ret_mixer_kernel, lb=lb, l_in=l_in, chunk=chunk, nsub=nsub,
                          nstep=nstep, pos_base=pos_base),
        name="ret_mixer",
        grid=(n_seq, nstep),
        in_specs=[pl.BlockSpec(memory_space=pltpu.SMEM), row, _resident((D_MODEL, RET_IN)),
                  _resident((1, ROPE_HALF)), st, _resident((RET_V, D_MODEL)),
                  _resident((1, D_MODEL)), _resident((1, D_MODEL))],
        out_specs=[row, st],
        out_shape=[jax.ShapeDtypeStruct((t, D_MODEL), F32),
                   jax.ShapeDtypeStruct((n_seq, RET_HEADS, RET_DK, RET_DV), F32)],
        scratch_shapes=scratch,
        compiler_params=_params("arbitrary", "arbitrary"),
    )(log_g, x, win_bf, inv, r0, wout_bf, gain, bias)


def _ffn_ln_kernel(x_ref, wgu_ref, wd_ref, gain_ref, bias_ref, o_ref, *, tm, mc):
    for c in range(tm // mc):
        rows = slice(c * mc, (c + 1) * mc)
        x = x_ref[rows, :]
        xb = x.astype(BF16)
        gate = _dot(xb, wgu_ref[:, 0:D_FF])
        up = _dot(xb, wgu_ref[:, D_FF:2 * D_FF])
        hid = (_silu(gate) * up).astype(BF16)
        z = DN_ALPHA * x + _dot(hid, wd_ref[...])
        o_ref[rows, :] = _layer_norm(z, gain_ref[...], bias_ref[...])


def _ffn_ln(x, wgu_bf, wd_bf, gain, bias, *, tm, mc):
    t = x.shape[0]
    assert t % tm == 0 and tm % mc == 0
    return pl.pallas_call(
        functools.partial(_ffn_ln_kernel, tm=tm, mc=mc),
        name="ffn_ln",
        grid=(t // tm,),
        in_specs=[pl.BlockSpec((tm, D_MODEL), lambda i: (i, 0)),
                  _resident((D_MODEL, 2 * D_FF)), _resident((D_FF, D_MODEL)),
                  _resident((1, D_MODEL)), _resident((1, D_MODEL))],
        out_specs=pl.BlockSpec((tm, D_MODEL), lambda i: (i, 0)),
        out_shape=jax.ShapeDtypeStruct((t, D_MODEL), F32),
        compiler_params=_params("parallel"),
    )(x, wgu_bf, wd_bf, gain, bias)


def _conv_ln_kernel(x_ref, win_ref, wc_ref, wout_ref, buf_ref, gain_ref, bias_ref, *rest,
                    tm, mc, seg, whole_seqs, tiles_per_seq, n_cast):
    cast_in = rest[:n_cast]
    o_ref, ob_ref, st_ref = rest[n_cast:n_cast + 3]
    cast_out = rest[n_cast + 3:2 * n_cast + 3]
    u_win, prev, mixed = rest[2 * n_cast + 3:]
    for src, dst in zip(cast_in, cast_out):
        dst[...] = src[...].astype(BF16)
    i = pl.program_id(0)
    w0 = wc_ref[0:1, :]
    w1 = wc_ref[1:2, :]
    w2 = wc_ref[2:3, :]
    tail = CONV_WIDTH - 1
    segs = mc // seg
    for c in range(tm // mc):
        rows = slice(c * mc, (c + 1) * mc)
        x = x_ref[rows, :]
        xb = x.astype(BF16)
        gate_b = _dot(xb, win_ref[:, 0:D_MODEL])
        u = _dot(xb, win_ref[:, D_MODEL:2 * D_MODEL]) * _dot(xb, win_ref[:, 2 * D_MODEL:])
        win = u_win.at[c]
        for s in range(segs):
            sg = c * segs + s
            if whole_seqs:
                prev[SUBLANES - tail:SUBLANES, :] = buf_ref[sg]
            elif sg == 0:
                @pl.when(i % tiles_per_seq == 0)
                def _():
                    prev[SUBLANES - tail:SUBLANES, :] = buf_ref[0]
            win[0:SUBLANES, :] = prev[...]
            win[SUBLANES:SUBLANES + seg, :] = u[s * seg:(s + 1) * seg]
            conv = (w0 * win[SUBLANES - 2:SUBLANES - 2 + seg, :]
                    + w1 * win[SUBLANES - 1:SUBLANES - 1 + seg, :]
                    + w2 * win[SUBLANES:SUBLANES + seg, :])
            mixed[c * mc + s * seg:c * mc + (s + 1) * seg, :] = (
                gate_b[s * seg:(s + 1) * seg] * conv).astype(BF16)
            prev[...] = win[seg:seg + SUBLANES, :]
            if whole_seqs:
                st_ref[sg] = win[SUBLANES + seg - tail:SUBLANES + seg, :]
            elif sg == tm // seg - 1:
                st_ref[0] = win[SUBLANES + seg - tail:SUBLANES + seg, :]
        z = DN_ALPHA * x + _dot(mixed[rows, :], wout_ref[...])
        out = _layer_norm(z, gain_ref[...], bias_ref[...])
        o_ref[rows, :] = out
        ob_ref[rows, :] = out.astype(BF16)


def _conv_ln(x, win_bf, wconv, wout_bf, buf, gain, bias, cast=(), *, n_seq, seq_len, tm, mc):
    t = x.shape[0]
    seg = min(seq_len, mc)
    assert t % tm == 0 and tm % mc == 0 and mc % seg == 0 and seg % SUBLANES == 0
    assert seg == seq_len or seq_len % tm == 0
    tiles_per_seq = max(seq_len // tm, 1)
    seq_per_tile = tm // seg if seg == seq_len else 1
    tail = CONV_WIDTH - 1
    steps = t // tm
    st_spec = pl.BlockSpec((seq_per_tile, tail, D_MODEL), lambda i: (i // tiles_per_seq, 0, 0))
    cast_specs = []
    for a in cast:
        n_lead, n_rows, n_cols = a.shape
        slabs = steps // n_lead
        assert steps % n_lead == 0 and n_rows % (slabs * 2 * SUBLANES) == 0
        cast_specs.append(pl.BlockSpec((None, n_rows // slabs, n_cols),
                                       lambda i, slabs=slabs: (i // slabs, i % slabs, 0)))
    outs = pl.pallas_call(
        functools.partial(_conv_ln_kernel, tm=tm, mc=mc, seg=seg, whole_seqs=seg == seq_len,
                          tiles_per_seq=tiles_per_seq, n_cast=len(cast)),
        name="conv_ln",
        grid=(steps,),
        in_specs=[pl.BlockSpec((tm, D_MODEL), lambda i: (i, 0)),
                  _resident((D_MODEL, 3 * D_MODEL)), _resident((CONV_WIDTH, D_MODEL)),
                  _resident((D_MODEL, D_MODEL)), st_spec,
                  _resident((1, D_MODEL)), _resident((1, D_MODEL))] + cast_specs,
        out_specs=[pl.BlockSpec((tm, D_MODEL), lambda i: (i, 0)),
                   pl.BlockSpec((tm, D_MODEL), lambda i: (i, 0)), st_spec] + cast_specs,
        out_shape=[jax.ShapeDtypeStruct((t, D_MODEL), F32),
                   jax.ShapeDtypeStruct((t, D_MODEL), BF16),
                   jax.ShapeDtypeStruct((n_seq, tail, D_MODEL), F32)]
                  + [jax.ShapeDtypeStruct(a.shape, BF16) for a in cast],
        scratch_shapes=[pltpu.VMEM((tm // mc, SUBLANES + seg, D_MODEL), F32),
                        pltpu.VMEM((SUBLANES, D_MODEL), F32),
                        pltpu.VMEM((tm, D_MODEL), BF16)],
        compiler_params=_params("arbitrary"),
    )(x, win_bf, wconv, wout_bf, buf, gain, bias, *cast)
    return outs[0], outs[1], outs[2], tuple(outs[3:])


def _split_bf16(a):
    hi = a.astype(BF16)
    lo = (a - hi.astype(F32)).astype(BF16)
    return hi, lo


def _router_kernel(x_ref, wr_ref, rank_ref, gate_ref, cnt_ref, *, tm):
    x_hi, x_lo = _split_bf16(x_ref[...])
    w_hi, w_lo = _split_bf16(wr_ref[...])
    logits = _dot(x_hi, w_hi) + (_dot(x_lo, w_hi) + _dot(x_hi, w_lo))
    lt = logits.T[0:N_EXPERTS, :]
    eidx = lax.broadcasted_iota(jnp.int32, (N_EXPERTS, tm), 0).astype(F32)
    neg = jnp.float32(-jnp.inf)
    m1 = jnp.max(lt, axis=0, keepdims=True)
    i1 = jnp.min(jnp.where(lt == m1, eidx, float(N_EXPERTS)), axis=0, keepdims=True)
    first = eidx == i1
    rest = jnp.where(first, neg, lt)
    m2 = jnp.max(rest, axis=0, keepdims=True)
    i2 = jnp.min(jnp.where(rest == m2, eidx, float(N_EXPERTS)), axis=0, keepdims=True)
    second = eidx == i2
    e2 = jnp.exp(m2 - m1)
    w1 = 1.0 / (1.0 + e2)
    w2 = e2 / (1.0 + e2)
    gate_ref[...] = jnp.where(first, w1, 0.0) + jnp.where(second, w2, 0.0)
    sel = first | second
    self32 = jnp.where(sel, 1.0, 0.0)

    c = CUMSUM_CHUNK if tm % CUMSUM_CHUNK == 0 else tm
    si = lax.broadcasted_iota(jnp.int32, (c, c), 0)
    ti = lax.broadcasted_iota(jnp.int32, (c, c), 1)
    tri = jnp.where(si < ti, 1.0, 0.0).astype(BF16)
    offset = jnp.zeros((N_EXPERTS, 1), F32)
    for j in range(tm // c):
        blk = self32[:, j * c:(j + 1) * c]
        rank = _dot(blk.astype(BF16), tri) + offset
        rank_ref[:, j * c:(j + 1) * c] = jnp.where(sel[:, j * c:(j + 1) * c],
                                                   rank.astype(jnp.int32), -1)
        offset = offset + jnp.sum(blk, axis=1, keepdims=True)
    cnt_ref[...] = jnp.broadcast_to(offset.astype(jnp.int32), (N_EXPERTS, LANES))


def _router(x, wr_pad, *, tm):
    t = x.shape[0]
    assert t % tm == 0
    nt = t // tm
    return pl.pallas_call(
        functools.partial(_router_kernel, tm=tm),
        name="router",
        grid=(nt,),
        in_specs=[pl.BlockSpec((tm, D_MODEL), lambda i: (i, 0)), _resident((D_MODEL, LANES))],
        out_specs=[pl.BlockSpec((N_EXPERTS, tm), lambda i: (0, i)),
                   pl.BlockSpec((N_EXPERTS, tm), lambda i: (0, i)),
                   pl.BlockSpec((None, N_EXPERTS, LANES), lambda i: (i, 0, 0))],
        out_shape=[jax.ShapeDtypeStruct((N_EXPERTS, t), jnp.int32),
                   jax.ShapeDtypeStruct((N_EXPERTS, t), F32),
                   jax.ShapeDtypeStruct((nt, N_EXPERTS, LANES), jnp.int32)],
        compiler_params=_params("parallel"),
    )(x, wr_pad)


def _moe_ln_kernel(cnt_ref, xb_ref, x_ref, rank_ref, gate_ref, wgu_ref, wd_ref,
                   gain_ref, bias_ref, out_hbm, buf, rsem, wsem, *, tm, nt):
    e = pl.program_id(0)
    i = pl.program_id(1)
    ne = pl.num_programs(0)
    single_tile = nt == 1
    s = e * nt + i
    slot = 0 if single_tile else lax.rem(s, MOE_BUFFERS)
    o_ref = buf.at[slot]

    def tile_rows(tile):
        start = tile * tm
        if not isinstance(start, int):
            start = pl.multiple_of(start, tm)
        return out_hbm.at[pl.ds(start, tm), :]

    def read_sum(tile, dst):
        return pltpu.make_async_copy(tile_rows(tile), buf.at[dst], rsem.at[dst])

    def write_sum(tile, src):
        return pltpu.make_async_copy(buf.at[src], tile_rows(tile), wsem.at[src])

    if not single_tile:
        next_slot = lax.rem(s + 1, MOE_BUFFERS)

        @pl.when(s >= 2)
        def _():
            write_sum(jnp.where(i >= 2, i - 2, i - 2 + nt), next_slot).wait()

        @pl.when((s + 1 >= nt) & (s + 1 < ne * nt))
        def _():
            read_sum(jnp.where(i + 1 < nt, i + 1, 0), next_slot).start()

        @pl.when(e > 0)
        def _():
            read_sum(i, slot).wait()

    @pl.when(e == 0)
    def _():
        o_ref[...] = jnp.zeros_like(o_ref)

    count = cnt_ref[i * N_EXPERTS + e]
    rank = rank_ref[pl.ds(e, 1), :]
    gate = gate_ref[pl.ds(e, 1), :]

    def expert_rows(row0, nrows):
        rows = row0 + lax.broadcasted_iota(jnp.int32, (nrows, 1), 0)
        hit = rank == rows
        onehot = jnp.where(hit, 1.0, 0.0)
        xg = _dot(onehot.astype(BF16), xb_ref[...]).astype(BF16)
        gcol = jnp.sum(jnp.where(hit, gate, 0.0), axis=1, keepdims=True)
        hid = (_silu(_dot(xg, wgu_ref[:, 0:D_FF])) * _dot(xg, wgu_ref[:, D_FF:])).astype(BF16)
        y = (_dot(hid, wd_ref[...]) * gcol).astype(BF16)
        o_ref[...] += _dot(onehot.T.astype(BF16), y)

    mean_rows = tm * TOP_K // N_EXPERTS
    single_rows = (mean_rows, mean_rows + MOE_SPARE_ROWS)
    lo = 0
    for nrows in single_rows:
        @pl.when((count > lo) & (count <= nrows))
        def _(nrows=nrows):
            expert_rows(0, nrows)
        lo = nrows

    @pl.when(count > single_rows[-1])
    def _():
        def full_block(r, carry):
            expert_rows(r * MOE_ROWS, MOE_ROWS)
            return carry

        lax.fori_loop(0, lax.shift_right_logical(count + (MOE_ROWS - 1), MOE_ROWS_LOG2),
                      full_block, 0)

    @pl.when(e == ne - 1)
    def _():
        z = DN_ALPHA * x_ref[...] + o_ref[...]
        o_ref[...] = _layer_norm(z, gain_ref[...], bias_ref[...])

    if single_tile:
        @pl.when(e == ne - 1)
        def _():
            done = write_sum(0, 0)
            done.start()
            done.wait()
    else:
        write_sum(i, slot).start()

        @pl.when(s == ne * nt - 1)
        def _():
            write_sum(i - 1, lax.rem(s + MOE_BUFFERS - 1, MOE_BUFFERS)).wait()
            write_sum(i, slot).wait()


def _moe_vmem_bytes(tm, weight_buffers):
    rows = tm * TOP_K // N_EXPERTS + MOE_SPARE_ROWS
    weights = weight_buffers * 3 * D_MODEL * D_FF * 2
    tiles = tm * D_MODEL * (MOE_BUFFERS * 4 + 4 + 2 * 2)
    values = rows * D_FF * (4 + 4 + 2) + rows * tm * (4 + 2) * 2 + tm * D_MODEL * 4
    return weights + tiles + values


def _moe_ln(xb, x, rank, gate, counts, wgu_bf, wd_bf, gain, bias, *, tm):
    t = x.shape[0]
    assert t % tm == 0
    nt = t // tm
    assert nt == 1 or nt >= MOE_BUFFERS
    last_e = N_EXPERTS - 1
    x_map = lambda e, i, c: (jnp.where(e == last_e, i, 0), 0)
    per_tile = pl.BlockSpec((N_EXPERTS, tm), lambda e, i, c: (0, i))
    const = lambda shape: pl.BlockSpec(shape, lambda e, i, c: (0,) * len(shape),
                                       pipeline_mode=pl.Buffered(1))
    weight_buffers = 2 if _moe_vmem_bytes(tm, 2) <= V7X_VMEM_LIMIT_BYTES else 1
    grid_spec = pltpu.PrefetchScalarGridSpec(
        num_scalar_prefetch=1,
        grid=(N_EXPERTS, nt),
        in_specs=[pl.BlockSpec((tm, D_MODEL), lambda e, i, c: (i, 0)),
                  pl.BlockSpec((tm, D_MODEL), x_map, pipeline_mode=pl.Buffered(1)),
                  per_tile, per_tile,
                  pl.BlockSpec((None, D_MODEL, 2 * D_FF), lambda e, i, c: (e, 0, 0),
                               pipeline_mode=pl.Buffered(weight_buffers)),
                  pl.BlockSpec((None, D_FF, D_MODEL), lambda e, i, c: (e, 0, 0),
                               pipeline_mode=pl.Buffered(weight_buffers)),
                  const((1, D_MODEL)), const((1, D_MODEL))],
        out_specs=pl.BlockSpec(memory_space=pl.ANY),
        scratch_shapes=[pltpu.VMEM((1 if nt == 1 else MOE_BUFFERS, tm, D_MODEL), F32),
                        pltpu.SemaphoreType.DMA((MOE_BUFFERS,)),
                        pltpu.SemaphoreType.DMA((MOE_BUFFERS,))],
    )
    return pl.pallas_call(
        functools.partial(_moe_ln_kernel, tm=tm, nt=nt),
        name="moe_ln",
        grid_spec=grid_spec,
        out_shape=jax.ShapeDtypeStruct((t, D_MODEL), F32),
        compiler_params=_params("arbitrary", "arbitrary"),
    )(counts, xb, x, rank, gate, wgu_bf, wd_bf, gain, bias)


def _trunk(x, n_seq, seq_len, pos_base, r0, conv_buf, w, moe_w,
           *, tm, tm_ffn, mc, lb, nsub, tm_moe):
    x, r_new = _ret_mixer(x, w["ret_in"], w["inv"], r0, w["log_g"], w["ret_out"], w["gain"][0],
                          w["bias"][0], n_seq=n_seq, seq_len=seq_len, pos_base=pos_base, lb=lb,
                          nsub=nsub)
    x = _ffn_ln(x, w["ffn_gu"], w["ffn_down"], w["gain"][1], w["bias"][1], tm=tm_ffn, mc=mc)
    cast = tuple(moe_w) if moe_w[0].dtype == F32 else ()
    x, xb, conv_new, cast_out = _conv_ln(x, w["conv_in"], w["conv_w"], w["conv_out"], conv_buf,
                                         w["gain"][2], w["bias"][2], cast, n_seq=n_seq,
                                         seq_len=seq_len, tm=tm, mc=mc)
    moe_gu, moe_down = cast_out if cast else moe_w
    rank, gate, counts = _router(x, w["router"], tm=tm_moe)
    y = _moe_ln(xb, x, rank, gate, counts[:, :, 0].reshape(-1), moe_gu, moe_down,
                w["gain"][3], w["bias"][3], tm=tm_moe)
    return y, r_new, conv_new, (moe_gu, moe_down)


def kernel(x_prompt, x_sample, state_ret, state_conv, ret_w_in, ret_w_out, conv_w_in, conv_w,
           conv_w_out, ffn_w_gu, ffn_w_down, moe_w_router, moe_w_gu, moe_w_down, ln_gain, ln_bias):
    batch, seq, _ = x_prompt.shape
    dec_batch, dec_seq, _ = x_sample.shape

    w = {
        "ret_in": ret_w_in[0].astype(BF16),
        "ret_out": ret_w_out[0].astype(BF16),
        "conv_in": conv_w_in[0].astype(BF16),
        "conv_w": conv_w[0],
        "conv_out": conv_w_out[0].astype(BF16),
        "ffn_gu": ffn_w_gu[0].astype(BF16),
        "ffn_down": ffn_w_down[0].astype(BF16),
        "router": jnp.pad(moe_w_router[0], ((0, 0), (0, LANES - N_EXPERTS))),
        "gain": ln_gain.reshape(2 * DEPTH, 1, D_MODEL),
        "bias": ln_bias.reshape(2 * DEPTH, 1, D_MODEL),
        "inv": (1.0 / (ROPE_BASE ** jnp.linspace(0.0, 1.0, ROPE_HALF, dtype=F32))).reshape(1, ROPE_HALF),
        "log_g": jnp.log1p(-jnp.exp2(-5.0 - jnp.arange(RET_HEADS, dtype=F32))),
    }

    zero_ret = jnp.zeros((batch, RET_HEADS, RET_DK, RET_DV), F32)
    zero_conv = jnp.zeros((batch, CONV_WIDTH - 1, D_MODEL), F32)
    y_p, ret_p, conv_p, moe_bf = _trunk(
        x_prompt.reshape(batch * seq, D_MODEL), batch, seq, 0, zero_ret, zero_conv, w,
        (moe_w_gu[0], moe_w_down[0]), tm=512, tm_ffn=1024, mc=256, lb=256, nsub=2, tm_moe=1024)
    y_s, ret_s, conv_s, _ = _trunk(
        x_sample.reshape(dec_batch * dec_seq, D_MODEL), dec_batch, dec_seq, PAST_LEN,
        state_ret[0], state_conv[0], w, moe_bf, tm=dec_batch * dec_seq,
        tm_ffn=dec_batch * dec_seq, mc=dec_batch * dec_seq,
        lb=LANES, nsub=1, tm_moe=dec_batch * dec_seq)
    return (y_p.reshape(batch, seq, D_MODEL), y_s.reshape(dec_batch, dec_seq, D_MODEL),
            ret_p[None], ret_s[None], conv_p[None], conv_s[None])
```

```python
import functools
import math

import jax
import jax.numpy as jnp
from jax import lax
from jax.experimental import pallas as pl
from jax.experimental.pallas import tpu as pltpu

D_MODEL = 1024
CHUNK = 64
RET_HEADS = 4
RET_DK = D_MODEL // RET_HEADS
RET_DV = 2 * D_MODEL // RET_HEADS
RET_QK = RET_HEADS * RET_DK
RET_V = RET_HEADS * RET_DV
RET_IN = 2 * RET_QK + 2 * RET_V
ROPE_HALF = RET_DK // 2
CONV_WIDTH = 3
D_FF = 2816
N_EXPERTS = 8
ROPE_BASE = 10000.0
LN_EPS = 1e-5
GN_EPS = 1e-6
DEPTH = 2
DN_ALPHA = (2 * DEPTH) ** 0.25
PAST_LEN = 4096

V7X_VMEM_LIMIT_BYTES = 58 * 1024 * 1024
LANES = 128
SUBLANES = 8

MOE_ROWS_LOG2 = 8
MOE_ROWS = 1 << MOE_ROWS_LOG2
TOP_K = 2
MOE_SPARE_ROWS = 48
MOE_BUFFERS = 3
CUMSUM_CHUNK = 512

F32 = jnp.float32
BF16 = jnp.bfloat16


def _params(*semantics):
    return pltpu.CompilerParams(dimension_semantics=semantics,
                                vmem_limit_bytes=V7X_VMEM_LIMIT_BYTES)


def _resident(shape):
    zeros = (0,) * len(shape)
    return pl.BlockSpec(shape, lambda *_: zeros, pipeline_mode=pl.Buffered(1))


def _layer_norm(z, gain, bias):
    mu = jnp.mean(z, axis=-1, keepdims=True)
    zc = z - mu
    var = jnp.mean(zc * zc, axis=-1, keepdims=True)
    return zc * lax.rsqrt(var + LN_EPS) * gain + bias


def _silu(x):
    return x / (1.0 + jnp.exp(-x))


def _dot(a, b):
    return jnp.dot(a, b, preferred_element_type=F32)


def _rotary(t, cos, sin):
    x1 = t[:, 0:ROPE_HALF]
    x2 = t[:, ROPE_HALF:RET_DK]
    return jnp.concatenate([x1 * cos - x2 * sin, x1 * sin + x2 * cos], axis=-1)


def _ret_mixer_kernel(lg_ref, x_ref, win_ref, inv_ref, r0_ref, wout_ref, gain_ref, bias_ref,
                      o_ref, rout_ref, state, decay, gated, *pads,
                      lb, l_in, chunk, nsub, nstep, pos_base):
    b = pl.program_id(0)
    n = pl.program_id(1)
    padded = l_in < lb

    @pl.when((b == 0) & (n == 0))
    def _():
        ii = lax.broadcasted_iota(jnp.int32, (lb, lb), 0)
        jj = lax.broadcasted_iota(jnp.int32, (lb, lb), 1)
        shift = int(math.log2(chunk))
        visible = (jj >> shift) <= (ii >> shift)
        dist = jnp.abs(ii - jj).astype(F32)
        for h in range(RET_HEADS):
            decay[h] = jnp.where(visible, jnp.exp(lg_ref[h] * dist), 0.0)
        for p in pads:
            p[...] = jnp.zeros_like(p)

    @pl.when(n == 0)
    def _():
        state[...] = r0_ref[...]

    jrow = lax.broadcasted_iota(jnp.int32, (lb, 1), 0).astype(F32)
    for sb in range(nsub):
        rows = slice(sb * l_in, (sb + 1) * l_in)
        x = x_ref[rows, :]
        if padded:
            xpad, = pads
            xpad[0:l_in, :] = x.astype(BF16)
            xb = xpad[...]
        else:
            xb = x.astype(BF16)
        row = (n * nsub + sb) * l_in + lax.broadcasted_iota(jnp.int32, (lb, 1), 0)
        ang = (pos_base + row).astype(F32) * inv_ref[...]
        cos = jnp.cos(ang)
        sin = jnp.sin(ang)
        q_all = _dot(xb, win_ref[:, 0:RET_QK])
        k_all = _dot(xb, win_ref[:, RET_QK:2 * RET_QK])
        v_all = _dot(xb, win_ref[:, 2 * RET_QK:2 * RET_QK + RET_V]).astype(BF16)
        g_all = _dot(xb, win_ref[:, 2 * RET_QK + RET_V:])[0:l_in]

        for h in range(RET_HEADS):
            lg = lg_ref[h]
            qk = slice(h * RET_DK, (h + 1) * RET_DK)
            vv = slice(h * RET_DV, (h + 1) * RET_DV)
            q_h = _rotary(q_all[:, qk], cos, sin).astype(BF16)
            k_h = _rotary(k_all[:, qk], cos, sin) * (RET_DK ** -0.5)
            k_dec = k_h * jnp.exp(lg * (l_in - 1.0 - jrow))
            v_h = v_all[:, vv]
            r_old = state[h]
            s = lax.dot_general(q_h, k_h.astype(BF16), (((1,), (1,)), ((), ())),
                                preferred_element_type=F32)
            p = (s * decay[h]).astype(BF16)
            cross = jnp.exp(lg * (jrow + 1.0))
            o = (_dot(p, v_h) + _dot(q_h, r_old.astype(BF16)) * cross)[0:l_in]
            block_decay = jnp.exp(jnp.full((1, RET_DV), lg * l_in, F32))
            state[h] = r_old * block_decay + _dot(k_dec.T.astype(BF16), v_h)

            mu = jnp.mean(o, axis=-1, keepdims=True)
            oc = o - mu
            var = jnp.mean(oc * oc, axis=-1, keepdims=True)
            on = oc * lax.rsqrt(var + GN_EPS)
            gated[rows, vv] = (_silu(g_all[:, vv]) * on).astype(BF16)

        z = DN_ALPHA * x + _dot(gated[rows, :], wout_ref[...])
        o_ref[rows, :] = _layer_norm(z, gain_ref[...], bias_ref[...])

    @pl.when(n == nstep - 1)
    def _():
        rout_ref[...] = state[...]


def _ret_mixer(x, win_bf, inv, r0, log_g, wout_bf, gain, bias,
               *, n_seq, seq_len, pos_base, lb, nsub):
    l_in = min(seq_len, lb)
    chunk = min(seq_len, CHUNK)
    assert seq_len % (nsub * l_in) == 0 and lb % chunk == 0 and chunk & (chunk - 1) == 0
    nstep = seq_len // (nsub * l_in)
    t = n_seq * seq_len
    row = pl.BlockSpec((nsub * l_in, D_MODEL), lambda b, n: (b * nstep + n, 0))
    st = pl.BlockSpec((None, RET_HEADS, RET_DK, RET_DV), lambda b, n: (b, 0, 0, 0))
    scratch = [pltpu.VMEM((RET_HEADS, RET_DK, RET_DV), F32), pltpu.VMEM((RET_HEADS, lb, lb), F32),
               pltpu.VMEM((nsub * l_in, RET_V), BF16)]
    if l_in < lb:
        scratch += [pltpu.VMEM((lb, D_MODEL), BF16)]
    return pl.pallas_call(
        functools.partial(_ret_mixer_kernel, lb=lb, l_in=l_in, chunk=chunk, nsub=nsub,
                          nstep=nstep, pos_base=pos_base),
        name="ret_mixer",
        grid=(n_seq, nstep),
        in_specs=[pl.BlockSpec(memory_space=pltpu.SMEM), row, _resident((D_MODEL, RET_IN)),
                  _resident((1, ROPE_HALF)), st, _resident((RET_V, D_MODEL)),
                  _resident((1, D_MODEL)), _resident((1, D_MODEL))],
        out_specs=[row, st],
        out_shape=[jax.ShapeDtypeStruct((t, D_MODEL), F32),
                   jax.ShapeDtypeStruct((n_seq, RET_HEADS, RET_DK, RET_DV), F32)],
        scratch_shapes=scratch,
        compiler_params=_params("arbitrary", "arbitrary"),
    )(log_g, x, win_bf, inv, r0, wout_bf, gain, bias)


def _ffn_ln_kernel(x_ref, wgu_ref, wd_ref, gain_ref, bias_ref, o_ref, *, tm, mc):
    for c in range(tm // mc):
        rows = slice(c * mc, (c + 1) * mc)
        x = x_ref[rows, :]
        xb = x.astype(BF16)
        gate = _dot(xb, wgu_ref[:, 0:D_FF])
        up = _dot(xb, wgu_ref[:, D_FF:2 * D_FF])
        hid = (_silu(gate) * up).astype(BF16)
        z = DN_ALPHA * x + _dot(hid, wd_ref[...])
        o_ref[rows, :] = _layer_norm(z, gain_ref[...], bias_ref[...])


def _ffn_ln(x, wgu_bf, wd_bf, gain, bias, *, tm, mc):
    t = x.shape[0]
    assert t % tm == 0 and tm % mc == 0
    return pl.pallas_call(
        functools.partial(_ffn_ln_kernel, tm=tm, mc=mc),
        name="ffn_ln",
        grid=(t // tm,),
        in_specs=[pl.BlockSpec((tm, D_MODEL), lambda i: (i, 0)),
                  _resident((D_MODEL, 2 * D_FF)), _resident((D_FF, D_MODEL)),
                  _resident((1, D_MODEL)), _resident((1, D_MODEL))],
        out_specs=pl.BlockSpec((tm, D_MODEL), lambda i: (i, 0)),
        out_shape=jax.ShapeDtypeStruct((t, D_MODEL), F32),
        compiler_params=_params("parallel"),
    )(x, wgu_bf, wd_bf, gain, bias)


def _conv_ln_kernel(x_ref, win_ref, wc_ref, wout_ref, buf_ref, gain_ref, bias_ref, *rest,
                    tm, mc, seg, whole_seqs, tiles_per_seq, n_cast):
    cast_in = rest[:n_cast]
    o_ref, ob_ref, st_ref = rest[n_cast:n_cast + 3]
    cast_out = rest[n_cast + 3:2 * n_cast + 3]
    u_win, prev, mixed = rest[2 * n_cast + 3:]
    for src, dst in zip(cast_in, cast_out):
        dst[...] = src[...].astype(BF16)
    i = pl.program_id(0)
    w0 = wc_ref[0:1, :]
    w1 = wc_ref[1:2, :]
    w2 = wc_ref[2:3, :]
    tail = CONV_WIDTH - 1
    segs = mc // seg
    for c in range(tm // mc):
        rows = slice(c * mc, (c + 1) * mc)
        x = x_ref[rows, :]
        xb = x.astype(BF16)
        gate_b = _dot(xb, win_ref[:, 0:D_MODEL])
        u = _dot(xb, win_ref[:, D_MODEL:2 * D_MODEL]) * _dot(xb, win_ref[:, 2 * D_MODEL:])
        win = u_win.at[c]
        for s in range(segs):
            sg = c * segs + s
            if whole_seqs:
                prev[SUBLANES - tail:SUBLANES, :] = buf_ref[sg]
            elif sg == 0:
                @pl.when(i % tiles_per_seq == 0)
                def _():
                    prev[SUBLANES - tail:SUBLANES, :] = buf_ref[0]
            win[0:SUBLANES, :] = prev[...]
            win[SUBLANES:SUBLANES + seg, :] = u[s * seg:(s + 1) * seg]
            conv = (w0 * win[SUBLANES - 2:SUBLANES - 2 + seg, :]
                    + w1 * win[SUBLANES - 1:SUBLANES - 1 + seg, :]
                    + w2 * win[SUBLANES:SUBLANES + seg, :])
            mixed[c * mc + s * seg:c * mc + (s + 1) * seg, :] = (
                gate_b[s * seg:(s + 1) * seg] * conv).astype(BF16)
            prev[...] = win[seg:seg + SUBLANES, :]
            if whole_seqs:
                st_ref[sg] = win[SUBLANES + seg - tail:SUBLANES + seg, :]
            elif sg == tm // seg - 1:
                st_ref[0] = win[SUBLANES + seg - tail:SUBLANES + seg, :]
        z = DN_ALPHA * x + _dot(mixed[rows, :], wout_ref[...])
        out = _layer_norm(z, gain_ref[...], bias_ref[...])
        o_ref[rows, :] = out
        ob_ref[rows, :] = out.astype(BF16)


def _conv_ln(x, win_bf, wconv, wout_bf, buf, gain, bias, cast=(), *, n_seq, seq_len, tm, mc):
    t = x.shape[0]
    seg = min(seq_len, mc)
    assert t % tm == 0 and tm % mc == 0 and mc % seg == 0 and seg % SUBLANES == 0
    assert seg == seq_len or seq_len % tm == 0
    tiles_per_seq = max(seq_len // tm, 1)
    seq_per_tile = tm // seg if seg == seq_len else 1
    tail = CONV_WIDTH - 1
    steps = t // tm
    st_spec = pl.BlockSpec((seq_per_tile, tail, D_MODEL), lambda i: (i // tiles_per_seq, 0, 0))
    cast_specs = []
    for a in cast:
        n_lead, n_rows, n_cols = a.shape
        slabs = steps // n_lead
        assert steps % n_lead == 0 and n_rows % (slabs * 2 * SUBLANES) == 0
        cast_specs.append(pl.BlockSpec((None, n_rows // slabs, n_cols),
                                       lambda i, slabs=slabs: (i // slabs, i % slabs, 0)))
    outs = pl.pallas_call(
        functools.partial(_conv_ln_kernel, tm=tm, mc=mc, seg=seg, whole_seqs=seg == seq_len,
                          tiles_per_seq=tiles_per_seq, n_cast=len(cast)),
        name="conv_ln",
        grid=(steps,),
        in_specs=[pl.BlockSpec((tm, D_MODEL), lambda i: (i, 0)),
                  _resident((D_MODEL, 3 * D_MODEL)), _resident((CONV_WIDTH, D_MODEL)),
                  _resident((D_MODEL, D_MODEL)), st_spec,
                  _resident((1, D_MODEL)), _resident((1, D_MODEL))] + cast_specs,
        out_specs=[pl.BlockSpec((tm, D_MODEL), lambda i: (i, 0)),
                   pl.BlockSpec((tm, D_MODEL), lambda i: (i, 0)), st_spec] + cast_specs,
        out_shape=[jax.ShapeDtypeStruct((t, D_MODEL), F32),
                   jax.ShapeDtypeStruct((t, D_MODEL), BF16),
                   jax.ShapeDtypeStruct((n_seq, tail, D_MODEL), F32)]
                  + [jax.ShapeDtypeStruct(a.shape, BF16) for a in cast],
        scratch_shapes=[pltpu.VMEM((tm // mc, SUBLANES + seg, D_MODEL), F32),
                        pltpu.VMEM((SUBLANES, D_MODEL), F32),
                        pltpu.VMEM((tm, D_MODEL), BF16)],
        compiler_params=_params("arbitrary"),
    )(x, win_bf, wconv, wout_bf, buf, gain, bias, *cast)
    return outs[0], outs[1], outs[2], tuple(outs[3:])


def _split_bf16(a):
    hi = a.astype(BF16)
    lo = (a - hi.astype(F32)).astype(BF16)
    return hi, lo


def _router_kernel(x_ref, wr_ref, rank_ref, gate_ref, cnt_ref, *, tm):
    x_hi, x_lo = _split_bf16(x_ref[...])
    w_hi, w_lo = _split_bf16(wr_ref[...])
    hi = _dot(x_hi, jnp.concatenate([w_hi, w_lo], axis=1))
    logits = hi[:, 0:LANES] + (_dot(x_lo, w_hi) + hi[:, LANES:])
    lt = logits.T[0:N_EXPERTS, :]
    eidx = lax.broadcasted_iota(jnp.int32, (N_EXPERTS, tm), 0).astype(F32)
    neg = jnp.float32(-jnp.inf)
    m1 = jnp.max(lt, axis=0, keepdims=True)
    i1 = jnp.min(jnp.where(lt == m1, eidx, float(N_EXPERTS)), axis=0, keepdims=True)
    first = eidx == i1
    rest = jnp.where(first, neg, lt)
    m2 = jnp.max(rest, axis=0, keepdims=True)
    i2 = jnp.min(jnp.where(rest == m2, eidx, float(N_EXPERTS)), axis=0, keepdims=True)
    second = eidx == i2
    e2 = jnp.exp(m2 - m1)
    w1 = 1.0 / (1.0 + e2)
    w2 = e2 / (1.0 + e2)
    gate_ref[...] = jnp.where(first, w1, 0.0) + jnp.where(second, w2, 0.0)
    sel = first | second
    self32 = jnp.where(sel, 1.0, 0.0)

    c = CUMSUM_CHUNK if tm % CUMSUM_CHUNK == 0 else tm
    si = lax.broadcasted_iota(jnp.int32, (c, c), 0)
    ti = lax.broadcasted_iota(jnp.int32, (c, c), 1)
    tri = jnp.where(si < ti, 1.0, 0.0).astype(BF16)
    offset = jnp.zeros((N_EXPERTS, 1), F32)
    for j in range(tm // c):
        blk = self32[:, j * c:(j + 1) * c]
        rank = _dot(blk.astype(BF16), tri) + offset
        rank_ref[:, j * c:(j + 1) * c] = jnp.where(sel[:, j * c:(j + 1) * c],
                                                   rank.astype(jnp.int32), -1)
        offset = offset + jnp.sum(blk, axis=1, keepdims=True)
    cnt_ref[...] = jnp.broadcast_to(offset.astype(jnp.int32), (N_EXPERTS, LANES))


def _router(x, wr_pad, *, tm):
    t = x.shape[0]
    assert t % tm == 0
    nt = t // tm
    return pl.pallas_call(
        functools.partial(_router_kernel, tm=tm),
        name="router",
        grid=(nt,),
        in_specs=[pl.BlockSpec((tm, D_MODEL), lambda i: (i, 0)), _resident((D_MODEL, LANES))],
        out_specs=[pl.BlockSpec((N_EXPERTS, tm), lambda i: (0, i)),
                   pl.BlockSpec((N_EXPERTS, tm), lambda i: (0, i)),
                   pl.BlockSpec((None, N_EXPERTS, LANES), lambda i: (i, 0, 0))],
        out_shape=[jax.ShapeDtypeStruct((N_EXPERTS, t), jnp.int32),
                   jax.ShapeDtypeStruct((N_EXPERTS, t), F32),
                   jax.ShapeDtypeStruct((nt, N_EXPERTS, LANES), jnp.int32)],
        compiler_params=_params("parallel"),
    )(x, wr_pad)


def _moe_ln_kernel(cnt_ref, xb_ref, x_hbm, rank_ref, gate_ref, wgu_ref, wd_ref,
                   gain_ref, bias_ref, out_hbm, buf, rsem, wsem, *, tm, nt):
    e = pl.program_id(0)
    i = pl.program_id(1)
    ne = pl.num_programs(0)
    single_tile = nt == 1
    s = e * nt + i
    slot = 0 if single_tile else lax.rem(s, MOE_BUFFERS)
    o_ref = buf.at[slot]

    def tile_rows(hbm, tile):
        start = tile * tm
        if not isinstance(start, int):
            start = pl.multiple_of(start, tm)
        return hbm.at[pl.ds(start, tm), :]

    def read_x(tile, dst):
        return pltpu.make_async_copy(tile_rows(x_hbm, tile), buf.at[dst], rsem.at[dst])

    def read_sum(tile, dst):
        return pltpu.make_async_copy(tile_rows(out_hbm, tile), buf.at[dst], rsem.at[dst])

    def write_sum(tile, src):
        return pltpu.make_async_copy(buf.at[src], tile_rows(out_hbm, tile), wsem.at[src])

    if single_tile:
        @pl.when(e == 0)
        def _():
            first = read_x(0, 0)
            first.start()
            first.wait()
    else:
        next_slot = lax.rem(s + 1, MOE_BUFFERS)
        next_tile = jnp.where(i + 1 < nt, i + 1, 0)

        @pl.when(s == 0)
        def _():
            read_x(0, 0).start()

        @pl.when(s >= 2)
        def _():
            write_sum(jnp.where(i >= 2, i - 2, i - 2 + nt), next_slot).wait()

        @pl.when(s + 1 < nt)
        def _():
            read_x(next_tile, next_slot).start()

        @pl.when((s + 1 >= nt) & (s + 1 < ne * nt))
        def _():
            read_sum(next_tile, next_slot).start()

        @pl.when(e == 0)
        def _():
            read_x(i, slot).wait()

        @pl.when(e > 0)
        def _():
            read_sum(i, slot).wait()

    count = cnt_ref[i * N_EXPERTS + e]
    rank = rank_ref[pl.ds(e, 1), :]
    gate = gate_ref[pl.ds(e, 1), :]

    def expert_rows(row0, nrows):
        rows = row0 + lax.broadcasted_iota(jnp.int32, (nrows, 1), 0)
        hit = rank == rows
        onehot = jnp.where(hit, 1.0, 0.0)
        xg = _dot(onehot.astype(BF16), xb_ref[...]).astype(BF16)
        gcol = jnp.sum(jnp.where(hit, gate, 0.0), axis=1, keepdims=True) * (1.0 / DN_ALPHA)
        hid = (_silu(_dot(xg, wgu_ref[:, 0:D_FF])) * _dot(xg, wgu_ref[:, D_FF:])).astype(BF16)
        y = (_dot(hid, wd_ref[...]) * gcol).astype(BF16)
        o_ref[...] += _dot(onehot.T.astype(BF16), y)

    mean_rows = tm * TOP_K // N_EXPERTS
    single_rows = (mean_rows, mean_rows + MOE_SPARE_ROWS)
    lo = 0
    for nrows in single_rows:
        @pl.when((count > lo) & (count <= nrows))
        def _(nrows=nrows):
            expert_rows(0, nrows)
        lo = nrows

    @pl.when(count > single_rows[-1])
    def _():
        def full_block(r, carry):
            expert_rows(r * MOE_ROWS, MOE_ROWS)
            return carry

        lax.fori_loop(0, lax.shift_right_logical(count + (MOE_ROWS - 1), MOE_ROWS_LOG2),
                      full_block, 0)

    @pl.when(e == ne - 1)
    def _():
        o_ref[...] = _layer_norm(DN_ALPHA * o_ref[...], gain_ref[...], bias_ref[...])

    if single_tile:
        @pl.when(e == ne - 1)
        def _():
            done = write_sum(0, 0)
            done.start()
            done.wait()
    else:
        write_sum(i, slot).start()

        @pl.when(s == ne * nt - 1)
        def _():
            write_sum(i - 1, lax.rem(s + MOE_BUFFERS - 1, MOE_BUFFERS)).wait()
            write_sum(i, slot).wait()


def _moe_vmem_bytes(tm, weight_buffers):
    rows = tm * TOP_K // N_EXPERTS + MOE_SPARE_ROWS
    weights = weight_buffers * 3 * D_MODEL * D_FF * 2
    tiles = tm * D_MODEL * (MOE_BUFFERS * 4 + 2 * 2)
    values = rows * D_FF * (4 + 4 + 2) + rows * tm * (4 + 2) * 2 + tm * D_MODEL * 4
    return weights + tiles + values


def _moe_ln(xb, x, rank, gate, counts, wgu_bf, wd_bf, gain, bias, *, tm):
    t = x.shape[0]
    assert t % tm == 0
    nt = t // tm
    assert nt == 1 or nt >= MOE_BUFFERS
    per_tile = pl.BlockSpec((N_EXPERTS, tm), lambda e, i, c: (0, i))
    const = lambda shape: pl.BlockSpec(shape, lambda e, i, c: (0,) * len(shape),
                                       pipeline_mode=pl.Buffered(1))
    weight_buffers = 2 if _moe_vmem_bytes(tm, 2) <= V7X_VMEM_LIMIT_BYTES else 1
    grid_spec = pltpu.PrefetchScalarGridSpec(
        num_scalar_prefetch=1,
        grid=(N_EXPERTS, nt),
        in_specs=[pl.BlockSpec((tm, D_MODEL), lambda e, i, c: (i, 0)),
                  pl.BlockSpec(memory_space=pl.ANY), per_tile, per_tile,
                  pl.BlockSpec((None, D_MODEL, 2 * D_FF), lambda e, i, c: (e, 0, 0),
                               pipeline_mode=pl.Buffered(weight_buffers)),
                  pl.BlockSpec((None, D_FF, D_MODEL), lambda e, i, c: (e, 0, 0),
                               pipeline_mode=pl.Buffered(weight_buffers)),
                  const((1, D_MODEL)), const((1, D_MODEL))],
        out_specs=pl.BlockSpec(memory_space=pl.ANY),
        scratch_shapes=[pltpu.VMEM((1 if nt == 1 else MOE_BUFFERS, tm, D_MODEL), F32),
                        pltpu.SemaphoreType.DMA((MOE_BUFFERS,)),
                        pltpu.SemaphoreType.DMA((MOE_BUFFERS,))],
    )
    return pl.pallas_call(
        functools.partial(_moe_ln_kernel, tm=tm, nt=nt),
        name="moe_ln",
        grid_spec=grid_spec,
        out_shape=jax.ShapeDtypeStruct((t, D_MODEL), F32),
        compiler_params=_params("arbitrary", "arbitrary"),
    )(counts, xb, x, rank, gate, wgu_bf, wd_bf, gain, bias)


def _trunk(x, n_seq, seq_len, pos_base, r0, conv_buf, w, moe_w,
           *, tm, tm_ffn, mc, lb, nsub, tm_moe):
    x, r_new = _ret_mixer(x, w["ret_in"], w["inv"], r0, w["log_g"], w["ret_out"], w["gain"][0],
                          w["bias"][0], n_seq=n_seq, seq_len=seq_len, pos_base=pos_base, lb=lb,
                          nsub=nsub)
    x = _ffn_ln(x, w["ffn_gu"], w["ffn_down"], w["gain"][1], w["bias"][1], tm=tm_ffn, mc=mc)
    cast = tuple(moe_w) if moe_w[0].dtype == F32 else ()
    x, xb, conv_new, cast_out = _conv_ln(x, w["conv_in"], w["conv_w"], w["conv_out"], conv_buf,
                                         w["gain"][2], w["bias"][2], cast, n_seq=n_seq,
                                         seq_len=seq_len, tm=tm, mc=mc)
    moe_gu, moe_down = cast_out if cast else moe_w
    rank, gate, counts = _router(x, w["router"], tm=tm_moe)
    y = _moe_ln(xb, x, rank, gate, counts[:, :, 0].reshape(-1), moe_gu, moe_down,
                w["gain"][3], w["bias"][3], tm=tm_moe)
    return y, r_new, conv_new, (moe_gu, moe_down)


def kernel(x_prompt, x_sample, state_ret, state_conv, ret_w_in, ret_w_out, conv_w_in, conv_w,
           conv_w_out, ffn_w_gu, ffn_w_down, moe_w_router, moe_w_gu, moe_w_down, ln_gain, ln_bias):
    batch, seq, _ = x_prompt.shape
    dec_batch, dec_seq, _ = x_sample.shape

    w = {
        "ret_in": ret_w_in[0].astype(BF16),
        "ret_out": ret_w_out[0].astype(BF16),
        "conv_in": conv_w_in[0].astype(BF16),
        "conv_w": conv_w[0],
        "conv_out": conv_w_out[0].astype(BF16),
        "ffn_gu": ffn_w_gu[0].astype(BF16),
        "ffn_down": ffn_w_down[0].astype(BF16),
        "router": jnp.pad(moe_w_router[0], ((0, 0), (0, LANES - N_EXPERTS))),
        "gain": ln_gain.reshape(2 * DEPTH, 1, D_MODEL),
        "bias": ln_bias.reshape(2 * DEPTH, 1, D_MODEL),
        "inv": (1.0 / (ROPE_BASE ** jnp.linspace(0.0, 1.0, ROPE_HALF, dtype=F32))).reshape(1, ROPE_HALF),
        "log_g": jnp.log1p(-jnp.exp2(-5.0 - jnp.arange(RET_HEADS, dtype=F32))),
    }

    zero_ret = jnp.zeros((batch, RET_HEADS, RET_DK, RET_DV), F32)
    zero_conv = jnp.zeros((batch, CONV_WIDTH - 1, D_MODEL), F32)
    y_p, ret_p, conv_p, moe_bf = _trunk(
        x_prompt.reshape(batch * seq, D_MODEL), batch, seq, 0, zero_ret, zero_conv, w,
        (moe_w_gu[0], moe_w_down[0]), tm=512, tm_ffn=1024, mc=256, lb=256, nsub=2, tm_moe=1024)
    y_s, ret_s, conv_s, _ = _trunk(
        x_sample.reshape(dec_batch * dec_seq, D_MODEL), dec_batch, dec_seq, PAST_LEN,
        state_ret[0], state_conv[0], w, moe_bf, tm=dec_batch * dec_seq,
        tm_ffn=dec_batch * dec_seq, mc=dec_batch * dec_seq,
        lb=LANES, nsub=1, tm_moe=dec_batch * dec_seq)
    return (y_p.reshape(batch, seq, D_MODEL), y_s.reshape(dec_batch, dec_seq, D_MODEL),
            ret_p[None], ret_s[None], conv_p[None], conv_s[None])
```

```python
import functools
import math

import jax
import jax.numpy as jnp
from jax import lax
from jax.experimental import pallas as pl
from jax.experimental.pallas import tpu as pltpu

D_MODEL = 1024
CHUNK = 64
RET_HEADS = 4
RET_DK = D_MODEL // RET_HEADS
RET_DV = 2 * D_MODEL // RET_HEADS
RET_QK = RET_HEADS * RET_DK
RET_V = RET_HEADS * RET_DV
RET_IN = 2 * RET_QK + 2 * RET_V
ROPE_HALF = RET_DK // 2
CONV_WIDTH = 3
D_FF = 2816
N_EXPERTS = 8
ROPE_BASE = 10000.0
LN_EPS = 1e-5
GN_EPS = 1e-6
DEPTH = 2
DN_ALPHA = (2 * DEPTH) ** 0.25
PAST_LEN = 4096

V7X_VMEM_LIMIT_BYTES = 58 * 1024 * 1024
LANES = 128
SUBLANES = 8

MOE_ROWS_LOG2 = 8
MOE_ROWS = 1 << MOE_ROWS_LOG2
TOP_K = 2
MOE_SPARE_ROWS = 32
MOE_BUFFERS = 3
CUMSUM_CHUNK = 512

F32 = jnp.float32
BF16 = jnp.bfloat16


def _params(*semantics):
    return pltpu.CompilerParams(dimension_semantics=semantics,
                                vmem_limit_bytes=V7X_VMEM_LIMIT_BYTES)


def _resident(shape):
    zeros = (0,) * len(shape)
    return pl.BlockSpec(shape, lambda *_: zeros, pipeline_mode=pl.Buffered(1))


def _layer_norm(z, gain, bias):
    mu = jnp.mean(z, axis=-1, keepdims=True)
    zc = z - mu
    var = jnp.mean(zc * zc, axis=-1, keepdims=True)
    return zc * lax.rsqrt(var + LN_EPS) * gain + bias


def _silu(x):
    return x / (1.0 + jnp.exp(-x))


def _dot(a, b):
    return jnp.dot(a, b, preferred_element_type=F32)


def _rotary(t, cos, sin):
    x1 = t[:, 0:ROPE_HALF]
    x2 = t[:, ROPE_HALF:RET_DK]
    return jnp.concatenate([x1 * cos - x2 * sin, x1 * sin + x2 * cos], axis=-1)


def _build_decay(decay, lg_ref, lb, chunk):
    ii = lax.broadcasted_iota(jnp.int32, (lb, lb), 0)
    jj = lax.broadcasted_iota(jnp.int32, (lb, lb), 1)
    shift = int(math.log2(chunk))
    visible = (jj >> shift) <= (ii >> shift)
    dist = jnp.abs(ii - jj).astype(F32)
    for h in range(RET_HEADS):
        decay[h] = jnp.where(visible, jnp.exp(lg_ref[h] * dist), 0.0)


def _retain_head(q_b, k_f, v_b, state, decay, h, lg, l_in):
    jrow = lax.broadcasted_iota(jnp.int32, (q_b.shape[0], 1), 0).astype(F32)
    k_dec = k_f * jnp.exp(lg * (l_in - 1.0 - jrow))
    r_old = state[h]
    s = lax.dot_general(q_b, k_f.astype(BF16), (((1,), (1,)), ((), ())),
                        preferred_element_type=F32)
    p = (s * decay[h]).astype(BF16)
    cross = jnp.exp(lg * (jrow + 1.0))
    o = (_dot(p, v_b) + _dot(q_b, r_old.astype(BF16)) * cross)[0:l_in]
    block_decay = jnp.exp(jnp.full((1, RET_DV), lg * l_in, F32))
    state[h] = r_old * block_decay + _dot(k_dec.T.astype(BF16), v_b)
    mu = jnp.mean(o, axis=-1, keepdims=True)
    oc = o - mu
    var = jnp.mean(oc * oc, axis=-1, keepdims=True)
    return oc * lax.rsqrt(var + GN_EPS)


def _ret_mixer_kernel(lg_ref, x_ref, win_ref, inv_ref, r0_ref, wout_ref, gain_ref, bias_ref,
                      o_ref, rout_ref, state, decay, gated, *, lb, chunk, nsub, nstep, pos_base):
    b = pl.program_id(0)
    n = pl.program_id(1)

    @pl.when((b == 0) & (n == 0))
    def _():
        _build_decay(decay, lg_ref, lb, chunk)

    @pl.when(n == 0)
    def _():
        state[...] = r0_ref[...]

    for sb in range(nsub):
        rows = slice(sb * lb, (sb + 1) * lb)
        x = x_ref[rows, :]
        xb = x.astype(BF16)
        row = (n * nsub + sb) * lb + lax.broadcasted_iota(jnp.int32, (lb, 1), 0)
        ang = (pos_base + row).astype(F32) * inv_ref[...]
        cos = jnp.cos(ang)
        sin = jnp.sin(ang)
        q_all = _dot(xb, win_ref[:, 0:RET_QK])
        k_all = _dot(xb, win_ref[:, RET_QK:2 * RET_QK])
        v_all = _dot(xb, win_ref[:, 2 * RET_QK:2 * RET_QK + RET_V]).astype(BF16)
        g_all = _dot(xb, win_ref[:, 2 * RET_QK + RET_V:])
        for h in range(RET_HEADS):
            qk = slice(h * RET_DK, (h + 1) * RET_DK)
            vv = slice(h * RET_DV, (h + 1) * RET_DV)
            q_h = _rotary(q_all[:, qk], cos, sin).astype(BF16)
            k_h = _rotary(k_all[:, qk], cos, sin) * (RET_DK ** -0.5)
            on = _retain_head(q_h, k_h, v_all[:, vv], state, decay, h, lg_ref[h], lb)
            gated[rows, vv] = (_silu(g_all[:, vv]) * on).astype(BF16)
        z = DN_ALPHA * x + _dot(gated[rows, :], wout_ref[...])
        o_ref[rows, :] = _layer_norm(z, gain_ref[...], bias_ref[...])

    @pl.when(n == nstep - 1)
    def _():
        rout_ref[...] = state[...]


def _ret_mixer_short_kernel(lg_ref, x_ref, win_ref, inv_ref, r0_ref, wout_ref, gain_ref, bias_ref,
                            o_ref, rout_ref, state, decay, q_s, k_s, v_s, g_s, gated,
                            q_pad, k_pad, v_pad, *, lb, seq_len, n_seq, pos_base):
    b = pl.program_id(0)

    @pl.when(b == 0)
    def _():
        _build_decay(decay, lg_ref, lb, seq_len)
        for pad in (q_pad, k_pad, v_pad):
            pad[...] = jnp.zeros_like(pad)
        xb = x_ref[...].astype(BF16)
        row = lax.broadcasted_iota(jnp.int32, (xb.shape[0], 1), 0)
        ang = (pos_base + (row & (seq_len - 1))).astype(F32) * inv_ref[...]
        cos = jnp.cos(ang)
        sin = jnp.sin(ang)
        q_all = _dot(xb, win_ref[:, 0:RET_QK])
        k_all = _dot(xb, win_ref[:, RET_QK:2 * RET_QK])
        for h in range(RET_HEADS):
            qk = slice(h * RET_DK, (h + 1) * RET_DK)
            q_s[:, qk] = _rotary(q_all[:, qk], cos, sin).astype(BF16)
            k_s[:, qk] = _rotary(k_all[:, qk], cos, sin) * (RET_DK ** -0.5)
        v_s[...] = _dot(xb, win_ref[:, 2 * RET_QK:2 * RET_QK + RET_V]).astype(BF16)
        g_s[...] = _dot(xb, win_ref[:, 2 * RET_QK + RET_V:])

    state[...] = r0_ref[...]
    rows = pl.ds(pl.multiple_of(b * seq_len, seq_len), seq_len)
    q_pad[0:seq_len, :] = q_s[rows, :]
    k_pad[0:seq_len, :] = k_s[rows, :]
    v_pad[0:seq_len, :] = v_s[rows, :]
    for h in range(RET_HEADS):
        qk = slice(h * RET_DK, (h + 1) * RET_DK)
        vv = slice(h * RET_DV, (h + 1) * RET_DV)
        on = _retain_head(q_pad[:, qk], k_pad[:, qk], v_pad[:, vv], state, decay, h, lg_ref[h],
                          seq_len)
        gated[rows, vv] = (_silu(g_s[rows, vv]) * on).astype(BF16)
    rout_ref[...] = state[...]

    @pl.when(b == n_seq - 1)
    def _():
        z = DN_ALPHA * x_ref[...] + _dot(gated[...], wout_ref[...])
        o_ref[...] = _layer_norm(z, gain_ref[...], bias_ref[...])


def _ret_mixer(x, win_bf, inv, r0, log_g, wout_bf, gain, bias,
               *, n_seq, seq_len, pos_base, lb, nsub):
    t = n_seq * seq_len
    st_shape = (None, RET_HEADS, RET_DK, RET_DV)
    weights = [_resident((D_MODEL, RET_IN)), _resident((1, ROPE_HALF))]
    tail = [_resident((RET_V, D_MODEL)), _resident((1, D_MODEL)), _resident((1, D_MODEL))]
    out_shape = [jax.ShapeDtypeStruct((t, D_MODEL), F32),
                 jax.ShapeDtypeStruct((n_seq, RET_HEADS, RET_DK, RET_DV), F32)]
    scratch = [pltpu.VMEM((RET_HEADS, RET_DK, RET_DV), F32), pltpu.VMEM((RET_HEADS, lb, lb), F32)]
    smem = pl.BlockSpec(memory_space=pltpu.SMEM)
    if seq_len < lb:
        assert seq_len <= CHUNK and seq_len & (seq_len - 1) == 0 and lb % LANES == 0
        st = pl.BlockSpec(st_shape, lambda b: (b, 0, 0, 0))
        scratch += [pltpu.VMEM((t, RET_QK), BF16), pltpu.VMEM((t, RET_QK), F32),
                    pltpu.VMEM((t, RET_V), BF16), pltpu.VMEM((t, RET_V), F32),
                    pltpu.VMEM((t, RET_V), BF16), pltpu.VMEM((lb, RET_QK), BF16),
                    pltpu.VMEM((lb, RET_QK), F32), pltpu.VMEM((lb, RET_V), BF16)]
        return pl.pallas_call(
            functools.partial(_ret_mixer_short_kernel, lb=lb, seq_len=seq_len, n_seq=n_seq,
                              pos_base=pos_base),
            name="ret_mixer_short",
            grid=(n_seq,),
            in_specs=[smem, _resident((t, D_MODEL))] + weights + [st] + tail,
            out_specs=[pl.BlockSpec((t, D_MODEL), lambda b: (0, 0)), st],
            out_shape=out_shape,
            scratch_shapes=scratch,
            compiler_params=_params("arbitrary"),
        )(log_g, x, win_bf, inv, r0, wout_bf, gain, bias)
    assert seq_len % (nsub * lb) == 0 and lb % CHUNK == 0 and CHUNK & (CHUNK - 1) == 0
    nstep = seq_len // (nsub * lb)
    rows = pl.BlockSpec((nsub * lb, D_MODEL), lambda b, n: (b * nstep + n, 0))
    st = pl.BlockSpec(st_shape, lambda b, n: (b, 0, 0, 0))
    return pl.pallas_call(
        functools.partial(_ret_mixer_kernel, lb=lb, chunk=CHUNK, nsub=nsub, nstep=nstep,
                          pos_base=pos_base),
        name="ret_mixer",
        grid=(n_seq, nstep),
        in_specs=[smem, rows] + weights + [st] + tail,
        out_specs=[rows, st],
        out_shape=out_shape,
        scratch_shapes=scratch + [pltpu.VMEM((nsub * lb, RET_V), BF16)],
        compiler_params=_params("arbitrary", "arbitrary"),
    )(log_g, x, win_bf, inv, r0, wout_bf, gain, bias)


def _ffn_ln_kernel(x_ref, wgu_ref, wd_ref, gain_ref, bias_ref, o_ref, *, tm, mc):
    for c in range(tm // mc):
        rows = slice(c * mc, (c + 1) * mc)
        x = x_ref[rows, :]
        xb = x.astype(BF16)
        gate = _dot(xb, wgu_ref[:, 0:D_FF])
        up = _dot(xb, wgu_ref[:, D_FF:2 * D_FF])
        hid = (_silu(gate) * up).astype(BF16)
        z = DN_ALPHA * x + _dot(hid, wd_ref[...])
        o_ref[rows, :] = _layer_norm(z, gain_ref[...], bias_ref[...])


def _ffn_ln(x, wgu_bf, wd_bf, gain, bias, *, tm, mc):
    t = x.shape[0]
    assert t % tm == 0 and tm % mc == 0
    return pl.pallas_call(
        functools.partial(_ffn_ln_kernel, tm=tm, mc=mc),
        name="ffn_ln",
        grid=(t // tm,),
        in_specs=[pl.BlockSpec((tm, D_MODEL), lambda i: (i, 0)),
                  _resident((D_MODEL, 2 * D_FF)), _resident((D_FF, D_MODEL)),
                  _resident((1, D_MODEL)), _resident((1, D_MODEL))],
        out_specs=pl.BlockSpec((tm, D_MODEL), lambda i: (i, 0)),
        out_shape=jax.ShapeDtypeStruct((t, D_MODEL), F32),
        compiler_params=_params("parallel"),
    )(x, wgu_bf, wd_bf, gain, bias)


def _conv_ln_kernel(x_ref, win_ref, wc_ref, wout_ref, buf_ref, gain_ref, bias_ref, *rest,
                    tm, mc, seg, whole_seqs, tiles_per_seq, n_cast):
    cast_in = rest[:n_cast]
    o_ref, ob_ref, st_ref = rest[n_cast:n_cast + 3]
    cast_out = rest[n_cast + 3:2 * n_cast + 3]
    u_win, prev, mixed = rest[2 * n_cast + 3:]
    for src, dst in zip(cast_in, cast_out):
        dst[...] = src[...].astype(BF16)
    i = pl.program_id(0)
    w0 = wc_ref[0:1, :]
    w1 = wc_ref[1:2, :]
    w2 = wc_ref[2:3, :]
    tail = CONV_WIDTH - 1
    segs = mc // seg
    for c in range(tm // mc):
        rows = slice(c * mc, (c + 1) * mc)
        x = x_ref[rows, :]
        xb = x.astype(BF16)
        gate_b = _dot(xb, win_ref[:, 0:D_MODEL])
        u = _dot(xb, win_ref[:, D_MODEL:2 * D_MODEL]) * _dot(xb, win_ref[:, 2 * D_MODEL:])
        win = u_win.at[c]
        for s in range(segs):
            sg = c * segs + s
            if whole_seqs:
                prev[SUBLANES - tail:SUBLANES, :] = buf_ref[sg]
            elif sg == 0:
                @pl.when(i % tiles_per_seq == 0)
                def _():
                    prev[SUBLANES - tail:SUBLANES, :] = buf_ref[0]
            win[0:SUBLANES, :] = prev[...]
            win[SUBLANES:SUBLANES + seg, :] = u[s * seg:(s + 1) * seg]
            conv = (w0 * win[SUBLANES - 2:SUBLANES - 2 + seg, :]
                    + w1 * win[SUBLANES - 1:SUBLANES - 1 + seg, :]
                    + w2 * win[SUBLANES:SUBLANES + seg, :])
            mixed[c * mc + s * seg:c * mc + (s + 1) * seg, :] = (
                gate_b[s * seg:(s + 1) * seg] * conv).astype(BF16)
            prev[...] = win[seg:seg + SUBLANES, :]
            if whole_seqs:
                st_ref[sg] = win[SUBLANES + seg - tail:SUBLANES + seg, :]
            elif sg == tm // seg - 1:
                st_ref[0] = win[SUBLANES + seg - tail:SUBLANES + seg, :]
        z = DN_ALPHA * x + _dot(mixed[rows, :], wout_ref[...])
        out = _layer_norm(z, gain_ref[...], bias_ref[...])
        o_ref[rows, :] = out
        ob_ref[rows, :] = out.astype(BF16)


def _conv_ln(x, win_bf, wconv, wout_bf, buf, gain, bias, cast=(), *, n_seq, seq_len, tm, mc):
    t = x.shape[0]
    seg = min(seq_len, mc)
    assert t % tm == 0 and tm % mc == 0 and mc % seg == 0 and seg % SUBLANES == 0
    assert seg == seq_len or seq_len % tm == 0
    tiles_per_seq = max(seq_len // tm, 1)
    seq_per_tile = tm // seg if seg == seq_len else 1
    tail = CONV_WIDTH - 1
    steps = t // tm
    st_spec = pl.BlockSpec((seq_per_tile, tail, D_MODEL), lambda i: (i // tiles_per_seq, 0, 0))
    cast_specs = []
    for a in cast:
        n_lead, n_rows, n_cols = a.shape
        slabs = steps // n_lead
        assert steps % n_lead == 0 and n_rows % (slabs * 2 * SUBLANES) == 0
        cast_specs.append(pl.BlockSpec((None, n_rows // slabs, n_cols),
                                       lambda i, slabs=slabs: (i // slabs, i % slabs, 0)))
    outs = pl.pallas_call(
        functools.partial(_conv_ln_kernel, tm=tm, mc=mc, seg=seg, whole_seqs=seg == seq_len,
                          tiles_per_seq=tiles_per_seq, n_cast=len(cast)),
        name="conv_ln",
        grid=(steps,),
        in_specs=[pl.BlockSpec((tm, D_MODEL), lambda i: (i, 0)),
                  _resident((D_MODEL, 3 * D_MODEL)), _resident((CONV_WIDTH, D_MODEL)),
                  _resident((D_MODEL, D_MODEL)), st_spec,
                  _resident((1, D_MODEL)), _resident((1, D_MODEL))] + cast_specs,
        out_specs=[pl.BlockSpec((tm, D_MODEL), lambda i: (i, 0)),
                   pl.BlockSpec((tm, D_MODEL), lambda i: (i, 0)), st_spec] + cast_specs,
        out_shape=[jax.ShapeDtypeStruct((t, D_MODEL), F32),
                   jax.ShapeDtypeStruct((t, D_MODEL), BF16),
                   jax.ShapeDtypeStruct((n_seq, tail, D_MODEL), F32)]
                  + [jax.ShapeDtypeStruct(a.shape, BF16) for a in cast],
        scratch_shapes=[pltpu.VMEM((tm // mc, SUBLANES + seg, D_MODEL), F32),
                        pltpu.VMEM((SUBLANES, D_MODEL), F32),
                        pltpu.VMEM((tm, D_MODEL), BF16)],
        compiler_params=_params("arbitrary"),
    )(x, win_bf, wconv, wout_bf, buf, gain, bias, *cast)
    return outs[0], outs[1], outs[2], tuple(outs[3:])


def _split_bf16(a):
    hi = a.astype(BF16)
    lo = (a - hi.astype(F32)).astype(BF16)
    return hi, lo


def _router_kernel(x_ref, wr_ref, rank_ref, gate_ref, cnt_ref, *, tm):
    x_hi, x_lo = _split_bf16(x_ref[...])
    w_hi, w_lo = _split_bf16(wr_ref[...])
    hi = _dot(x_hi, jnp.concatenate([w_hi, w_lo], axis=1))
    logits = hi[:, 0:LANES] + (_dot(x_lo, w_hi) + hi[:, LANES:])
    lt = logits.T[0:N_EXPERTS, :]
    eidx = lax.broadcasted_iota(jnp.int32, (N_EXPERTS, tm), 0).astype(F32)
    neg = jnp.float32(-jnp.inf)
    m1 = jnp.max(lt, axis=0, keepdims=True)
    i1 = jnp.min(jnp.where(lt == m1, eidx, float(N_EXPERTS)), axis=0, keepdims=True)
    first = eidx == i1
    rest = jnp.where(first, neg, lt)
    m2 = jnp.max(rest, axis=0, keepdims=True)
    i2 = jnp.min(jnp.where(rest == m2, eidx, float(N_EXPERTS)), axis=0, keepdims=True)
    second = eidx == i2
    e2 = jnp.exp(m2 - m1)
    w1 = 1.0 / (1.0 + e2)
    w2 = e2 / (1.0 + e2)
    gate_ref[...] = jnp.where(first, w1, 0.0) + jnp.where(second, w2, 0.0)
    sel = first | second
    self32 = jnp.where(sel, 1.0, 0.0)

    c = CUMSUM_CHUNK if tm % CUMSUM_CHUNK == 0 else tm
    si = lax.broadcasted_iota(jnp.int32, (c, c), 0)
    ti = lax.broadcasted_iota(jnp.int32, (c, c), 1)
    tri = jnp.where(si < ti, 1.0, 0.0).astype(BF16)
    offset = jnp.zeros((N_EXPERTS, 1), F32)
    for j in range(tm // c):
        blk = self32[:, j * c:(j + 1) * c]
        rank = _dot(blk.astype(BF16), tri) + offset
        rank_ref[:, j * c:(j + 1) * c] = jnp.where(sel[:, j * c:(j + 1) * c],
                                                   rank.astype(jnp.int32), -1)
        offset = offset + jnp.sum(blk, axis=1, keepdims=True)
    cnt_ref[...] = jnp.broadcast_to(offset.astype(jnp.int32), (N_EXPERTS, LANES))


def _router(x, wr_pad, *, tm):
    t = x.shape[0]
    assert t % tm == 0
    nt = t // tm
    return pl.pallas_call(
        functools.partial(_router_kernel, tm=tm),
        name="router",
        grid=(nt,),
        in_specs=[pl.BlockSpec((tm, D_MODEL), lambda i: (i, 0)), _resident((D_MODEL, LANES))],
        out_specs=[pl.BlockSpec((N_EXPERTS, tm), lambda i: (0, i)),
                   pl.BlockSpec((N_EXPERTS, tm), lambda i: (0, i)),
                   pl.BlockSpec((None, N_EXPERTS, LANES), lambda i: (i, 0, 0))],
        out_shape=[jax.ShapeDtypeStruct((N_EXPERTS, t), jnp.int32),
                   jax.ShapeDtypeStruct((N_EXPERTS, t), F32),
                   jax.ShapeDtypeStruct((nt, N_EXPERTS, LANES), jnp.int32)],
        compiler_params=_params("parallel"),
    )(x, wr_pad)


def _moe_ln_kernel(cnt_ref, xb_ref, x_hbm, rank_ref, gate_ref, wgu_ref, wd_ref,
                   gain_ref, bias_ref, out_hbm, buf, rsem, wsem, *, tm, nt):
    e = pl.program_id(0)
    i = pl.program_id(1)
    ne = pl.num_programs(0)
    single_tile = nt == 1
    s = e * nt + i
    slot = 0 if single_tile else lax.rem(s, MOE_BUFFERS)
    o_ref = buf.at[slot]

    def tile_rows(hbm, tile):
        start = tile * tm
        if not isinstance(start, int):
            start = pl.multiple_of(start, tm)
        return hbm.at[pl.ds(start, tm), :]

    def read_x(tile, dst):
        return pltpu.make_async_copy(tile_rows(x_hbm, tile), buf.at[dst], rsem.at[dst])

    def read_sum(tile, dst):
        return pltpu.make_async_copy(tile_rows(out_hbm, tile), buf.at[dst], rsem.at[dst])

    def write_sum(tile, src):
        return pltpu.make_async_copy(buf.at[src], tile_rows(out_hbm, tile), wsem.at[src])

    if single_tile:
        @pl.when(e == 0)
        def _():
            first = read_x(0, 0)
            first.start()
            first.wait()
    else:
        next_slot = lax.rem(s + 1, MOE_BUFFERS)
        next_tile = jnp.where(i + 1 < nt, i + 1, 0)

        @pl.when(s == 0)
        def _():
            read_x(0, 0).start()

        @pl.when(s >= 2)
        def _():
            write_sum(jnp.where(i >= 2, i - 2, i - 2 + nt), next_slot).wait()

        @pl.when(s + 1 < nt)
        def _():
            read_x(next_tile, next_slot).start()

        @pl.when((s + 1 >= nt) & (s + 1 < ne * nt))
        def _():
            read_sum(next_tile, next_slot).start()

        @pl.when(e == 0)
        def _():
            read_x(i, slot).wait()

        @pl.when(e > 0)
        def _():
            read_sum(i, slot).wait()

    count = cnt_ref[i * N_EXPERTS + e]
    rank = rank_ref[pl.ds(e, 1), :]
    gate = gate_ref[pl.ds(e, 1), :]

    def expert_rows(row0, nrows):
        rows = row0 + lax.broadcasted_iota(jnp.int32, (nrows, 1), 0)
        hit = rank == rows
        onehot = jnp.where(hit, 1.0, 0.0)
        xg = _dot(onehot.astype(BF16), xb_ref[...]).astype(BF16)
        gcol = jnp.sum(jnp.where(hit, gate, 0.0), axis=1, keepdims=True) * (1.0 / DN_ALPHA)
        hid = (_silu(_dot(xg, wgu_ref[:, 0:D_FF])) * _dot(xg, wgu_ref[:, D_FF:])).astype(BF16)
        y = (_dot(hid, wd_ref[...]) * gcol).astype(BF16)
        o_ref[...] += _dot(onehot.T.astype(BF16), y)

    mean_rows = tm * TOP_K // N_EXPERTS
    single_rows = (mean_rows, mean_rows + MOE_SPARE_ROWS)
    lo = 0
    for nrows in single_rows:
        @pl.when((count > lo) & (count <= nrows))
        def _(nrows=nrows):
            expert_rows(0, nrows)
        lo = nrows

    @pl.when(count > single_rows[-1])
    def _():
        def full_block(r, carry):
            expert_rows(r * MOE_ROWS, MOE_ROWS)
            return carry

        lax.fori_loop(0, lax.shift_right_logical(count + (MOE_ROWS - 1), MOE_ROWS_LOG2),
                      full_block, 0)

    @pl.when(e == ne - 1)
    def _():
        o_ref[...] = _layer_norm(DN_ALPHA * o_ref[...], gain_ref[...], bias_ref[...])

    if single_tile:
        @pl.when(e == ne - 1)
        def _():
            done = write_sum(0, 0)
            done.start()
            done.wait()
    else:
        write_sum(i, slot).start()

        @pl.when(s == ne * nt - 1)
        def _():
            write_sum(i - 1, lax.rem(s + MOE_BUFFERS - 1, MOE_BUFFERS)).wait()
            write_sum(i, slot).wait()


def _moe_vmem_bytes(tm, weight_buffers):
    rows = tm * TOP_K // N_EXPERTS + MOE_SPARE_ROWS
    weights = weight_buffers * 3 * D_MODEL * D_FF * 2
    tiles = tm * D_MODEL * (MOE_BUFFERS * 4 + 2 * 2)
    values = rows * D_FF * (4 + 4 + 2) + rows * tm * (4 + 2) * 2 + tm * D_MODEL * 4
    return weights + tiles + values


def _moe_ln(xb, x, rank, gate, counts, wgu_bf, wd_bf, gain, bias, *, tm):
    t = x.shape[0]
    assert t % tm == 0
    nt = t // tm
    assert nt == 1 or nt >= MOE_BUFFERS
    per_tile = pl.BlockSpec((N_EXPERTS, tm), lambda e, i, c: (0, i))
    const = lambda shape: pl.BlockSpec(shape, lambda e, i, c: (0,) * len(shape),
                                       pipeline_mode=pl.Buffered(1))
    weight_buffers = 2 if _moe_vmem_bytes(tm, 2) <= V7X_VMEM_LIMIT_BYTES else 1
    grid_spec = pltpu.PrefetchScalarGridSpec(
        num_scalar_prefetch=1,
        grid=(N_EXPERTS, nt),
        in_specs=[pl.BlockSpec((tm, D_MODEL), lambda e, i, c: (i, 0)),
                  pl.BlockSpec(memory_space=pl.ANY), per_tile, per_tile,
                  pl.BlockSpec((None, D_MODEL, 2 * D_FF), lambda e, i, c: (e, 0, 0),
                               pipeline_mode=pl.Buffered(weight_buffers)),
                  pl.BlockSpec((None, D_FF, D_MODEL), lambda e, i, c: (e, 0, 0),
                               pipeline_mode=pl.Buffered(weight_buffers)),
                  const((1, D_MODEL)), const((1, D_MODEL))],
        out_specs=pl.BlockSpec(memory_space=pl.ANY),
        scratch_shapes=[pltpu.VMEM((1 if nt == 1 else MOE_BUFFERS, tm, D_MODEL), F32),
                        pltpu.SemaphoreType.DMA((MOE_BUFFERS,)),
                        pltpu.SemaphoreType.DMA((MOE_BUFFERS,))],
    )
    return pl.pallas_call(
        functools.partial(_moe_ln_kernel, tm=tm, nt=nt),
        name="moe_ln",
        grid_spec=grid_spec,
        out_shape=jax.ShapeDtypeStruct((t, D_MODEL), F32),
        compiler_params=_params("arbitrary", "arbitrary"),
    )(counts, xb, x, rank, gate, wgu_bf, wd_bf, gain, bias)


def _trunk(x, n_seq, seq_len, pos_base, r0, conv_buf, w, moe_w,
           *, tm, tm_ffn, mc, lb, nsub, tm_moe):
    x, r_new = _ret_mixer(x, w["ret_in"], w["inv"], r0, w["log_g"], w["ret_out"], w["gain"][0],
                          w["bias"][0], n_seq=n_seq, seq_len=seq_len, pos_base=pos_base, lb=lb,
                          nsub=nsub)
    x = _ffn_ln(x, w["ffn_gu"], w["ffn_down"], w["gain"][1], w["bias"][1], tm=tm_ffn, mc=mc)
    cast = tuple(moe_w) if moe_w[0].dtype == F32 else ()
    x, xb, conv_new, cast_out = _conv_ln(x, w["conv_in"], w["conv_w"], w["conv_out"], conv_buf,
                                         w["gain"][2], w["bias"][2], cast, n_seq=n_seq,
                                         seq_len=seq_len, tm=tm, mc=mc)
    moe_gu, moe_down = cast_out if cast else moe_w
    rank, gate, counts = _router(x, w["router"], tm=tm_moe)
    y = _moe_ln(xb, x, rank, gate, counts[:, :, 0].reshape(-1), moe_gu, moe_down,
                w["gain"][3], w["bias"][3], tm=tm_moe)
    return y, r_new, conv_new, (moe_gu, moe_down)


def kernel(x_prompt, x_sample, state_ret, state_conv, ret_w_in, ret_w_out, conv_w_in, conv_w,
           conv_w_out, ffn_w_gu, ffn_w_down, moe_w_router, moe_w_gu, moe_w_down, ln_gain, ln_bias):
    batch, seq, _ = x_prompt.shape
    dec_batch, dec_seq, _ = x_sample.shape

    w = {
        "ret_in": ret_w_in[0].astype(BF16),
        "ret_out": ret_w_out[0].astype(BF16),
        "conv_in": conv_w_in[0].astype(BF16),
        "conv_w": conv_w[0],
        "conv_out": conv_w_out[0].astype(BF16),
        "ffn_gu": ffn_w_gu[0].astype(BF16),
        "ffn_down": ffn_w_down[0].astype(BF16),
        "router": jnp.pad(moe_w_router[0], ((0, 0), (0, LANES - N_EXPERTS))),
        "gain": ln_gain.reshape(2 * DEPTH, 1, D_MODEL),
        "bias": ln_bias.reshape(2 * DEPTH, 1, D_MODEL),
        "inv": (1.0 / (ROPE_BASE ** jnp.linspace(0.0, 1.0, ROPE_HALF, dtype=F32))).reshape(1, ROPE_HALF),
        "log_g": jnp.log1p(-jnp.exp2(-5.0 - jnp.arange(RET_HEADS, dtype=F32))),
    }

    zero_ret = jnp.zeros((batch, RET_HEADS, RET_DK, RET_DV), F32)
    zero_conv = jnp.zeros((batch, CONV_WIDTH - 1, D_MODEL), F32)
    y_p, ret_p, conv_p, moe_bf = _trunk(
        x_prompt.reshape(batch * seq, D_MODEL), batch, seq, 0, zero_ret, zero_conv, w,
        (moe_w_gu[0], moe_w_down[0]), tm=512, tm_ffn=1024, mc=256, lb=256, nsub=2, tm_moe=1024)
    y_s, ret_s, conv_s, _ = _trunk(
        x_sample.reshape(dec_batch * dec_seq, D_MODEL), dec_batch, dec_seq, PAST_LEN,
        state_ret[0], state_conv[0], w, moe_bf, tm=dec_batch * dec_seq,
        tm_ffn=dec_batch * dec_seq, mc=dec_batch * dec_seq,
        lb=LANES, nsub=1, tm_moe=dec_batch * dec_seq)
    return (y_p.reshape(batch, seq, D_MODEL), y_s.reshape(dec_batch, dec_seq, D_MODEL),
            ret_p[None], ret_s[None], conv_p[None], conv_s[None])
```

```python
import functools
import math

import jax
import jax.numpy as jnp
from jax import lax
from jax.experimental import pallas as pl
from jax.experimental.pallas import tpu as pltpu

D_MODEL = 1024
CHUNK = 64
RET_HEADS = 4
RET_DK = D_MODEL // RET_HEADS
RET_DV = 2 * D_MODEL // RET_HEADS
RET_QK = RET_HEADS * RET_DK
RET_V = RET_HEADS * RET_DV
RET_IN = 2 * RET_QK + 2 * RET_V
ROPE_HALF = RET_DK // 2
CONV_WIDTH = 3
D_FF = 2816
N_EXPERTS = 8
ROPE_BASE = 10000.0
LN_EPS = 1e-5
GN_EPS = 1e-6
DEPTH = 2
DN_ALPHA = (2 * DEPTH) ** 0.25
PAST_LEN = 4096

V7X_VMEM_LIMIT_BYTES = 58 * 1024 * 1024
LANES = 128
SUBLANES = 8
BF16_ROWS = 2 * SUBLANES

MOE_ROWS_LOG2 = 8
MOE_ROWS = 1 << MOE_ROWS_LOG2
TOP_K = 2
MOE_SPARE_ROWS = 32
MOE_BUFFERS = 3
CUMSUM_CHUNK = 512

F32 = jnp.float32
BF16 = jnp.bfloat16


def _params(*semantics):
    return pltpu.CompilerParams(dimension_semantics=semantics,
                                vmem_limit_bytes=V7X_VMEM_LIMIT_BYTES)


def _resident(shape):
    zeros = (0,) * len(shape)
    return pl.BlockSpec(shape, lambda *_: zeros, pipeline_mode=pl.Buffered(1))


def _layer_norm(z, gain, bias):
    mu = jnp.mean(z, axis=-1, keepdims=True)
    zc = z - mu
    var = jnp.mean(zc * zc, axis=-1, keepdims=True)
    return zc * lax.rsqrt(var + LN_EPS) * gain + bias


def _silu(x):
    return x / (1.0 + jnp.exp(-x))


def _dot(a, b):
    return jnp.dot(a, b, preferred_element_type=F32)


def _cast_blocks(cast, steps):
    blocks = []
    for a in cast:
        n_lead, n_rows, n_cols = a.shape
        per_lead = max(p for p in range(1, n_rows // BF16_ROWS + 1)
                       if n_rows % (p * BF16_ROWS) == 0 and steps % (n_lead * p) == 0)
        repeat = steps // (n_lead * per_lead)

        def at(step, per_lead=per_lead, repeat=repeat):
            slab = step // repeat
            return slab // per_lead, slab % per_lead, 0
        blocks.append(((None, n_rows // per_lead, n_cols), at))
    return blocks


def _cast_slabs(cast_in, cast_out):
    for src, dst in zip(cast_in, cast_out):
        dst[...] = src[...].astype(BF16)


def _rotary(t, cos, sin):
    x1 = t[:, 0:ROPE_HALF]
    x2 = t[:, ROPE_HALF:RET_DK]
    return jnp.concatenate([x1 * cos - x2 * sin, x1 * sin + x2 * cos], axis=-1)


def _build_decay(decay, lg_ref, lb, chunk):
    ii = lax.broadcasted_iota(jnp.int32, (lb, lb), 0)
    jj = lax.broadcasted_iota(jnp.int32, (lb, lb), 1)
    shift = int(math.log2(chunk))
    visible = (jj >> shift) <= (ii >> shift)
    dist = jnp.abs(ii - jj).astype(F32)
    for h in range(RET_HEADS):
        decay[h] = jnp.where(visible, jnp.exp(lg_ref[h] * dist), 0.0)


def _retain_head(q_b, k_f, v_b, state, decay, h, lg, l_in):
    jrow = lax.broadcasted_iota(jnp.int32, (q_b.shape[0], 1), 0).astype(F32)
    k_dec = k_f * jnp.exp(lg * (l_in - 1.0 - jrow))
    r_old = state[h]
    s = lax.dot_general(q_b, k_f.astype(BF16), (((1,), (1,)), ((), ())),
                        preferred_element_type=F32)
    p = (s * decay[h]).astype(BF16)
    cross = jnp.exp(lg * (jrow + 1.0))
    o = (_dot(p, v_b) + _dot(q_b, r_old.astype(BF16)) * cross)[0:l_in]
    block_decay = jnp.exp(jnp.full((1, RET_DV), lg * l_in, F32))
    state[h] = r_old * block_decay + _dot(k_dec.T.astype(BF16), v_b)
    mu = jnp.mean(o, axis=-1, keepdims=True)
    oc = o - mu
    var = jnp.mean(oc * oc, axis=-1, keepdims=True)
    return oc * lax.rsqrt(var + GN_EPS)


def _ret_mixer_kernel(lg_ref, x_ref, win_ref, inv_ref, r0_ref, wout_ref, gain_ref, bias_ref, *rest,
                      lb, chunk, nsub, nstep, pos_base, n_cast):
    cast_in = rest[:n_cast]
    o_ref, rout_ref = rest[n_cast:n_cast + 2]
    cast_out = rest[n_cast + 2:2 * n_cast + 2]
    state, decay, gated = rest[2 * n_cast + 2:]
    _cast_slabs(cast_in, cast_out)
    b = pl.program_id(0)
    n = pl.program_id(1)

    @pl.when((b == 0) & (n == 0))
    def _():
        _build_decay(decay, lg_ref, lb, chunk)

    @pl.when(n == 0)
    def _():
        state[...] = r0_ref[...]

    for sb in range(nsub):
        rows = slice(sb * lb, (sb + 1) * lb)
        x = x_ref[rows, :]
        xb = x.astype(BF16)
        row = (n * nsub + sb) * lb + lax.broadcasted_iota(jnp.int32, (lb, 1), 0)
        ang = (pos_base + row).astype(F32) * inv_ref[...]
        cos = jnp.cos(ang)
        sin = jnp.sin(ang)
        q_all = _dot(xb, win_ref[:, 0:RET_QK])
        k_all = _dot(xb, win_ref[:, RET_QK:2 * RET_QK])
        v_all = _dot(xb, win_ref[:, 2 * RET_QK:2 * RET_QK + RET_V]).astype(BF16)
        g_all = _dot(xb, win_ref[:, 2 * RET_QK + RET_V:])
        for h in range(RET_HEADS):
            qk = slice(h * RET_DK, (h + 1) * RET_DK)
            vv = slice(h * RET_DV, (h + 1) * RET_DV)
            q_h = _rotary(q_all[:, qk], cos, sin).astype(BF16)
            k_h = _rotary(k_all[:, qk], cos, sin) * (RET_DK ** -0.5)
            on = _retain_head(q_h, k_h, v_all[:, vv], state, decay, h, lg_ref[h], lb)
            gated[rows, vv] = (_silu(g_all[:, vv]) * on).astype(BF16)
        z = DN_ALPHA * x + _dot(gated[rows, :], wout_ref[...])
        o_ref[rows, :] = _layer_norm(z, gain_ref[...], bias_ref[...])

    @pl.when(n == nstep - 1)
    def _():
        rout_ref[...] = state[...]


def _ret_mixer_short_kernel(lg_ref, x_ref, win_ref, inv_ref, r0_ref, wout_ref, gain_ref, bias_ref,
                            o_ref, rout_ref, state, decay, q_s, k_s, v_s, g_s, gated,
                            q_pad, k_pad, v_pad, *, lb, seq_len, n_seq, pos_base):
    b = pl.program_id(0)

    @pl.when(b == 0)
    def _():
        _build_decay(decay, lg_ref, lb, seq_len)
        for pad in (q_pad, k_pad, v_pad):
            pad[...] = jnp.zeros_like(pad)
        xb = x_ref[...].astype(BF16)
        row = lax.broadcasted_iota(jnp.int32, (xb.shape[0], 1), 0)
        ang = (pos_base + (row & (seq_len - 1))).astype(F32) * inv_ref[...]
        cos = jnp.cos(ang)
        sin = jnp.sin(ang)
        q_all = _dot(xb, win_ref[:, 0:RET_QK])
        k_all = _dot(xb, win_ref[:, RET_QK:2 * RET_QK])
        for h in range(RET_HEADS):
            qk = slice(h * RET_DK, (h + 1) * RET_DK)
            q_s[:, qk] = _rotary(q_all[:, qk], cos, sin).astype(BF16)
            k_s[:, qk] = _rotary(k_all[:, qk], cos, sin) * (RET_DK ** -0.5)
        v_s[...] = _dot(xb, win_ref[:, 2 * RET_QK:2 * RET_QK + RET_V]).astype(BF16)
        g_s[...] = _dot(xb, win_ref[:, 2 * RET_QK + RET_V:])

    state[...] = r0_ref[...]
    rows = pl.ds(pl.multiple_of(b * seq_len, seq_len), seq_len)
    q_pad[0:seq_len, :] = q_s[rows, :]
    k_pad[0:seq_len, :] = k_s[rows, :]
    v_pad[0:seq_len, :] = v_s[rows, :]
    for h in range(RET_HEADS):
        qk = slice(h * RET_DK, (h + 1) * RET_DK)
        vv = slice(h * RET_DV, (h + 1) * RET_DV)
        on = _retain_head(q_pad[:, qk], k_pad[:, qk], v_pad[:, vv], state, decay, h, lg_ref[h],
                          seq_len)
        gated[rows, vv] = (_silu(g_s[rows, vv]) * on).astype(BF16)
    rout_ref[...] = state[...]

    @pl.when(b == n_seq - 1)
    def _():
        z = DN_ALPHA * x_ref[...] + _dot(gated[...], wout_ref[...])
        o_ref[...] = _layer_norm(z, gain_ref[...], bias_ref[...])


def _ret_mixer(x, win_bf, inv, r0, log_g, wout_bf, gain, bias, cast=(),
               *, n_seq, seq_len, pos_base, lb, nsub):
    t = n_seq * seq_len
    st_shape = (None, RET_HEADS, RET_DK, RET_DV)
    weights = [_resident((D_MODEL, RET_IN)), _resident((1, ROPE_HALF))]
    tail = [_resident((RET_V, D_MODEL)), _resident((1, D_MODEL)), _resident((1, D_MODEL))]
    out_shape = [jax.ShapeDtypeStruct((t, D_MODEL), F32),
                 jax.ShapeDtypeStruct((n_seq, RET_HEADS, RET_DK, RET_DV), F32)]
    scratch = [pltpu.VMEM((RET_HEADS, RET_DK, RET_DV), F32), pltpu.VMEM((RET_HEADS, lb, lb), F32)]
    smem = pl.BlockSpec(memory_space=pltpu.SMEM)
    if seq_len < lb:
        assert seq_len <= CHUNK and seq_len & (seq_len - 1) == 0 and lb % LANES == 0
        assert not cast
        st = pl.BlockSpec(st_shape, lambda b: (b, 0, 0, 0))
        scratch += [pltpu.VMEM((t, RET_QK), BF16), pltpu.VMEM((t, RET_QK), F32),
                    pltpu.VMEM((t, RET_V), BF16), pltpu.VMEM((t, RET_V), F32),
                    pltpu.VMEM((t, RET_V), BF16), pltpu.VMEM((lb, RET_QK), BF16),
                    pltpu.VMEM((lb, RET_QK), F32), pltpu.VMEM((lb, RET_V), BF16)]
        y, r_new = pl.pallas_call(
            functools.partial(_ret_mixer_short_kernel, lb=lb, seq_len=seq_len, n_seq=n_seq,
                              pos_base=pos_base),
            name="ret_mixer_short",
            grid=(n_seq,),
            in_specs=[smem, _resident((t, D_MODEL))] + weights + [st] + tail,
            out_specs=[pl.BlockSpec((t, D_MODEL), lambda b: (0, 0)), st],
            out_shape=out_shape,
            scratch_shapes=scratch,
            compiler_params=_params("arbitrary"),
        )(log_g, x, win_bf, inv, r0, wout_bf, gain, bias)
        return y, r_new, ()
    assert seq_len % (nsub * lb) == 0 and lb % CHUNK == 0 and CHUNK & (CHUNK - 1) == 0
    nstep = seq_len // (nsub * lb)
    rows = pl.BlockSpec((nsub * lb, D_MODEL), lambda b, n: (b * nstep + n, 0))
    st = pl.BlockSpec(st_shape, lambda b, n: (b, 0, 0, 0))
    cast_specs = [pl.BlockSpec(shape, lambda b, n, at=at: at(b * nstep + n)) for shape, at in
                  _cast_blocks(cast, n_seq * nstep)]
    outs = pl.pallas_call(
        functools.partial(_ret_mixer_kernel, lb=lb, chunk=CHUNK, nsub=nsub, nstep=nstep,
                          pos_base=pos_base, n_cast=len(cast)),
        name="ret_mixer",
        grid=(n_seq, nstep),
        in_specs=[smem, rows] + weights + [st] + tail + cast_specs,
        out_specs=[rows, st] + cast_specs,
        out_shape=out_shape + [jax.ShapeDtypeStruct(a.shape, BF16) for a in cast],
        scratch_shapes=scratch + [pltpu.VMEM((nsub * lb, RET_V), BF16)],
        compiler_params=_params("arbitrary", "arbitrary"),
    )(log_g, x, win_bf, inv, r0, wout_bf, gain, bias, *cast)
    return outs[0], outs[1], tuple(outs[2:])


def _ffn_ln_kernel(x_ref, wgu_ref, wd_ref, gain_ref, bias_ref, o_ref, *, tm, mc):
    for c in range(tm // mc):
        rows = slice(c * mc, (c + 1) * mc)
        x = x_ref[rows, :]
        xb = x.astype(BF16)
        gate = _dot(xb, wgu_ref[:, 0:D_FF])
        up = _dot(xb, wgu_ref[:, D_FF:2 * D_FF])
        hid = (_silu(gate) * up).astype(BF16)
        z = DN_ALPHA * x + _dot(hid, wd_ref[...])
        o_ref[rows, :] = _layer_norm(z, gain_ref[...], bias_ref[...])


def _ffn_ln(x, wgu_bf, wd_bf, gain, bias, *, tm, mc):
    t = x.shape[0]
    assert t % tm == 0 and tm % mc == 0
    return pl.pallas_call(
        functools.partial(_ffn_ln_kernel, tm=tm, mc=mc),
        name="ffn_ln",
        grid=(t // tm,),
        in_specs=[pl.BlockSpec((tm, D_MODEL), lambda i: (i, 0)),
                  _resident((D_MODEL, 2 * D_FF)), _resident((D_FF, D_MODEL)),
                  _resident((1, D_MODEL)), _resident((1, D_MODEL))],
        out_specs=pl.BlockSpec((tm, D_MODEL), lambda i: (i, 0)),
        out_shape=jax.ShapeDtypeStruct((t, D_MODEL), F32),
        compiler_params=_params("parallel"),
    )(x, wgu_bf, wd_bf, gain, bias)


def _conv_ln_kernel(x_ref, win_ref, wc_ref, wout_ref, buf_ref, gain_ref, bias_ref, *rest,
                    tm, mc, seg, whole_seqs, tiles_per_seq, n_cast):
    cast_in = rest[:n_cast]
    o_ref, ob_ref, st_ref = rest[n_cast:n_cast + 3]
    cast_out = rest[n_cast + 3:2 * n_cast + 3]
    u_win, prev, mixed = rest[2 * n_cast + 3:]
    _cast_slabs(cast_in, cast_out)
    i = pl.program_id(0)
    w0 = wc_ref[0:1, :]
    w1 = wc_ref[1:2, :]
    w2 = wc_ref[2:3, :]
    tail = CONV_WIDTH - 1
    segs = mc // seg
    for c in range(tm // mc):
        rows = slice(c * mc, (c + 1) * mc)
        x = x_ref[rows, :]
        xb = x.astype(BF16)
        gate_b = _dot(xb, win_ref[:, 0:D_MODEL])
        u = _dot(xb, win_ref[:, D_MODEL:2 * D_MODEL]) * _dot(xb, win_ref[:, 2 * D_MODEL:])
        win = u_win.at[c]
        for s in range(segs):
            sg = c * segs + s
            if whole_seqs:
                prev[SUBLANES - tail:SUBLANES, :] = buf_ref[sg]
            elif sg == 0:
                @pl.when(i % tiles_per_seq == 0)
                def _():
                    prev[SUBLANES - tail:SUBLANES, :] = buf_ref[0]
            win[0:SUBLANES, :] = prev[...]
            win[SUBLANES:SUBLANES + seg, :] = u[s * seg:(s + 1) * seg]
            conv = (w0 * win[SUBLANES - 2:SUBLANES - 2 + seg, :]
                    + w1 * win[SUBLANES - 1:SUBLANES - 1 + seg, :]
                    + w2 * win[SUBLANES:SUBLANES + seg, :])
            mixed[c * mc + s * seg:c * mc + (s + 1) * seg, :] = (
                gate_b[s * seg:(s + 1) * seg] * conv).astype(BF16)
            prev[...] = win[seg:seg + SUBLANES, :]
            if whole_seqs:
                st_ref[sg] = win[SUBLANES + seg - tail:SUBLANES + seg, :]
            elif sg == tm // seg - 1:
                st_ref[0] = win[SUBLANES + seg - tail:SUBLANES + seg, :]
        z = DN_ALPHA * x + _dot(mixed[rows, :], wout_ref[...])
        out = _layer_norm(z, gain_ref[...], bias_ref[...])
        o_ref[rows, :] = out
        ob_ref[rows, :] = out.astype(BF16)


def _conv_ln(x, win_bf, wconv, wout_bf, buf, gain, bias, cast=(), *, n_seq, seq_len, tm, mc):
    t = x.shape[0]
    seg = min(seq_len, mc)
    assert t % tm == 0 and tm % mc == 0 and mc % seg == 0 and seg % SUBLANES == 0
    assert seg == seq_len or seq_len % tm == 0
    tiles_per_seq = max(seq_len // tm, 1)
    seq_per_tile = tm // seg if seg == seq_len else 1
    tail = CONV_WIDTH - 1
    steps = t // tm
    st_spec = pl.BlockSpec((seq_per_tile, tail, D_MODEL), lambda i: (i // tiles_per_seq, 0, 0))
    cast_specs = [pl.BlockSpec(shape, lambda i, at=at: at(i)) for shape, at in
                  _cast_blocks(cast, steps)]
    outs = pl.pallas_call(
        functools.partial(_conv_ln_kernel, tm=tm, mc=mc, seg=seg, whole_seqs=seg == seq_len,
                          tiles_per_seq=tiles_per_seq, n_cast=len(cast)),
        name="conv_ln",
        grid=(steps,),
        in_specs=[pl.BlockSpec((tm, D_MODEL), lambda i: (i, 0)),
                  _resident((D_MODEL, 3 * D_MODEL)), _resident((CONV_WIDTH, D_MODEL)),
                  _resident((D_MODEL, D_MODEL)), st_spec,
                  _resident((1, D_MODEL)), _resident((1, D_MODEL))] + cast_specs,
        out_specs=[pl.BlockSpec((tm, D_MODEL), lambda i: (i, 0)),
                   pl.BlockSpec((tm, D_MODEL), lambda i: (i, 0)), st_spec] + cast_specs,
        out_shape=[jax.ShapeDtypeStruct((t, D_MODEL), F32),
                   jax.ShapeDtypeStruct((t, D_MODEL), BF16),
                   jax.ShapeDtypeStruct((n_seq, tail, D_MODEL), F32)]
                  + [jax.ShapeDtypeStruct(a.shape, BF16) for a in cast],
        scratch_shapes=[pltpu.VMEM((tm // mc, SUBLANES + seg, D_MODEL), F32),
                        pltpu.VMEM((SUBLANES, D_MODEL), F32),
                        pltpu.VMEM((tm, D_MODEL), BF16)],
        compiler_params=_params("arbitrary"),
    )(x, win_bf, wconv, wout_bf, buf, gain, bias, *cast)
    return outs[0], outs[1], outs[2], tuple(outs[3:])


def _split_bf16(a):
    hi = a.astype(BF16)
    lo = (a - hi.astype(F32)).astype(BF16)
    return hi, lo


def _router_kernel(x_ref, wr_ref, rank_ref, gate_ref, cnt_ref, *, tm):
    x_hi, x_lo = _split_bf16(x_ref[...])
    w_hi, w_lo = _split_bf16(wr_ref[...])
    hi = _dot(x_hi, jnp.concatenate([w_hi, w_lo], axis=1))
    logits = hi[:, 0:LANES] + (_dot(x_lo, w_hi) + hi[:, LANES:])
    lt = logits.T[0:N_EXPERTS, :]
    eidx = lax.broadcasted_iota(jnp.int32, (N_EXPERTS, tm), 0).astype(F32)
    neg = jnp.float32(-jnp.inf)
    m1 = jnp.max(lt, axis=0, keepdims=True)
    i1 = jnp.min(jnp.where(lt == m1, eidx, float(N_EXPERTS)), axis=0, keepdims=True)
    first = eidx == i1
    rest = jnp.where(first, neg, lt)
    m2 = jnp.max(rest, axis=0, keepdims=True)
    i2 = jnp.min(jnp.where(rest == m2, eidx, float(N_EXPERTS)), axis=0, keepdims=True)
    second = eidx == i2
    e2 = jnp.exp(m2 - m1)
    w1 = 1.0 / (1.0 + e2)
    w2 = e2 / (1.0 + e2)
    gate_ref[...] = jnp.where(first, w1, 0.0) + jnp.where(second, w2, 0.0)
    sel = first | second
    self32 = jnp.where(sel, 1.0, 0.0)

    c = CUMSUM_CHUNK if tm % CUMSUM_CHUNK == 0 else tm
    si = lax.broadcasted_iota(jnp.int32, (c, c), 0)
    ti = lax.broadcasted_iota(jnp.int32, (c, c), 1)
    tri = jnp.where(si < ti, 1.0, 0.0).astype(BF16)
    offset = jnp.zeros((N_EXPERTS, 1), F32)
    for j in range(tm // c):
        blk = self32[:, j * c:(j + 1) * c]
        rank = _dot(blk.astype(BF16), tri) + offset
        rank_ref[:, j * c:(j + 1) * c] = jnp.where(sel[:, j * c:(j + 1) * c],
                                                   rank.astype(jnp.int32), -1)
        offset = offset + jnp.sum(blk, axis=1, keepdims=True)
    cnt_ref[...] = jnp.broadcast_to(offset.astype(jnp.int32), (N_EXPERTS, LANES))


def _router(x, wr_pad, *, tm):
    t = x.shape[0]
    assert t % tm == 0
    nt = t // tm
    return pl.pallas_call(
        functools.partial(_router_kernel, tm=tm),
        name="router",
        grid=(nt,),
        in_specs=[pl.BlockSpec((tm, D_MODEL), lambda i: (i, 0)), _resident((D_MODEL, LANES))],
        out_specs=[pl.BlockSpec((N_EXPERTS, tm), lambda i: (0, i)),
                   pl.BlockSpec((N_EXPERTS, tm), lambda i: (0, i)),
                   pl.BlockSpec((None, N_EXPERTS, LANES), lambda i: (i, 0, 0))],
        out_shape=[jax.ShapeDtypeStruct((N_EXPERTS, t), jnp.int32),
                   jax.ShapeDtypeStruct((N_EXPERTS, t), F32),
                   jax.ShapeDtypeStruct((nt, N_EXPERTS, LANES), jnp.int32)],
        compiler_params=_params("parallel"),
    )(x, wr_pad)


def _moe_ln_kernel(cnt_ref, xb_ref, x_hbm, rank_ref, gate_ref, wgu_ref, wd_ref,
                   gain_ref, bias_ref, out_hbm, buf, rsem, wsem, *, tm, nt):
    e = pl.program_id(0)
    i = pl.program_id(1)
    ne = pl.num_programs(0)
    single_tile = nt == 1
    s = e * nt + i
    slot = 0 if single_tile else lax.rem(s, MOE_BUFFERS)
    o_ref = buf.at[slot]

    def tile_rows(hbm, tile):
        start = tile * tm
        if not isinstance(start, int):
            start = pl.multiple_of(start, tm)
        return hbm.at[pl.ds(start, tm), :]

    def read_x(tile, dst):
        return pltpu.make_async_copy(tile_rows(x_hbm, tile), buf.at[dst], rsem.at[dst])

    def read_sum(tile, dst):
        return pltpu.make_async_copy(tile_rows(out_hbm, tile), buf.at[dst], rsem.at[dst])

    def write_sum(tile, src):
        return pltpu.make_async_copy(buf.at[src], tile_rows(out_hbm, tile), wsem.at[src])

    if single_tile:
        @pl.when(e == 0)
        def _():
            first = read_x(0, 0)
            first.start()
            first.wait()
    else:
        next_slot = lax.rem(s + 1, MOE_BUFFERS)
        next_tile = jnp.where(i + 1 < nt, i + 1, 0)

        @pl.when(s == 0)
        def _():
            read_x(0, 0).start()

        @pl.when(s >= 2)
        def _():
            write_sum(jnp.where(i >= 2, i - 2, i - 2 + nt), next_slot).wait()

        @pl.when(s + 1 < nt)
        def _():
            read_x(next_tile, next_slot).start()

        @pl.when((s + 1 >= nt) & (s + 1 < ne * nt))
        def _():
            read_sum(next_tile, next_slot).start()

        @pl.when(e == 0)
        def _():
            read_x(i, slot).wait()

        @pl.when(e > 0)
        def _():
            read_sum(i, slot).wait()

    count = cnt_ref[i * N_EXPERTS + e]
    rank = rank_ref[pl.ds(e, 1), :]
    gate = gate_ref[pl.ds(e, 1), :]

    def expert_rows(row0, nrows):
        rows = row0 + lax.broadcasted_iota(jnp.int32, (nrows, 1), 0)
        hit = rank == rows
        onehot = jnp.where(hit, 1.0, 0.0)
        xg = _dot(onehot.astype(BF16), xb_ref[...]).astype(BF16)
        gcol = jnp.sum(jnp.where(hit, gate, 0.0), axis=1, keepdims=True) * (1.0 / DN_ALPHA)
        hid = (_silu(_dot(xg, wgu_ref[:, 0:D_FF])) * _dot(xg, wgu_ref[:, D_FF:])).astype(BF16)
        y = (_dot(hid, wd_ref[...]) * gcol).astype(BF16)
        o_ref[...] += _dot(onehot.T.astype(BF16), y)

    mean_rows = tm * TOP_K // N_EXPERTS
    single_rows = (mean_rows, mean_rows + MOE_SPARE_ROWS)
    lo = 0
    for nrows in single_rows:
        @pl.when((count > lo) & (count <= nrows))
        def _(nrows=nrows):
            expert_rows(0, nrows)
        lo = nrows

    @pl.when(count > single_rows[-1])
    def _():
        def full_block(r, carry):
            expert_rows(r * MOE_ROWS, MOE_ROWS)
            return carry

        lax.fori_loop(0, lax.shift_right_logical(count + (MOE_ROWS - 1), MOE_ROWS_LOG2),
                      full_block, 0)

    @pl.when(e == ne - 1)
    def _():
        o_ref[...] = _layer_norm(DN_ALPHA * o_ref[...], gain_ref[...], bias_ref[...])

    if single_tile:
        @pl.when(e == ne - 1)
        def _():
            done = write_sum(0, 0)
            done.start()
            done.wait()
    else:
        write_sum(i, slot).start()

        @pl.when(s == ne * nt - 1)
        def _():
            write_sum(i - 1, lax.rem(s + MOE_BUFFERS - 1, MOE_BUFFERS)).wait()
            write_sum(i, slot).wait()


def _moe_vmem_bytes(tm, weight_buffers):
    rows = tm * TOP_K // N_EXPERTS + MOE_SPARE_ROWS
    weights = weight_buffers * 3 * D_MODEL * D_FF * 2
    tiles = tm * D_MODEL * (MOE_BUFFERS * 4 + 2 * 2)
    values = rows * D_FF * (4 + 4 + 2) + rows * tm * (4 + 2) * 2 + tm * D_MODEL * 4
    return weights + tiles + values


def _moe_ln(xb, x, rank, gate, counts, wgu_bf, wd_bf, gain, bias, *, tm):
    t = x.shape[0]
    assert t % tm == 0
    nt = t // tm
    assert nt == 1 or nt >= MOE_BUFFERS
    per_tile = pl.BlockSpec((N_EXPERTS, tm), lambda e, i, c: (0, i))
    const = lambda shape: pl.BlockSpec(shape, lambda e, i, c: (0,) * len(shape),
                                       pipeline_mode=pl.Buffered(1))
    weight_buffers = 2 if _moe_vmem_bytes(tm, 2) <= V7X_VMEM_LIMIT_BYTES else 1
    down_bytes = D_FF * D_MODEL * 2
    down_buffers = 2 if _moe_vmem_bytes(tm, 1) + down_bytes <= V7X_VMEM_LIMIT_BYTES else 1
    grid_spec = pltpu.PrefetchScalarGridSpec(
        num_scalar_prefetch=1,
        grid=(N_EXPERTS, nt),
        in_specs=[pl.BlockSpec((tm, D_MODEL), lambda e, i, c: (i, 0)),
                  pl.BlockSpec(memory_space=pl.ANY), per_tile, per_tile,
                  pl.BlockSpec((None, D_MODEL, 2 * D_FF), lambda e, i, c: (e, 0, 0),
                               pipeline_mode=pl.Buffered(weight_buffers)),
                  pl.BlockSpec((None, D_FF, D_MODEL), lambda e, i, c: (e, 0, 0),
                               pipeline_mode=pl.Buffered(down_buffers)),
                  const((1, D_MODEL)), const((1, D_MODEL))],
        out_specs=pl.BlockSpec(memory_space=pl.ANY),
        scratch_shapes=[pltpu.VMEM((1 if nt == 1 else MOE_BUFFERS, tm, D_MODEL), F32),
                        pltpu.SemaphoreType.DMA((MOE_BUFFERS,)),
                        pltpu.SemaphoreType.DMA((MOE_BUFFERS,))],
    )
    return pl.pallas_call(
        functools.partial(_moe_ln_kernel, tm=tm, nt=nt),
        name="moe_ln",
        grid_spec=grid_spec,
        out_shape=jax.ShapeDtypeStruct((t, D_MODEL), F32),
        compiler_params=_params("arbitrary", "arbitrary"),
    )(counts, xb, x, rank, gate, wgu_bf, wd_bf, gain, bias)


def _trunk(x, n_seq, seq_len, pos_base, r0, conv_buf, w, late_w, moe_w,
           *, tm, tm_ffn, mc, lb, nsub, tm_moe):
    cast = tuple(late_w) if late_w[0].dtype == F32 else ()
    x, r_new, cast_out = _ret_mixer(x, w["ret_in"], w["inv"], r0, w["log_g"], w["ret_out"],
                                    w["gain"][0], w["bias"][0], cast, n_seq=n_seq,
                                    seq_len=seq_len, pos_base=pos_base, lb=lb, nsub=nsub)
    late_w = cast_out if cast else late_w
    ffn_gu, ffn_down, conv_in, conv_out = (a[0] for a in late_w)
    x = _ffn_ln(x, ffn_gu, ffn_down, w["gain"][1], w["bias"][1], tm=tm_ffn, mc=mc)
    cast = tuple(moe_w) if moe_w[0].dtype == F32 else ()
    x, xb, conv_new, cast_out = _conv_ln(x, conv_in, w["conv_w"], conv_out, conv_buf,
                                         w["gain"][2], w["bias"][2], cast, n_seq=n_seq,
                                         seq_len=seq_len, tm=tm, mc=mc)
    moe_w = cast_out if cast else moe_w
    rank, gate, counts = _router(x, w["router"], tm=tm_moe)
    y = _moe_ln(xb, x, rank, gate, counts[:, :, 0].reshape(-1), moe_w[0], moe_w[1],
                w["gain"][3], w["bias"][3], tm=tm_moe)
    return y, r_new, conv_new, late_w, moe_w


def kernel(x_prompt, x_sample, state_ret, state_conv, ret_w_in, ret_w_out, conv_w_in, conv_w,
           conv_w_out, ffn_w_gu, ffn_w_down, moe_w_router, moe_w_gu, moe_w_down, ln_gain, ln_bias):
    batch, seq, _ = x_prompt.shape
    dec_batch, dec_seq, _ = x_sample.shape

    w = {
        "ret_in": ret_w_in[0].astype(BF16),
        "ret_out": ret_w_out[0].astype(BF16),
        "conv_w": conv_w[0],
        "router": jnp.pad(moe_w_router[0], ((0, 0), (0, LANES - N_EXPERTS))),
        "gain": ln_gain.reshape(2 * DEPTH, 1, D_MODEL),
        "bias": ln_bias.reshape(2 * DEPTH, 1, D_MODEL),
        "inv": (1.0 / (ROPE_BASE ** jnp.linspace(0.0, 1.0, ROPE_HALF, dtype=F32))).reshape(1, ROPE_HALF),
        "log_g": jnp.log1p(-jnp.exp2(-5.0 - jnp.arange(RET_HEADS, dtype=F32))),
    }

    zero_ret = jnp.zeros((batch, RET_HEADS, RET_DK, RET_DV), F32)
    zero_conv = jnp.zeros((batch, CONV_WIDTH - 1, D_MODEL), F32)
    y_p, ret_p, conv_p, late_bf, moe_bf = _trunk(
        x_prompt.reshape(batch * seq, D_MODEL), batch, seq, 0, zero_ret, zero_conv, w,
        (ffn_w_gu, ffn_w_down, conv_w_in, conv_w_out), (moe_w_gu[0], moe_w_down[0]),
        tm=512, tm_ffn=1024, mc=256, lb=256, nsub=2, tm_moe=1024)
    y_s, ret_s, conv_s, _, _ = _trunk(
        x_sample.reshape(dec_batch * dec_seq, D_MODEL), dec_batch, dec_seq, PAST_LEN,
        state_ret[0], state_conv[0], w, late_bf, moe_bf, tm=dec_batch * dec_seq,
        tm_ffn=dec_batch * dec_seq, mc=dec_batch * dec_seq,
        lb=LANES, nsub=1, tm_moe=dec_batch * dec_seq)
    return (y_p.reshape(batch, seq, D_MODEL), y_s.reshape(dec_batch, dec_seq, D_MODEL),
            ret_p[None], ret_s[None], conv_p[None], conv_s[None])
```

```python
import functools
import math

import jax
import jax.numpy as jnp
from jax import lax
from jax.experimental import pallas as pl
from jax.experimental.pallas import tpu as pltpu

D_MODEL = 1024
CHUNK = 64
RET_HEADS = 4
RET_DK = D_MODEL // RET_HEADS
RET_DV = 2 * D_MODEL // RET_HEADS
RET_QK = RET_HEADS * RET_DK
RET_V = RET_HEADS * RET_DV
RET_IN = 2 * RET_QK + 2 * RET_V
ROPE_HALF = RET_DK // 2
CONV_WIDTH = 3
D_FF = 2816
N_EXPERTS = 8
ROPE_BASE = 10000.0
LN_EPS = 1e-5
GN_EPS = 1e-6
DEPTH = 2
DN_ALPHA = (2 * DEPTH) ** 0.25
PAST_LEN = 4096

V7X_VMEM_LIMIT_BYTES = 58 * 1024 * 1024
LANES = 128
SUBLANES = 8
BF16_ROWS = 2 * SUBLANES

MOE_ROWS_LOG2 = 8
MOE_ROWS = 1 << MOE_ROWS_LOG2
TOP_K = 2
MOE_SPARE_ROWS = 32
MOE_BUFFERS = 3
CUMSUM_CHUNK = 512

F32 = jnp.float32
BF16 = jnp.bfloat16


def _params(*semantics):
    return pltpu.CompilerParams(dimension_semantics=semantics,
                                vmem_limit_bytes=V7X_VMEM_LIMIT_BYTES)


def _resident(shape):
    zeros = (0,) * len(shape)
    return pl.BlockSpec(shape, lambda *_: zeros, pipeline_mode=pl.Buffered(1))


def _layer_norm(z, gain, bias):
    mu = jnp.mean(z, axis=-1, keepdims=True)
    zc = z - mu
    var = jnp.mean(zc * zc, axis=-1, keepdims=True)
    return zc * lax.rsqrt(var + LN_EPS) * gain + bias


def _silu(x):
    return x / (1.0 + jnp.exp(-x))


def _dot(a, b):
    return jnp.dot(a, b, preferred_element_type=F32)


def _cast_blocks(cast, steps):
    blocks = []
    for a in cast:
        n_lead, n_rows, n_cols = a.shape
        per_lead = max(p for p in range(1, n_rows // BF16_ROWS + 1)
                       if n_rows % (p * BF16_ROWS) == 0 and steps % (n_lead * p) == 0)
        repeat = steps // (n_lead * per_lead)

        def at(step, per_lead=per_lead, repeat=repeat):
            slab = step // repeat
            return slab // per_lead, slab % per_lead, 0
        blocks.append(((None, n_rows // per_lead, n_cols), at))
    return blocks


def _cast_slabs(cast_in, cast_out):
    for src, dst in zip(cast_in, cast_out):
        dst[...] = src[...].astype(BF16)


def _rotary(t, cos, sin):
    x1 = t[:, 0:ROPE_HALF]
    x2 = t[:, ROPE_HALF:RET_DK]
    return jnp.concatenate([x1 * cos - x2 * sin, x1 * sin + x2 * cos], axis=-1)


def _build_decay(decay, lg_ref, lb, chunk):
    ii = lax.broadcasted_iota(jnp.int32, (lb, lb), 0)
    jj = lax.broadcasted_iota(jnp.int32, (lb, lb), 1)
    shift = int(math.log2(chunk))
    visible = (jj >> shift) <= (ii >> shift)
    dist = jnp.abs(ii - jj).astype(F32)
    for h in range(RET_HEADS):
        decay[h] = jnp.where(visible, jnp.exp(lg_ref[h] * dist), 0.0)


def _retain_head(q_b, k_f, v_b, state, decay, h, lg, l_in):
    jrow = lax.broadcasted_iota(jnp.int32, (q_b.shape[0], 1), 0).astype(F32)
    k_dec = k_f * jnp.exp(lg * (l_in - 1.0 - jrow))
    r_old = state[h]
    s = lax.dot_general(q_b, k_f.astype(BF16), (((1,), (1,)), ((), ())),
                        preferred_element_type=F32)
    p = (s * decay[h]).astype(BF16)
    cross = jnp.exp(lg * (jrow + 1.0))
    o = (_dot(p, v_b) + _dot(q_b, r_old.astype(BF16)) * cross)[0:l_in]
    block_decay = jnp.exp(jnp.full((1, RET_DV), lg * l_in, F32))
    state[h] = r_old * block_decay + _dot(k_dec.T.astype(BF16), v_b)
    mu = jnp.mean(o, axis=-1, keepdims=True)
    oc = o - mu
    var = jnp.mean(oc * oc, axis=-1, keepdims=True)
    return oc * lax.rsqrt(var + GN_EPS)


def _ret_mixer_kernel(lg_ref, x_ref, win_ref, inv_ref, r0_ref, wout_ref, gain_ref, bias_ref, *rest,
                      lb, chunk, nsub, nstep, pos_base, n_cast):
    cast_in = rest[:n_cast]
    o_ref, rout_ref = rest[n_cast:n_cast + 2]
    cast_out = rest[n_cast + 2:2 * n_cast + 2]
    state, decay, gated = rest[2 * n_cast + 2:]
    _cast_slabs(cast_in, cast_out)
    b = pl.program_id(0)
    n = pl.program_id(1)

    @pl.when((b == 0) & (n == 0))
    def _():
        _build_decay(decay, lg_ref, lb, chunk)

    @pl.when(n == 0)
    def _():
        state[...] = r0_ref[...]

    for sb in range(nsub):
        rows = slice(sb * lb, (sb + 1) * lb)
        x = x_ref[rows, :]
        xb = x.astype(BF16)
        row = (n * nsub + sb) * lb + lax.broadcasted_iota(jnp.int32, (lb, 1), 0)
        ang = (pos_base + row).astype(F32) * inv_ref[...]
        cos = jnp.cos(ang)
        sin = jnp.sin(ang)
        q_all = _dot(xb, win_ref[:, 0:RET_QK])
        k_all = _dot(xb, win_ref[:, RET_QK:2 * RET_QK])
        v_all = _dot(xb, win_ref[:, 2 * RET_QK:2 * RET_QK + RET_V]).astype(BF16)
        g_all = _dot(xb, win_ref[:, 2 * RET_QK + RET_V:])
        for h in range(RET_HEADS):
            qk = slice(h * RET_DK, (h + 1) * RET_DK)
            vv = slice(h * RET_DV, (h + 1) * RET_DV)
            q_h = _rotary(q_all[:, qk], cos, sin).astype(BF16)
            k_h = _rotary(k_all[:, qk], cos, sin) * (RET_DK ** -0.5)
            on = _retain_head(q_h, k_h, v_all[:, vv], state, decay, h, lg_ref[h], lb)
            gated[rows, vv] = (_silu(g_all[:, vv]) * on).astype(BF16)
        z = DN_ALPHA * x + _dot(gated[rows, :], wout_ref[...])
        o_ref[rows, :] = _layer_norm(z, gain_ref[...], bias_ref[...])

    @pl.when(n == nstep - 1)
    def _():
        rout_ref[...] = state[...]


def _ret_mixer_short_kernel(lg_ref, x_ref, win_ref, inv_ref, r0_ref, wout_ref, gain_ref, bias_ref,
                            o_ref, rout_ref, state, decay, q_s, k_s, v_s, g_s, gated,
                            q_pad, k_pad, v_pad, *, lb, seq_len, n_seq, pos_base):
    b = pl.program_id(0)

    @pl.when(b == 0)
    def _():
        _build_decay(decay, lg_ref, lb, seq_len)
        for pad in (q_pad, k_pad, v_pad):
            pad[...] = jnp.zeros_like(pad)
        xb = x_ref[...].astype(BF16)
        row = lax.broadcasted_iota(jnp.int32, (xb.shape[0], 1), 0)
        ang = (pos_base + (row & (seq_len - 1))).astype(F32) * inv_ref[...]
        cos = jnp.cos(ang)
        sin = jnp.sin(ang)
        q_all = _dot(xb, win_ref[:, 0:RET_QK])
        k_all = _dot(xb, win_ref[:, RET_QK:2 * RET_QK])
        for h in range(RET_HEADS):
            qk = slice(h * RET_DK, (h + 1) * RET_DK)
            q_s[:, qk] = _rotary(q_all[:, qk], cos, sin).astype(BF16)
            k_s[:, qk] = _rotary(k_all[:, qk], cos, sin) * (RET_DK ** -0.5)
        v_s[...] = _dot(xb, win_ref[:, 2 * RET_QK:2 * RET_QK + RET_V]).astype(BF16)
        g_s[...] = _dot(xb, win_ref[:, 2 * RET_QK + RET_V:])

    state[...] = r0_ref[...]
    rows = pl.ds(pl.multiple_of(b * seq_len, seq_len), seq_len)
    q_pad[0:seq_len, :] = q_s[rows, :]
    k_pad[0:seq_len, :] = k_s[rows, :]
    v_pad[0:seq_len, :] = v_s[rows, :]
    for h in range(RET_HEADS):
        qk = slice(h * RET_DK, (h + 1) * RET_DK)
        vv = slice(h * RET_DV, (h + 1) * RET_DV)
        on = _retain_head(q_pad[:, qk], k_pad[:, qk], v_pad[:, vv], state, decay, h, lg_ref[h],
                          seq_len)
        gated[rows, vv] = (_silu(g_s[rows, vv]) * on).astype(BF16)
    rout_ref[...] = state[...]

    @pl.when(b == n_seq - 1)
    def _():
        z = DN_ALPHA * x_ref[...] + _dot(gated[...], wout_ref[...])
        o_ref[...] = _layer_norm(z, gain_ref[...], bias_ref[...])


def _ret_mixer(x, win_bf, inv, r0, log_g, wout_bf, gain, bias, cast=(),
               *, n_seq, seq_len, pos_base, lb, nsub):
    t = n_seq * seq_len
    st_shape = (None, RET_HEADS, RET_DK, RET_DV)
    weights = [_resident((D_MODEL, RET_IN)), _resident((1, ROPE_HALF))]
    tail = [_resident((RET_V, D_MODEL)), _resident((1, D_MODEL)), _resident((1, D_MODEL))]
    out_shape = [jax.ShapeDtypeStruct((t, D_MODEL), F32),
                 jax.ShapeDtypeStruct((n_seq, RET_HEADS, RET_DK, RET_DV), F32)]
    scratch = [pltpu.VMEM((RET_HEADS, RET_DK, RET_DV), F32), pltpu.VMEM((RET_HEADS, lb, lb), F32)]
    smem = pl.BlockSpec(memory_space=pltpu.SMEM)
    if seq_len < lb:
        assert seq_len <= CHUNK and seq_len & (seq_len - 1) == 0 and lb % LANES == 0
        assert not cast
        st = pl.BlockSpec(st_shape, lambda b: (b, 0, 0, 0))
        scratch += [pltpu.VMEM((t, RET_QK), BF16), pltpu.VMEM((t, RET_QK), F32),
                    pltpu.VMEM((t, RET_V), BF16), pltpu.VMEM((t, RET_V), F32),
                    pltpu.VMEM((t, RET_V), BF16), pltpu.VMEM((lb, RET_QK), BF16),
                    pltpu.VMEM((lb, RET_QK), F32), pltpu.VMEM((lb, RET_V), BF16)]
        y, r_new = pl.pallas_call(
            functools.partial(_ret_mixer_short_kernel, lb=lb, seq_len=seq_len, n_seq=n_seq,
                              pos_base=pos_base),
            name="ret_mixer_short",
            grid=(n_seq,),
            in_specs=[smem, _resident((t, D_MODEL))] + weights + [st] + tail,
            out_specs=[pl.BlockSpec((t, D_MODEL), lambda b: (0, 0)), st],
            out_shape=out_shape,
            scratch_shapes=scratch,
            compiler_params=_params("arbitrary"),
        )(log_g, x, win_bf, inv, r0, wout_bf, gain, bias)
        return y, r_new, ()
    assert seq_len % (nsub * lb) == 0 and lb % CHUNK == 0 and CHUNK & (CHUNK - 1) == 0
    nstep = seq_len // (nsub * lb)
    rows = pl.BlockSpec((nsub * lb, D_MODEL), lambda b, n: (b * nstep + n, 0))
    st = pl.BlockSpec(st_shape, lambda b, n: (b, 0, 0, 0))
    cast_specs = [pl.BlockSpec(shape, lambda b, n, at=at: at(b * nstep + n)) for shape, at in
                  _cast_blocks(cast, n_seq * nstep)]
    outs = pl.pallas_call(
        functools.partial(_ret_mixer_kernel, lb=lb, chunk=CHUNK, nsub=nsub, nstep=nstep,
                          pos_base=pos_base, n_cast=len(cast)),
        name="ret_mixer",
        grid=(n_seq, nstep),
        in_specs=[smem, rows] + weights + [st] + tail + cast_specs,
        out_specs=[rows, st] + cast_specs,
        out_shape=out_shape + [jax.ShapeDtypeStruct(a.shape, BF16) for a in cast],
        scratch_shapes=scratch + [pltpu.VMEM((nsub * lb, RET_V), BF16)],
        compiler_params=_params("arbitrary", "arbitrary"),
    )(log_g, x, win_bf, inv, r0, wout_bf, gain, bias, *cast)
    return outs[0], outs[1], tuple(outs[2:])


def _ffn_ln_kernel(x_ref, wgu_ref, wd_ref, gain_ref, bias_ref, o_ref, *, tm, mc):
    for c in range(tm // mc):
        rows = slice(c * mc, (c + 1) * mc)
        x = x_ref[rows, :]
        xb = x.astype(BF16)
        gate = _dot(xb, wgu_ref[:, 0:D_FF])
        up = _dot(xb, wgu_ref[:, D_FF:2 * D_FF])
        hid = (_silu(gate) * up).astype(BF16)
        z = DN_ALPHA * x + _dot(hid, wd_ref[...])
        o_ref[rows, :] = _layer_norm(z, gain_ref[...], bias_ref[...])


def _ffn_ln(x, wgu_bf, wd_bf, gain, bias, *, tm, mc):
    t = x.shape[0]
    assert t % tm == 0 and tm % mc == 0
    return pl.pallas_call(
        functools.partial(_ffn_ln_kernel, tm=tm, mc=mc),
        name="ffn_ln",
        grid=(t // tm,),
        in_specs=[pl.BlockSpec((tm, D_MODEL), lambda i: (i, 0)),
                  _resident((D_MODEL, 2 * D_FF)), _resident((D_FF, D_MODEL)),
                  _resident((1, D_MODEL)), _resident((1, D_MODEL))],
        out_specs=pl.BlockSpec((tm, D_MODEL), lambda i: (i, 0)),
        out_shape=jax.ShapeDtypeStruct((t, D_MODEL), F32),
        compiler_params=_params("parallel"),
    )(x, wgu_bf, wd_bf, gain, bias)


def _conv_ln_kernel(x_ref, win_ref, wc_ref, wout_ref, buf_ref, gain_ref, bias_ref, *rest,
                    tm, mc, seg, whole_seqs, tiles_per_seq, n_cast):
    cast_in = rest[:n_cast]
    o_ref, ob_ref, st_ref = rest[n_cast:n_cast + 3]
    cast_out = rest[n_cast + 3:2 * n_cast + 3]
    u_win, prev, mixed = rest[2 * n_cast + 3:]
    _cast_slabs(cast_in, cast_out)
    i = pl.program_id(0)
    w0 = wc_ref[0:1, :]
    w1 = wc_ref[1:2, :]
    w2 = wc_ref[2:3, :]
    tail = CONV_WIDTH - 1
    segs = mc // seg
    for c in range(tm // mc):
        rows = slice(c * mc, (c + 1) * mc)
        x = x_ref[rows, :]
        xb = x.astype(BF16)
        gate_b = _dot(xb, win_ref[:, 0:D_MODEL])
        u = _dot(xb, win_ref[:, D_MODEL:2 * D_MODEL]) * _dot(xb, win_ref[:, 2 * D_MODEL:])
        win = u_win.at[c]
        for s in range(segs):
            sg = c * segs + s
            if whole_seqs:
                prev[SUBLANES - tail:SUBLANES, :] = buf_ref[sg]
            elif sg == 0:
                carried = prev[SUBLANES - tail:SUBLANES, :]
                prev[SUBLANES - tail:SUBLANES, :] = jnp.where(i % tiles_per_seq == 0,
                                                              buf_ref[0], carried)
            win[0:SUBLANES, :] = prev[...]
            win[SUBLANES:SUBLANES + seg, :] = u[s * seg:(s + 1) * seg]
            conv = (w0 * win[SUBLANES - 2:SUBLANES - 2 + seg, :]
                    + w1 * win[SUBLANES - 1:SUBLANES - 1 + seg, :]
                    + w2 * win[SUBLANES:SUBLANES + seg, :])
            mixed[c * mc + s * seg:c * mc + (s + 1) * seg, :] = (
                gate_b[s * seg:(s + 1) * seg] * conv).astype(BF16)
            prev[...] = win[seg:seg + SUBLANES, :]
            if whole_seqs:
                st_ref[sg] = win[SUBLANES + seg - tail:SUBLANES + seg, :]
            elif sg == tm // seg - 1:
                st_ref[0] = win[SUBLANES + seg - tail:SUBLANES + seg, :]
        z = DN_ALPHA * x + _dot(mixed[rows, :], wout_ref[...])
        out = _layer_norm(z, gain_ref[...], bias_ref[...])
        o_ref[rows, :] = out
        ob_ref[rows, :] = out.astype(BF16)


def _conv_ln(x, win_bf, wconv, wout_bf, buf, gain, bias, cast=(), *, n_seq, seq_len, tm, mc):
    t = x.shape[0]
    seg = min(seq_len, mc)
    assert t % tm == 0 and tm % mc == 0 and mc % seg == 0 and seg % SUBLANES == 0
    assert seg == seq_len or seq_len % tm == 0
    tiles_per_seq = max(seq_len // tm, 1)
    seq_per_tile = tm // seg if seg == seq_len else 1
    tail = CONV_WIDTH - 1
    steps = t // tm
    st_spec = pl.BlockSpec((seq_per_tile, tail, D_MODEL), lambda i: (i // tiles_per_seq, 0, 0))
    cast_specs = [pl.BlockSpec(shape, lambda i, at=at: at(i)) for shape, at in
                  _cast_blocks(cast, steps)]
    outs = pl.pallas_call(
        functools.partial(_conv_ln_kernel, tm=tm, mc=mc, seg=seg, whole_seqs=seg == seq_len,
                          tiles_per_seq=tiles_per_seq, n_cast=len(cast)),
        name="conv_ln",
        grid=(steps,),
        in_specs=[pl.BlockSpec((tm, D_MODEL), lambda i: (i, 0)),
                  _resident((D_MODEL, 3 * D_MODEL)), _resident((CONV_WIDTH, D_MODEL)),
                  _resident((D_MODEL, D_MODEL)), st_spec,
                  _resident((1, D_MODEL)), _resident((1, D_MODEL))] + cast_specs,
        out_specs=[pl.BlockSpec((tm, D_MODEL), lambda i: (i, 0)),
                   pl.BlockSpec((tm, D_MODEL), lambda i: (i, 0)), st_spec] + cast_specs,
        out_shape=[jax.ShapeDtypeStruct((t, D_MODEL), F32),
                   jax.ShapeDtypeStruct((t, D_MODEL), BF16),
                   jax.ShapeDtypeStruct((n_seq, tail, D_MODEL), F32)]
                  + [jax.ShapeDtypeStruct(a.shape, BF16) for a in cast],
        scratch_shapes=[pltpu.VMEM((tm // mc, SUBLANES + seg, D_MODEL), F32),
                        pltpu.VMEM((SUBLANES, D_MODEL), F32),
                        pltpu.VMEM((tm, D_MODEL), BF16)],
        compiler_params=_params("arbitrary"),
    )(x, win_bf, wconv, wout_bf, buf, gain, bias, *cast)
    return outs[0], outs[1], outs[2], tuple(outs[3:])


def _split_bf16(a):
    hi = a.astype(BF16)
    lo = (a - hi.astype(F32)).astype(BF16)
    return hi, lo


def _router_kernel(x_ref, wr_ref, rank_ref, gate_ref, cnt_ref, *, tm):
    x_hi, x_lo = _split_bf16(x_ref[...])
    w_hi, w_lo = _split_bf16(wr_ref[...])
    hi = _dot(x_hi, jnp.concatenate([w_hi, w_lo], axis=1))
    logits = hi[:, 0:LANES] + (_dot(x_lo, w_hi) + hi[:, LANES:])
    lt = logits.T[0:N_EXPERTS, :]
    eidx = lax.broadcasted_iota(jnp.int32, (N_EXPERTS, tm), 0).astype(F32)
    neg = jnp.float32(-jnp.inf)
    m1 = jnp.max(lt, axis=0, keepdims=True)
    i1 = jnp.min(jnp.where(lt == m1, eidx, float(N_EXPERTS)), axis=0, keepdims=True)
    first = eidx == i1
    rest = jnp.where(first, neg, lt)
    m2 = jnp.max(rest, axis=0, keepdims=True)
    i2 = jnp.min(jnp.where(rest == m2, eidx, float(N_EXPERTS)), axis=0, keepdims=True)
    second = eidx == i2
    e2 = jnp.exp(m2 - m1)
    w1 = 1.0 / (1.0 + e2)
    w2 = e2 / (1.0 + e2)
    gate_ref[...] = jnp.where(first, w1, 0.0) + jnp.where(second, w2, 0.0)
    sel = first | second
    self32 = jnp.where(sel, 1.0, 0.0)

    c = CUMSUM_CHUNK if tm % CUMSUM_CHUNK == 0 else tm
    si = lax.broadcasted_iota(jnp.int32, (c, c), 0)
    ti = lax.broadcasted_iota(jnp.int32, (c, c), 1)
    tri = jnp.where(si < ti, 1.0, 0.0).astype(BF16)
    offset = jnp.zeros((N_EXPERTS, 1), F32)
    for j in range(tm // c):
        blk = self32[:, j * c:(j + 1) * c]
        rank = _dot(blk.astype(BF16), tri) + offset
        rank_ref[:, j * c:(j + 1) * c] = jnp.where(sel[:, j * c:(j + 1) * c],
                                                   rank.astype(jnp.int32), -1)
        offset = offset + jnp.sum(blk, axis=1, keepdims=True)
    cnt_ref[...] = jnp.broadcast_to(offset.astype(jnp.int32), (N_EXPERTS, LANES))


def _router(x, wr_pad, *, tm):
    t = x.shape[0]
    assert t % tm == 0
    nt = t // tm
    return pl.pallas_call(
        functools.partial(_router_kernel, tm=tm),
        name="router",
        grid=(nt,),
        in_specs=[pl.BlockSpec((tm, D_MODEL), lambda i: (i, 0)), _resident((D_MODEL, LANES))],
        out_specs=[pl.BlockSpec((N_EXPERTS, tm), lambda i: (0, i)),
                   pl.BlockSpec((N_EXPERTS, tm), lambda i: (0, i)),
                   pl.BlockSpec((None, N_EXPERTS, LANES), lambda i: (i, 0, 0))],
        out_shape=[jax.ShapeDtypeStruct((N_EXPERTS, t), jnp.int32),
                   jax.ShapeDtypeStruct((N_EXPERTS, t), F32),
                   jax.ShapeDtypeStruct((nt, N_EXPERTS, LANES), jnp.int32)],
        compiler_params=_params("parallel"),
    )(x, wr_pad)


def _moe_ln_kernel(cnt_ref, xb_ref, x_hbm, rank_ref, gate_ref, wgu_ref, wd_ref,
                   gain_ref, bias_ref, out_hbm, buf, rsem, wsem, *, tm, nt):
    e = pl.program_id(0)
    i = pl.program_id(1)
    ne = pl.num_programs(0)
    single_tile = nt == 1
    s = e * nt + i
    slot = 0 if single_tile else lax.rem(s, MOE_BUFFERS)
    o_ref = buf.at[slot]

    def tile_rows(hbm, tile):
        start = tile * tm
        if not isinstance(start, int):
            start = pl.multiple_of(start, tm)
        return hbm.at[pl.ds(start, tm), :]

    def read_x(tile, dst):
        return pltpu.make_async_copy(tile_rows(x_hbm, tile), buf.at[dst], rsem.at[dst])

    def read_sum(tile, dst):
        return pltpu.make_async_copy(tile_rows(out_hbm, tile), buf.at[dst], rsem.at[dst])

    def write_sum(tile, src):
        return pltpu.make_async_copy(buf.at[src], tile_rows(out_hbm, tile), wsem.at[src])

    if single_tile:
        @pl.when(e == 0)
        def _():
            first = read_x(0, 0)
            first.start()
            first.wait()
    else:
        next_slot = lax.rem(s + 1, MOE_BUFFERS)
        next_tile = jnp.where(i + 1 < nt, i + 1, 0)

        @pl.when(s == 0)
        def _():
            read_x(0, 0).start()

        @pl.when(s >= 2)
        def _():
            write_sum(jnp.where(i >= 2, i - 2, i - 2 + nt), next_slot).wait()

        @pl.when(s + 1 < nt)
        def _():
            read_x(next_tile, next_slot).start()

        @pl.when((s + 1 >= nt) & (s + 1 < ne * nt))
        def _():
            read_sum(next_tile, next_slot).start()

        @pl.when(e == 0)
        def _():
            read_x(i, slot).wait()

        @pl.when(e > 0)
        def _():
            read_sum(i, slot).wait()

    count = cnt_ref[i * N_EXPERTS + e]
    rank = rank_ref[pl.ds(e, 1), :]
    gate = gate_ref[pl.ds(e, 1), :]

    def expert_rows(row0, nrows):
        rows = row0 + lax.broadcasted_iota(jnp.int32, (nrows, 1), 0)
        hit = rank == rows
        onehot = jnp.where(hit, 1.0, 0.0)
        xg = _dot(onehot.astype(BF16), xb_ref[...]).astype(BF16)
        gcol = jnp.sum(jnp.where(hit, gate, 0.0), axis=1, keepdims=True) * (1.0 / DN_ALPHA)
        hid = (_silu(_dot(xg, wgu_ref[:, 0:D_FF])) * _dot(xg, wgu_ref[:, D_FF:])).astype(BF16)
        y = (_dot(hid, wd_ref[...]) * gcol).astype(BF16)
        o_ref[...] += _dot(onehot.T.astype(BF16), y)

    mean_rows = tm * TOP_K // N_EXPERTS
    single_rows = (mean_rows, mean_rows + MOE_SPARE_ROWS)
    lo = 0
    for nrows in single_rows:
        @pl.when((count > lo) & (count <= nrows))
        def _(nrows=nrows):
            expert_rows(0, nrows)
        lo = nrows

    @pl.when(count > single_rows[-1])
    def _():
        def full_block(r, carry):
            expert_rows(r * MOE_ROWS, MOE_ROWS)
            return carry

        lax.fori_loop(0, lax.shift_right_logical(count + (MOE_ROWS - 1), MOE_ROWS_LOG2),
                      full_block, 0)

    @pl.when(e == ne - 1)
    def _():
        o_ref[...] = _layer_norm(DN_ALPHA * o_ref[...], gain_ref[...], bias_ref[...])

    if single_tile:
        @pl.when(e == ne - 1)
        def _():
            done = write_sum(0, 0)
            done.start()
            done.wait()
    else:
        write_sum(i, slot).start()

        @pl.when(s == ne * nt - 1)
        def _():
            write_sum(i - 1, lax.rem(s + MOE_BUFFERS - 1, MOE_BUFFERS)).wait()
            write_sum(i, slot).wait()


def _moe_vmem_bytes(tm, weight_buffers):
    rows = tm * TOP_K // N_EXPERTS + MOE_SPARE_ROWS
    weights = weight_buffers * 3 * D_MODEL * D_FF * 2
    tiles = tm * D_MODEL * (MOE_BUFFERS * 4 + 2 * 2)
    values = rows * D_FF * (4 + 4 + 2) + rows * tm * (4 + 2) * 2 + tm * D_MODEL * 4
    return weights + tiles + values


def _moe_ln(xb, x, rank, gate, counts, wgu_bf, wd_bf, gain, bias, *, tm):
    t = x.shape[0]
    assert t % tm == 0
    nt = t // tm
    assert nt == 1 or nt >= MOE_BUFFERS
    per_tile = pl.BlockSpec((N_EXPERTS, tm), lambda e, i, c: (0, i))
    const = lambda shape: pl.BlockSpec(shape, lambda e, i, c: (0,) * len(shape),
                                       pipeline_mode=pl.Buffered(1))
    weight_buffers = 2 if _moe_vmem_bytes(tm, 2) <= V7X_VMEM_LIMIT_BYTES else 1
    down_bytes = D_FF * D_MODEL * 2
    down_buffers = 2 if _moe_vmem_bytes(tm, 1) + down_bytes <= V7X_VMEM_LIMIT_BYTES else 1
    grid_spec = pltpu.PrefetchScalarGridSpec(
        num_scalar_prefetch=1,
        grid=(N_EXPERTS, nt),
        in_specs=[pl.BlockSpec((tm, D_MODEL), lambda e, i, c: (i, 0)),
                  pl.BlockSpec(memory_space=pl.ANY), per_tile, per_tile,
                  pl.BlockSpec((None, D_MODEL, 2 * D_FF), lambda e, i, c: (e, 0, 0),
                               pipeline_mode=pl.Buffered(weight_buffers)),
                  pl.BlockSpec((None, D_FF, D_MODEL), lambda e, i, c: (e, 0, 0),
                               pipeline_mode=pl.Buffered(down_buffers)),
                  const((1, D_MODEL)), const((1, D_MODEL))],
        out_specs=pl.BlockSpec(memory_space=pl.ANY),
        scratch_shapes=[pltpu.VMEM((1 if nt == 1 else MOE_BUFFERS, tm, D_MODEL), F32),
                        pltpu.SemaphoreType.DMA((MOE_BUFFERS,)),
                        pltpu.SemaphoreType.DMA((MOE_BUFFERS,))],
    )
    return pl.pallas_call(
        functools.partial(_moe_ln_kernel, tm=tm, nt=nt),
        name="moe_ln",
        grid_spec=grid_spec,
        out_shape=jax.ShapeDtypeStruct((t, D_MODEL), F32),
        compiler_params=_params("arbitrary", "arbitrary"),
    )(counts, xb, x, rank, gate, wgu_bf, wd_bf, gain, bias)


def _trunk(x, n_seq, seq_len, pos_base, r0, conv_buf, w, late_w, moe_w,
           *, tm, tm_ffn, mc, lb, nsub, tm_moe):
    cast = tuple(late_w) if late_w[0].dtype == F32 else ()
    x, r_new, cast_out = _ret_mixer(x, w["ret_in"], w["inv"], r0, w["log_g"], w["ret_out"],
                                    w["gain"][0], w["bias"][0], cast, n_seq=n_seq,
                                    seq_len=seq_len, pos_base=pos_base, lb=lb, nsub=nsub)
    late_w = cast_out if cast else late_w
    ffn_gu, ffn_down, conv_in, conv_out = (a[0] for a in late_w)
    x = _ffn_ln(x, ffn_gu, ffn_down, w["gain"][1], w["bias"][1], tm=tm_ffn, mc=mc)
    cast = tuple(moe_w) if moe_w[0].dtype == F32 else ()
    x, xb, conv_new, cast_out = _conv_ln(x, conv_in, w["conv_w"], conv_out, conv_buf,
                                         w["gain"][2], w["bias"][2], cast, n_seq=n_seq,
                                         seq_len=seq_len, tm=tm, mc=mc)
    moe_w = cast_out if cast else moe_w
    rank, gate, counts = _router(x, w["router"], tm=tm_moe)
    y = _moe_ln(xb, x, rank, gate, counts[:, :, 0].reshape(-1), moe_w[0], moe_w[1],
                w["gain"][3], w["bias"][3], tm=tm_moe)
    return y, r_new, conv_new, late_w, moe_w


def kernel(x_prompt, x_sample, state_ret, state_conv, ret_w_in, ret_w_out, conv_w_in, conv_w,
           conv_w_out, ffn_w_gu, ffn_w_down, moe_w_router, moe_w_gu, moe_w_down, ln_gain, ln_bias):
    batch, seq, _ = x_prompt.shape
    dec_batch, dec_seq, _ = x_sample.shape

    w = {
        "ret_in": ret_w_in[0].astype(BF16),
        "ret_out": ret_w_out[0].astype(BF16),
        "conv_w": conv_w[0],
        "router": jnp.pad(moe_w_router[0], ((0, 0), (0, LANES - N_EXPERTS))),
        "gain": ln_gain.reshape(2 * DEPTH, 1, D_MODEL),
        "bias": ln_bias.reshape(2 * DEPTH, 1, D_MODEL),
        "inv": (1.0 / (ROPE_BASE ** jnp.linspace(0.0, 1.0, ROPE_HALF, dtype=F32))).reshape(1, ROPE_HALF),
        "log_g": jnp.log1p(-jnp.exp2(-5.0 - jnp.arange(RET_HEADS, dtype=F32))),
    }

    zero_ret = jnp.zeros((batch, RET_HEADS, RET_DK, RET_DV), F32)
    zero_conv = jnp.zeros((batch, CONV_WIDTH - 1, D_MODEL), F32)
    y_p, ret_p, conv_p, late_bf, moe_bf = _trunk(
        x_prompt.reshape(batch * seq, D_MODEL), batch, seq, 0, zero_ret, zero_conv, w,
        (ffn_w_gu, ffn_w_down, conv_w_in, conv_w_out), (moe_w_gu[0], moe_w_down[0]),
        tm=512, tm_ffn=1024, mc=256, lb=256, nsub=2, tm_moe=1024)
    y_s, ret_s, conv_s, _, _ = _trunk(
        x_sample.reshape(dec_batch * dec_seq, D_MODEL), dec_batch, dec_seq, PAST_LEN,
        state_ret[0], state_conv[0], w, late_bf, moe_bf, tm=dec_batch * dec_seq,
        tm_ffn=dec_batch * dec_seq, mc=dec_batch * dec_seq,
        lb=LANES, nsub=1, tm_moe=dec_batch * dec_seq)
    return (y_p.reshape(batch, seq, D_MODEL), y_s.reshape(dec_batch, dec_seq, D_MODEL),
            ret_p[None], ret_s[None], conv_p[None], conv_s[None])
```

```python
import functools
import math

import jax
import jax.numpy as jnp
from jax import lax
from jax.experimental import pallas as pl
from jax.experimental.pallas import tpu as pltpu

D_MODEL = 1024
CHUNK = 64
RET_HEADS = 4
RET_DK = D_MODEL // RET_HEADS
RET_DV = 2 * D_MODEL // RET_HEADS
RET_QK = RET_HEADS * RET_DK
RET_V = RET_HEADS * RET_DV
RET_IN = 2 * RET_QK + 2 * RET_V
ROPE_HALF = RET_DK // 2
CONV_WIDTH = 3
D_FF = 2816
N_EXPERTS = 8
ROPE_BASE = 10000.0
LN_EPS = 1e-5
GN_EPS = 1e-6
DEPTH = 2
DN_ALPHA = (2 * DEPTH) ** 0.25
PAST_LEN = 4096

V7X_VMEM_LIMIT_BYTES = 58 * 1024 * 1024
LANES = 128
SUBLANES = 8
BF16_ROWS = 2 * SUBLANES

MOE_ROWS_LOG2 = 8
MOE_ROWS = 1 << MOE_ROWS_LOG2
TOP_K = 2
MOE_SPARE_ROWS = (0, 32, 64)
MOE_BUFFERS = 3
CUMSUM_CHUNK = 512

F32 = jnp.float32
BF16 = jnp.bfloat16


def _params(*semantics):
    return pltpu.CompilerParams(dimension_semantics=semantics,
                                vmem_limit_bytes=V7X_VMEM_LIMIT_BYTES)


def _resident(shape):
    zeros = (0,) * len(shape)
    return pl.BlockSpec(shape, lambda *_: zeros, pipeline_mode=pl.Buffered(1))


def _layer_norm(z, gain, bias):
    mu = jnp.mean(z, axis=-1, keepdims=True)
    zc = z - mu
    var = jnp.mean(zc * zc, axis=-1, keepdims=True)
    return zc * lax.rsqrt(var + LN_EPS) * gain + bias


def _silu(x):
    return x / (1.0 + jnp.exp(-x))


def _dot(a, b):
    return jnp.dot(a, b, preferred_element_type=F32)


def _cast_blocks(cast, steps):
    blocks = []
    for a in cast:
        n_lead, n_rows, n_cols = a.shape
        per_lead = max(p for p in range(1, n_rows // BF16_ROWS + 1)
                       if n_rows % (p * BF16_ROWS) == 0 and steps % (n_lead * p) == 0)
        repeat = steps // (n_lead * per_lead)

        def at(step, per_lead=per_lead, repeat=repeat):
            slab = step // repeat
            return slab // per_lead, slab % per_lead, 0
        blocks.append(((None, n_rows // per_lead, n_cols), at))
    return blocks


def _cast_slabs(cast_in, cast_out):
    for src, dst in zip(cast_in, cast_out):
        dst[...] = src[...].astype(BF16)


def _rotary(t, cos, sin):
    x1 = t[:, 0:ROPE_HALF]
    x2 = t[:, ROPE_HALF:RET_DK]
    return jnp.concatenate([x1 * cos - x2 * sin, x1 * sin + x2 * cos], axis=-1)


def _build_decay(decay, lg_ref, lb, chunk):
    ii = lax.broadcasted_iota(jnp.int32, (lb, lb), 0)
    jj = lax.broadcasted_iota(jnp.int32, (lb, lb), 1)
    shift = int(math.log2(chunk))
    visible = (jj >> shift) <= (ii >> shift)
    dist = jnp.abs(ii - jj).astype(F32)
    for h in range(RET_HEADS):
        decay[h] = jnp.where(visible, jnp.exp(lg_ref[h] * dist), 0.0)


def _retain_head(q_b, k_f, v_b, state, decay, h, lg, l_in):
    jrow = lax.broadcasted_iota(jnp.int32, (q_b.shape[0], 1), 0).astype(F32)
    k_dec = k_f * jnp.exp(lg * (l_in - 1.0 - jrow))
    r_old = state[h]
    s = lax.dot_general(q_b, k_f.astype(BF16), (((1,), (1,)), ((), ())),
                        preferred_element_type=F32)
    p = (s * decay[h]).astype(BF16)
    cross = jnp.exp(lg * (jrow + 1.0))
    o = (_dot(p, v_b) + _dot(q_b, r_old.astype(BF16)) * cross)[0:l_in]
    block_decay = jnp.exp(jnp.full((1, RET_DV), lg * l_in, F32))
    state[h] = r_old * block_decay + _dot(k_dec.T.astype(BF16), v_b)
    mu = jnp.mean(o, axis=-1, keepdims=True)
    oc = o - mu
    var = jnp.mean(oc * oc, axis=-1, keepdims=True)
    return oc * lax.rsqrt(var + GN_EPS)


def _ret_mixer_kernel(lg_ref, x_ref, win_ref, inv_ref, r0_ref, wout_ref, gain_ref, bias_ref, *rest,
                      lb, chunk, nsub, nstep, pos_base, n_cast):
    cast_in = rest[:n_cast]
    o_ref, rout_ref = rest[n_cast:n_cast + 2]
    cast_out = rest[n_cast + 2:2 * n_cast + 2]
    state, decay, gated = rest[2 * n_cast + 2:]
    _cast_slabs(cast_in, cast_out)
    b = pl.program_id(0)
    n = pl.program_id(1)

    @pl.when((b == 0) & (n == 0))
    def _():
        _build_decay(decay, lg_ref, lb, chunk)

    @pl.when(n == 0)
    def _():
        state[...] = r0_ref[...]

    for sb in range(nsub):
        rows = slice(sb * lb, (sb + 1) * lb)
        x = x_ref[rows, :]
        xb = x.astype(BF16)
        row = (n * nsub + sb) * lb + lax.broadcasted_iota(jnp.int32, (lb, 1), 0)
        ang = (pos_base + row).astype(F32) * inv_ref[...]
        cos = jnp.cos(ang)
        sin = jnp.sin(ang)
        q_all = _dot(xb, win_ref[:, 0:RET_QK])
        k_all = _dot(xb, win_ref[:, RET_QK:2 * RET_QK])
        v_all = _dot(xb, win_ref[:, 2 * RET_QK:2 * RET_QK + RET_V]).astype(BF16)
        g_all = _dot(xb, win_ref[:, 2 * RET_QK + RET_V:])
        for h in range(RET_HEADS):
            qk = slice(h * RET_DK, (h + 1) * RET_DK)
            vv = slice(h * RET_DV, (h + 1) * RET_DV)
            q_h = _rotary(q_all[:, qk], cos, sin).astype(BF16)
            k_h = _rotary(k_all[:, qk], cos, sin) * (RET_DK ** -0.5)
            on = _retain_head(q_h, k_h, v_all[:, vv], state, decay, h, lg_ref[h], lb)
            gated[rows, vv] = (_silu(g_all[:, vv]) * on).astype(BF16)
        z = DN_ALPHA * x + _dot(gated[rows, :], wout_ref[...])
        o_ref[rows, :] = _layer_norm(z, gain_ref[...], bias_ref[...])

    @pl.when(n == nstep - 1)
    def _():
        rout_ref[...] = state[...]


def _ret_mixer_short_kernel(lg_ref, x_ref, win_ref, inv_ref, r0_ref, wout_ref, gain_ref, bias_ref,
                            o_ref, rout_ref, state, decay, q_s, k_s, v_s, g_s, gated,
                            q_pad, k_pad, v_pad, *, lb, seq_len, n_seq, pos_base):
    b = pl.program_id(0)

    @pl.when(b == 0)
    def _():
        _build_decay(decay, lg_ref, lb, seq_len)
        for pad in (q_pad, k_pad, v_pad):
            pad[...] = jnp.zeros_like(pad)
        xb = x_ref[...].astype(BF16)
        row = lax.broadcasted_iota(jnp.int32, (xb.shape[0], 1), 0)
        ang = (pos_base + (row & (seq_len - 1))).astype(F32) * inv_ref[...]
        cos = jnp.cos(ang)
        sin = jnp.sin(ang)
        q_all = _dot(xb, win_ref[:, 0:RET_QK])
        k_all = _dot(xb, win_ref[:, RET_QK:2 * RET_QK])
        for h in range(RET_HEADS):
            qk = slice(h * RET_DK, (h + 1) * RET_DK)
            q_s[:, qk] = _rotary(q_all[:, qk], cos, sin).astype(BF16)
            k_s[:, qk] = _rotary(k_all[:, qk], cos, sin) * (RET_DK ** -0.5)
        v_s[...] = _dot(xb, win_ref[:, 2 * RET_QK:2 * RET_QK + RET_V]).astype(BF16)
        g_s[...] = _dot(xb, win_ref[:, 2 * RET_QK + RET_V:])

    state[...] = r0_ref[...]
    rows = pl.ds(pl.multiple_of(b * seq_len, seq_len), seq_len)
    q_pad[0:seq_len, :] = q_s[rows, :]
    k_pad[0:seq_len, :] = k_s[rows, :]
    v_pad[0:seq_len, :] = v_s[rows, :]
    for h in range(RET_HEADS):
        qk = slice(h * RET_DK, (h + 1) * RET_DK)
        vv = slice(h * RET_DV, (h + 1) * RET_DV)
        on = _retain_head(q_pad[:, qk], k_pad[:, qk], v_pad[:, vv], state, decay, h, lg_ref[h],
                          seq_len)
        gated[rows, vv] = (_silu(g_s[rows, vv]) * on).astype(BF16)
    rout_ref[...] = state[...]

    @pl.when(b == n_seq - 1)
    def _():
        z = DN_ALPHA * x_ref[...] + _dot(gated[...], wout_ref[...])
        o_ref[...] = _layer_norm(z, gain_ref[...], bias_ref[...])


def _ret_mixer(x, win_bf, inv, r0, log_g, wout_bf, gain, bias, cast=(),
               *, n_seq, seq_len, pos_base, lb, nsub):
    t = n_seq * seq_len
    st_shape = (None, RET_HEADS, RET_DK, RET_DV)
    weights = [_resident((D_MODEL, RET_IN)), _resident((1, ROPE_HALF))]
    tail = [_resident((RET_V, D_MODEL)), _resident((1, D_MODEL)), _resident((1, D_MODEL))]
    out_shape = [jax.ShapeDtypeStruct((t, D_MODEL), F32),
                 jax.ShapeDtypeStruct((n_seq, RET_HEADS, RET_DK, RET_DV), F32)]
    scratch = [pltpu.VMEM((RET_HEADS, RET_DK, RET_DV), F32), pltpu.VMEM((RET_HEADS, lb, lb), F32)]
    smem = pl.BlockSpec(memory_space=pltpu.SMEM)
    if seq_len < lb:
        assert seq_len <= CHUNK and seq_len & (seq_len - 1) == 0 and lb % LANES == 0
        assert not cast
        st = pl.BlockSpec(st_shape, lambda b: (b, 0, 0, 0))
        scratch += [pltpu.VMEM((t, RET_QK), BF16), pltpu.VMEM((t, RET_QK), F32),
                    pltpu.VMEM((t, RET_V), BF16), pltpu.VMEM((t, RET_V), F32),
                    pltpu.VMEM((t, RET_V), BF16), pltpu.VMEM((lb, RET_QK), BF16),
                    pltpu.VMEM((lb, RET_QK), F32), pltpu.VMEM((lb, RET_V), BF16)]
        y, r_new = pl.pallas_call(
            functools.partial(_ret_mixer_short_kernel, lb=lb, seq_len=seq_len, n_seq=n_seq,
                              pos_base=pos_base),
            name="ret_mixer_short",
            grid=(n_seq,),
            in_specs=[smem, _resident((t, D_MODEL))] + weights + [st] + tail,
            out_specs=[pl.BlockSpec((t, D_MODEL), lambda b: (0, 0)), st],
            out_shape=out_shape,
            scratch_shapes=scratch,
            compiler_params=_params("arbitrary"),
        )(log_g, x, win_bf, inv, r0, wout_bf, gain, bias)
        return y, r_new, ()
    assert seq_len % (nsub * lb) == 0 and lb % CHUNK == 0 and CHUNK & (CHUNK - 1) == 0
    nstep = seq_len // (nsub * lb)
    rows = pl.BlockSpec((nsub * lb, D_MODEL), lambda b, n: (b * nstep + n, 0))
    st = pl.BlockSpec(st_shape, lambda b, n: (b, 0, 0, 0))
    cast_specs = [pl.BlockSpec(shape, lambda b, n, at=at: at(b * nstep + n)) for shape, at in
                  _cast_blocks(cast, n_seq * nstep)]
    outs = pl.pallas_call(
        functools.partial(_ret_mixer_kernel, lb=lb, chunk=CHUNK, nsub=nsub, nstep=nstep,
                          pos_base=pos_base, n_cast=len(cast)),
        name="ret_mixer",
        grid=(n_seq, nstep),
        in_specs=[smem, rows] + weights + [st] + tail + cast_specs,
        out_specs=[rows, st] + cast_specs,
        out_shape=out_shape + [jax.ShapeDtypeStruct(a.shape, BF16) for a in cast],
        scratch_shapes=scratch + [pltpu.VMEM((nsub * lb, RET_V), BF16)],
        compiler_params=_params("arbitrary", "arbitrary"),
    )(log_g, x, win_bf, inv, r0, wout_bf, gain, bias, *cast)
    return outs[0], outs[1], tuple(outs[2:])


def _ffn_ln_kernel(x_ref, wgu_ref, wd_ref, gain_ref, bias_ref, o_ref, *, tm, mc):
    for c in range(tm // mc):
        rows = slice(c * mc, (c + 1) * mc)
        x = x_ref[rows, :]
        xb = x.astype(BF16)
        gate = _dot(xb, wgu_ref[:, 0:D_FF])
        up = _dot(xb, wgu_ref[:, D_FF:2 * D_FF])
        hid = (_silu(gate) * up).astype(BF16)
        z = DN_ALPHA * x + _dot(hid, wd_ref[...])
        o_ref[rows, :] = _layer_norm(z, gain_ref[...], bias_ref[...])


def _ffn_ln(x, wgu_bf, wd_bf, gain, bias, *, tm, mc):
    t = x.shape[0]
    assert t % tm == 0 and tm % mc == 0
    return pl.pallas_call(
        functools.partial(_ffn_ln_kernel, tm=tm, mc=mc),
        name="ffn_ln",
        grid=(t // tm,),
        in_specs=[pl.BlockSpec((tm, D_MODEL), lambda i: (i, 0)),
                  _resident((D_MODEL, 2 * D_FF)), _resident((D_FF, D_MODEL)),
                  _resident((1, D_MODEL)), _resident((1, D_MODEL))],
        out_specs=pl.BlockSpec((tm, D_MODEL), lambda i: (i, 0)),
        out_shape=jax.ShapeDtypeStruct((t, D_MODEL), F32),
        compiler_params=_params("parallel"),
    )(x, wgu_bf, wd_bf, gain, bias)


def _conv_ln_kernel(x_ref, win_ref, wc_ref, wout_ref, buf_ref, gain_ref, bias_ref, *rest,
                    tm, mc, seg, whole_seqs, tiles_per_seq, n_cast):
    cast_in = rest[:n_cast]
    o_ref, ob_ref, st_ref = rest[n_cast:n_cast + 3]
    cast_out = rest[n_cast + 3:2 * n_cast + 3]
    u_win, prev, mixed = rest[2 * n_cast + 3:]
    _cast_slabs(cast_in, cast_out)
    i = pl.program_id(0)
    w0 = wc_ref[0:1, :]
    w1 = wc_ref[1:2, :]
    w2 = wc_ref[2:3, :]
    tail = CONV_WIDTH - 1
    segs = mc // seg
    for c in range(tm // mc):
        rows = slice(c * mc, (c + 1) * mc)
        x = x_ref[rows, :]
        xb = x.astype(BF16)
        gate_b = _dot(xb, win_ref[:, 0:D_MODEL])
        u = _dot(xb, win_ref[:, D_MODEL:2 * D_MODEL]) * _dot(xb, win_ref[:, 2 * D_MODEL:])
        win = u_win.at[c]
        for s in range(segs):
            sg = c * segs + s
            if whole_seqs:
                prev[SUBLANES - tail:SUBLANES, :] = buf_ref[sg]
            elif sg == 0:
                carried = prev[SUBLANES - tail:SUBLANES, :]
                prev[SUBLANES - tail:SUBLANES, :] = jnp.where(i % tiles_per_seq == 0,
                                                              buf_ref[0], carried)
            win[0:SUBLANES, :] = prev[...]
            win[SUBLANES:SUBLANES + seg, :] = u[s * seg:(s + 1) * seg]
            conv = (w0 * win[SUBLANES - 2:SUBLANES - 2 + seg, :]
                    + w1 * win[SUBLANES - 1:SUBLANES - 1 + seg, :]
                    + w2 * win[SUBLANES:SUBLANES + seg, :])
            mixed[c * mc + s * seg:c * mc + (s + 1) * seg, :] = (
                gate_b[s * seg:(s + 1) * seg] * conv).astype(BF16)
            prev[...] = win[seg:seg + SUBLANES, :]
            if whole_seqs:
                st_ref[sg] = win[SUBLANES + seg - tail:SUBLANES + seg, :]
            elif sg == tm // seg - 1:
                st_ref[0] = win[SUBLANES + seg - tail:SUBLANES + seg, :]
        z = DN_ALPHA * x + _dot(mixed[rows, :], wout_ref[...])
        out = _layer_norm(z, gain_ref[...], bias_ref[...])
        o_ref[rows, :] = out
        ob_ref[rows, :] = out.astype(BF16)


def _conv_ln(x, win_bf, wconv, wout_bf, buf, gain, bias, cast=(), *, n_seq, seq_len, tm, mc):
    t = x.shape[0]
    seg = min(seq_len, mc)
    assert t % tm == 0 and tm % mc == 0 and mc % seg == 0 and seg % SUBLANES == 0
    assert seg == seq_len or seq_len % tm == 0
    tiles_per_seq = max(seq_len // tm, 1)
    seq_per_tile = tm // seg if seg == seq_len else 1
    tail = CONV_WIDTH - 1
    steps = t // tm
    st_spec = pl.BlockSpec((seq_per_tile, tail, D_MODEL), lambda i: (i // tiles_per_seq, 0, 0))
    cast_specs = [pl.BlockSpec(shape, lambda i, at=at: at(i)) for shape, at in
                  _cast_blocks(cast, steps)]
    outs = pl.pallas_call(
        functools.partial(_conv_ln_kernel, tm=tm, mc=mc, seg=seg, whole_seqs=seg == seq_len,
                          tiles_per_seq=tiles_per_seq, n_cast=len(cast)),
        name="conv_ln",
        grid=(steps,),
        in_specs=[pl.BlockSpec((tm, D_MODEL), lambda i: (i, 0)),
                  _resident((D_MODEL, 3 * D_MODEL)), _resident((CONV_WIDTH, D_MODEL)),
                  _resident((D_MODEL, D_MODEL)), st_spec,
                  _resident((1, D_MODEL)), _resident((1, D_MODEL))] + cast_specs,
        out_specs=[pl.BlockSpec((tm, D_MODEL), lambda i: (i, 0)),
                   pl.BlockSpec((tm, D_MODEL), lambda i: (i, 0)), st_spec] + cast_specs,
        out_shape=[jax.ShapeDtypeStruct((t, D_MODEL), F32),
                   jax.ShapeDtypeStruct((t, D_MODEL), BF16),
                   jax.ShapeDtypeStruct((n_seq, tail, D_MODEL), F32)]
                  + [jax.ShapeDtypeStruct(a.shape, BF16) for a in cast],
        scratch_shapes=[pltpu.VMEM((tm // mc, SUBLANES + seg, D_MODEL), F32),
                        pltpu.VMEM((SUBLANES, D_MODEL), F32),
                        pltpu.VMEM((tm, D_MODEL), BF16)],
        compiler_params=_params("arbitrary"),
    )(x, win_bf, wconv, wout_bf, buf, gain, bias, *cast)
    return outs[0], outs[1], outs[2], tuple(outs[3:])


def _split_bf16(a):
    hi = a.astype(BF16)
    lo = (a - hi.astype(F32)).astype(BF16)
    return hi, lo


def _router_kernel(x_ref, wr_ref, rank_ref, gate_ref, cnt_ref, *, tm):
    x_hi, x_lo = _split_bf16(x_ref[...])
    w_hi, w_lo = _split_bf16(wr_ref[...])
    hi = _dot(x_hi, jnp.concatenate([w_hi, w_lo], axis=1))
    logits = hi[:, 0:LANES] + (_dot(x_lo, w_hi) + hi[:, LANES:])
    lt = logits.T[0:N_EXPERTS, :]
    eidx = lax.broadcasted_iota(jnp.int32, (N_EXPERTS, tm), 0).astype(F32)
    neg = jnp.float32(-jnp.inf)
    m1 = jnp.max(lt, axis=0, keepdims=True)
    i1 = jnp.min(jnp.where(lt == m1, eidx, float(N_EXPERTS)), axis=0, keepdims=True)
    first = eidx == i1
    rest = jnp.where(first, neg, lt)
    m2 = jnp.max(rest, axis=0, keepdims=True)
    i2 = jnp.min(jnp.where(rest == m2, eidx, float(N_EXPERTS)), axis=0, keepdims=True)
    second = eidx == i2
    e2 = jnp.exp(m2 - m1)
    w1 = 1.0 / (1.0 + e2)
    w2 = e2 / (1.0 + e2)
    gate_ref[...] = jnp.where(first, w1, 0.0) + jnp.where(second, w2, 0.0)
    sel = first | second
    self32 = jnp.where(sel, 1.0, 0.0)

    c = CUMSUM_CHUNK if tm % CUMSUM_CHUNK == 0 else tm
    si = lax.broadcasted_iota(jnp.int32, (c, c), 0)
    ti = lax.broadcasted_iota(jnp.int32, (c, c), 1)
    tri = jnp.where(si < ti, 1.0, 0.0).astype(BF16)
    offset = jnp.zeros((N_EXPERTS, 1), F32)
    for j in range(tm // c):
        blk = self32[:, j * c:(j + 1) * c]
        rank = _dot(blk.astype(BF16), tri) + offset
        rank_ref[:, j * c:(j + 1) * c] = jnp.where(sel[:, j * c:(j + 1) * c],
                                                   rank.astype(jnp.int32), -1)
        offset = offset + jnp.sum(blk, axis=1, keepdims=True)
    cnt_ref[...] = jnp.broadcast_to(offset.astype(jnp.int32), (N_EXPERTS, LANES))


def _router(x, wr_pad, *, tm):
    t = x.shape[0]
    assert t % tm == 0
    nt = t // tm
    return pl.pallas_call(
        functools.partial(_router_kernel, tm=tm),
        name="router",
        grid=(nt,),
        in_specs=[pl.BlockSpec((tm, D_MODEL), lambda i: (i, 0)), _resident((D_MODEL, LANES))],
        out_specs=[pl.BlockSpec((N_EXPERTS, tm), lambda i: (0, i)),
                   pl.BlockSpec((N_EXPERTS, tm), lambda i: (0, i)),
                   pl.BlockSpec((None, N_EXPERTS, LANES), lambda i: (i, 0, 0))],
        out_shape=[jax.ShapeDtypeStruct((N_EXPERTS, t), jnp.int32),
                   jax.ShapeDtypeStruct((N_EXPERTS, t), F32),
                   jax.ShapeDtypeStruct((nt, N_EXPERTS, LANES), jnp.int32)],
        compiler_params=_params("parallel"),
    )(x, wr_pad)


def _moe_ln_kernel(cnt_ref, xb_ref, x_hbm, rank_ref, gate_ref, wgu_ref, wd_ref,
                   gain_ref, bias_ref, out_hbm, buf, rsem, wsem, *, tm, nt):
    e = pl.program_id(0)
    i = pl.program_id(1)
    ne = pl.num_programs(0)
    single_tile = nt == 1
    s = e * nt + i
    slot = 0 if single_tile else lax.rem(s, MOE_BUFFERS)
    o_ref = buf.at[slot]

    def tile_rows(hbm, tile):
        start = tile * tm
        if not isinstance(start, int):
            start = pl.multiple_of(start, tm)
        return hbm.at[pl.ds(start, tm), :]

    def read_x(tile, dst):
        return pltpu.make_async_copy(tile_rows(x_hbm, tile), buf.at[dst], rsem.at[dst])

    def read_sum(tile, dst):
        return pltpu.make_async_copy(tile_rows(out_hbm, tile), buf.at[dst], rsem.at[dst])

    def write_sum(tile, src):
        return pltpu.make_async_copy(buf.at[src], tile_rows(out_hbm, tile), wsem.at[src])

    if single_tile:
        @pl.when(e == 0)
        def _():
            first = read_x(0, 0)
            first.start()
            first.wait()
    else:
        next_slot = lax.rem(s + 1, MOE_BUFFERS)
        next_tile = jnp.where(i + 1 < nt, i + 1, 0)

        @pl.when(s == 0)
        def _():
            read_x(0, 0).start()

        @pl.when(s >= 2)
        def _():
            write_sum(jnp.where(i >= 2, i - 2, i - 2 + nt), next_slot).wait()

        @pl.when(s + 1 < nt)
        def _():
            read_x(next_tile, next_slot).start()

        @pl.when((s + 1 >= nt) & (s + 1 < ne * nt))
        def _():
            read_sum(next_tile, next_slot).start()

        @pl.when(e == 0)
        def _():
            read_x(i, slot).wait()

        @pl.when(e > 0)
        def _():
            read_sum(i, slot).wait()

    count = cnt_ref[i * N_EXPERTS + e]
    rank = rank_ref[pl.ds(e, 1), :]
    gate = gate_ref[pl.ds(e, 1), :]

    def expert_rows(row0, nrows):
        rows = row0 + lax.broadcasted_iota(jnp.int32, (nrows, 1), 0)
        hit = rank == rows
        onehot = jnp.where(hit, 1.0, 0.0)
        xg = _dot(onehot.astype(BF16), xb_ref[...]).astype(BF16)
        gcol = jnp.sum(jnp.where(hit, gate, 0.0), axis=1, keepdims=True) * (1.0 / DN_ALPHA)
        hid = (_silu(_dot(xg, wgu_ref[:, 0:D_FF])) * _dot(xg, wgu_ref[:, D_FF:])).astype(BF16)
        y = (_dot(hid, wd_ref[...]) * gcol).astype(BF16)
        o_ref[...] += _dot(onehot.T.astype(BF16), y)

    mean_rows = tm * TOP_K // N_EXPERTS
    single_rows = tuple(mean_rows + spare for spare in MOE_SPARE_ROWS)
    lo = 0
    for nrows in single_rows:
        @pl.when((count > lo) & (count <= nrows))
        def _(nrows=nrows):
            expert_rows(0, nrows)
        lo = nrows

    @pl.when(count > single_rows[-1])
    def _():
        def full_block(r, carry):
            expert_rows(r * MOE_ROWS, MOE_ROWS)
            return carry

        lax.fori_loop(0, lax.shift_right_logical(count + (MOE_ROWS - 1), MOE_ROWS_LOG2),
                      full_block, 0)

    @pl.when(e == ne - 1)
    def _():
        o_ref[...] = _layer_norm(DN_ALPHA * o_ref[...], gain_ref[...], bias_ref[...])

    if single_tile:
        @pl.when(e == ne - 1)
        def _():
            done = write_sum(0, 0)
            done.start()
            done.wait()
    else:
        write_sum(i, slot).start()

        @pl.when(s == ne * nt - 1)
        def _():
            write_sum(i - 1, lax.rem(s + MOE_BUFFERS - 1, MOE_BUFFERS)).wait()
            write_sum(i, slot).wait()


def _moe_vmem_bytes(tm, weight_buffers):
    rows = tm * TOP_K // N_EXPERTS + MOE_SPARE_ROWS[-1]
    weights = weight_buffers * 3 * D_MODEL * D_FF * 2
    tiles = tm * D_MODEL * (MOE_BUFFERS * 4 + 2 * 2)
    values = rows * D_FF * (4 + 4 + 2) + rows * tm * (4 + 2) * 2 + tm * D_MODEL * 4
    return weights + tiles + values


def _moe_ln(xb, x, rank, gate, counts, wgu_bf, wd_bf, gain, bias, *, tm):
    t = x.shape[0]
    assert t % tm == 0
    nt = t // tm
    assert nt == 1 or nt >= MOE_BUFFERS
    per_tile = pl.BlockSpec((N_EXPERTS, tm), lambda e, i, c: (0, i))
    const = lambda shape: pl.BlockSpec(shape, lambda e, i, c: (0,) * len(shape),
                                       pipeline_mode=pl.Buffered(1))
    weight_buffers = 2 if _moe_vmem_bytes(tm, 2) <= V7X_VMEM_LIMIT_BYTES else 1
    down_bytes = D_FF * D_MODEL * 2
    down_buffers = 2 if _moe_vmem_bytes(tm, 1) + down_bytes <= V7X_VMEM_LIMIT_BYTES else 1
    grid_spec = pltpu.PrefetchScalarGridSpec(
        num_scalar_prefetch=1,
        grid=(N_EXPERTS, nt),
        in_specs=[pl.BlockSpec((tm, D_MODEL), lambda e, i, c: (i, 0)),
                  pl.BlockSpec(memory_space=pl.ANY), per_tile, per_tile,
                  pl.BlockSpec((None, D_MODEL, 2 * D_FF), lambda e, i, c: (e, 0, 0),
                               pipeline_mode=pl.Buffered(weight_buffers)),
                  pl.BlockSpec((None, D_FF, D_MODEL), lambda e, i, c: (e, 0, 0),
                               pipeline_mode=pl.Buffered(down_buffers)),
                  const((1, D_MODEL)), const((1, D_MODEL))],
        out_specs=pl.BlockSpec(memory_space=pl.ANY),
        scratch_shapes=[pltpu.VMEM((1 if nt == 1 else MOE_BUFFERS, tm, D_MODEL), F32),
                        pltpu.SemaphoreType.DMA((MOE_BUFFERS,)),
                        pltpu.SemaphoreType.DMA((MOE_BUFFERS,))],
    )
    return pl.pallas_call(
        functools.partial(_moe_ln_kernel, tm=tm, nt=nt),
        name="moe_ln",
        grid_spec=grid_spec,
        out_shape=jax.ShapeDtypeStruct((t, D_MODEL), F32),
        compiler_params=_params("arbitrary", "arbitrary"),
    )(counts, xb, x, rank, gate, wgu_bf, wd_bf, gain, bias)


def _trunk(x, n_seq, seq_len, pos_base, r0, conv_buf, w, late_w, moe_w,
           *, tm, tm_ffn, mc, lb, nsub, tm_moe):
    cast = tuple(late_w) if late_w[0].dtype == F32 else ()
    x, r_new, cast_out = _ret_mixer(x, w["ret_in"], w["inv"], r0, w["log_g"], w["ret_out"],
                                    w["gain"][0], w["bias"][0], cast, n_seq=n_seq,
                                    seq_len=seq_len, pos_base=pos_base, lb=lb, nsub=nsub)
    late_w = cast_out if cast else late_w
    ffn_gu, ffn_down, conv_in, conv_out = (a[0] for a in late_w)
    x = _ffn_ln(x, ffn_gu, ffn_down, w["gain"][1], w["bias"][1], tm=tm_ffn, mc=mc)
    cast = tuple(moe_w) if moe_w[0].dtype == F32 else ()
    x, xb, conv_new, cast_out = _conv_ln(x, conv_in, w["conv_w"], conv_out, conv_buf,
                                         w["gain"][2], w["bias"][2], cast, n_seq=n_seq,
                                         seq_len=seq_len, tm=tm, mc=mc)
    moe_w = cast_out if cast else moe_w
    rank, gate, counts = _router(x, w["router"], tm=tm_moe)
    y = _moe_ln(xb, x, rank, gate, counts[:, :, 0].reshape(-1), moe_w[0], moe_w[1],
                w["gain"][3], w["bias"][3], tm=tm_moe)
    return y, r_new, conv_new, late_w, moe_w


def kernel(x_prompt, x_sample, state_ret, state_conv, ret_w_in, ret_w_out, conv_w_in, conv_w,
           conv_w_out, ffn_w_gu, ffn_w_down, moe_w_router, moe_w_gu, moe_w_down, ln_gain, ln_bias):
    batch, seq, _ = x_prompt.shape
    dec_batch, dec_seq, _ = x_sample.shape

    w = {
        "ret_in": ret_w_in[0].astype(BF16),
        "ret_out": ret_w_out[0].astype(BF16),
        "conv_w": conv_w[0],
        "router": jnp.pad(moe_w_router[0], ((0, 0), (0, LANES - N_EXPERTS))),
        "gain": ln_gain.reshape(2 * DEPTH, 1, D_MODEL),
        "bias": ln_bias.reshape(2 * DEPTH, 1, D_MODEL),
        "inv": (1.0 / (ROPE_BASE ** jnp.linspace(0.0, 1.0, ROPE_HALF, dtype=F32))).reshape(1, ROPE_HALF),
        "log_g": jnp.log1p(-jnp.exp2(-5.0 - jnp.arange(RET_HEADS, dtype=F32))),
    }

    zero_ret = jnp.zeros((batch, RET_HEADS, RET_DK, RET_DV), F32)
    zero_conv = jnp.zeros((batch, CONV_WIDTH - 1, D_MODEL), F32)
    y_p, ret_p, conv_p, late_bf, moe_bf = _trunk(
        x_prompt.reshape(batch * seq, D_MODEL), batch, seq, 0, zero_ret, zero_conv, w,
        (ffn_w_gu, ffn_w_down, conv_w_in, conv_w_out), (moe_w_gu[0], moe_w_down[0]),
        tm=512, tm_ffn=1024, mc=256, lb=256, nsub=2, tm_moe=1024)
    y_s, ret_s, conv_s, _, _ = _trunk(
        x_sample.reshape(dec_batch * dec_seq, D_MODEL), dec_batch, dec_seq, PAST_LEN,
        state_ret[0], state_conv[0], w, late_bf, moe_bf, tm=dec_batch * dec_seq,
        tm_ffn=dec_batch * dec_seq, mc=dec_batch * dec_seq,
        lb=LANES, nsub=1, tm_moe=dec_batch * dec_seq)
    return (y_p.reshape(batch, seq, D_MODEL), y_s.reshape(dec_batch, dec_seq, D_MODEL),
            ret_p[None], ret_s[None], conv_p[None], conv_s[None])
```

```python
import functools
import math

import jax
import jax.numpy as jnp
from jax import lax
from jax.experimental import pallas as pl
from jax.experimental.pallas import tpu as pltpu

D_MODEL = 1024
CHUNK = 64
RET_HEADS = 4
RET_DK = D_MODEL // RET_HEADS
RET_DV = 2 * D_MODEL // RET_HEADS
RET_QK = RET_HEADS * RET_DK
RET_V = RET_HEADS * RET_DV
RET_IN = 2 * RET_QK + 2 * RET_V
ROPE_HALF = RET_DK // 2
CONV_WIDTH = 3
D_FF = 2816
N_EXPERTS = 8
ROPE_BASE = 10000.0
LN_EPS = 1e-5
GN_EPS = 1e-6
DEPTH = 2
DN_ALPHA = (2 * DEPTH) ** 0.25
PAST_LEN = 4096

V7X_VMEM_LIMIT_BYTES = 58 * 1024 * 1024
LANES = 128
SUBLANES = 8
BF16_ROWS = 2 * SUBLANES

MOE_ROWS_LOG2 = 8
MOE_ROWS = 1 << MOE_ROWS_LOG2
TOP_K = 2
MOE_SPARE_ROWS = (0, 16, 32, 64)
MOE_BUFFERS = 3
CUMSUM_CHUNK = 512

F32 = jnp.float32
BF16 = jnp.bfloat16


def _params(*semantics):
    return pltpu.CompilerParams(dimension_semantics=semantics,
                                vmem_limit_bytes=V7X_VMEM_LIMIT_BYTES)


def _resident(shape):
    zeros = (0,) * len(shape)
    return pl.BlockSpec(shape, lambda *_: zeros, pipeline_mode=pl.Buffered(1))


def _layer_norm(z, gain, bias):
    mu = jnp.mean(z, axis=-1, keepdims=True)
    zc = z - mu
    var = jnp.mean(zc * zc, axis=-1, keepdims=True)
    return zc * lax.rsqrt(var + LN_EPS) * gain + bias


def _silu(x):
    return x / (1.0 + jnp.exp(-x))


def _dot(a, b):
    return jnp.dot(a, b, preferred_element_type=F32)


def _cast_blocks(cast, steps):
    blocks = []
    for a in cast:
        n_lead, n_rows, n_cols = a.shape
        per_lead = max(p for p in range(1, n_rows // BF16_ROWS + 1)
                       if n_rows % (p * BF16_ROWS) == 0 and steps % (n_lead * p) == 0)
        repeat = steps // (n_lead * per_lead)

        def at(step, per_lead=per_lead, repeat=repeat):
            slab = step // repeat
            return slab // per_lead, slab % per_lead, 0
        blocks.append(((None, n_rows // per_lead, n_cols), at))
    return blocks


def _cast_slabs(cast_in, cast_out):
    for src, dst in zip(cast_in, cast_out):
        dst[...] = src[...].astype(BF16)


def _rotary(t, cos, sin):
    x1 = t[:, 0:ROPE_HALF]
    x2 = t[:, ROPE_HALF:RET_DK]
    return jnp.concatenate([x1 * cos - x2 * sin, x1 * sin + x2 * cos], axis=-1)


def _build_decay(decay, lg_ref, lb, chunk):
    ii = lax.broadcasted_iota(jnp.int32, (lb, lb), 0)
    jj = lax.broadcasted_iota(jnp.int32, (lb, lb), 1)
    shift = int(math.log2(chunk))
    visible = (jj >> shift) <= (ii >> shift)
    dist = jnp.abs(ii - jj).astype(F32)
    for h in range(RET_HEADS):
        decay[h] = jnp.where(visible, jnp.exp(lg_ref[h] * dist), 0.0)


def _retain_head(q_b, k_f, v_b, state, decay, h, lg, l_in):
    jrow = lax.broadcasted_iota(jnp.int32, (q_b.shape[0], 1), 0).astype(F32)
    k_dec = k_f * jnp.exp(lg * (l_in - 1.0 - jrow))
    r_old = state[h]
    s = lax.dot_general(q_b, k_f.astype(BF16), (((1,), (1,)), ((), ())),
                        preferred_element_type=F32)
    p = (s * decay[h]).astype(BF16)
    cross = jnp.exp(lg * (jrow + 1.0))
    o = (_dot(p, v_b) + _dot(q_b, r_old.astype(BF16)) * cross)[0:l_in]
    block_decay = jnp.exp(jnp.full((1, RET_DV), lg * l_in, F32))
    state[h] = r_old * block_decay + _dot(k_dec.T.astype(BF16), v_b)
    mu = jnp.mean(o, axis=-1, keepdims=True)
    oc = o - mu
    var = jnp.mean(oc * oc, axis=-1, keepdims=True)
    return oc * lax.rsqrt(var + GN_EPS)


def _ret_mixer_kernel(lg_ref, x_ref, win_ref, inv_ref, r0_ref, wout_ref, gain_ref, bias_ref, *rest,
                      lb, chunk, nsub, nstep, pos_base, n_cast):
    cast_in = rest[:n_cast]
    o_ref, rout_ref = rest[n_cast:n_cast + 2]
    cast_out = rest[n_cast + 2:2 * n_cast + 2]
    state, decay, gated = rest[2 * n_cast + 2:]
    _cast_slabs(cast_in, cast_out)
    b = pl.program_id(0)
    n = pl.program_id(1)

    @pl.when((b == 0) & (n == 0))
    def _():
        _build_decay(decay, lg_ref, lb, chunk)

    @pl.when(n == 0)
    def _():
        state[...] = r0_ref[...]

    for sb in range(nsub):
        rows = slice(sb * lb, (sb + 1) * lb)
        x = x_ref[rows, :]
        xb = x.astype(BF16)
        row = (n * nsub + sb) * lb + lax.broadcasted_iota(jnp.int32, (lb, 1), 0)
        ang = (pos_base + row).astype(F32) * inv_ref[...]
        cos = jnp.cos(ang)
        sin = jnp.sin(ang)
        q_all = _dot(xb, win_ref[:, 0:RET_QK])
        k_all = _dot(xb, win_ref[:, RET_QK:2 * RET_QK])
        v_all = _dot(xb, win_ref[:, 2 * RET_QK:2 * RET_QK + RET_V]).astype(BF16)
        g_all = _dot(xb, win_ref[:, 2 * RET_QK + RET_V:])
        for h in range(RET_HEADS):
            qk = slice(h * RET_DK, (h + 1) * RET_DK)
            vv = slice(h * RET_DV, (h + 1) * RET_DV)
            q_h = _rotary(q_all[:, qk], cos, sin).astype(BF16)
            k_h = _rotary(k_all[:, qk], cos, sin) * (RET_DK ** -0.5)
            on = _retain_head(q_h, k_h, v_all[:, vv], state, decay, h, lg_ref[h], lb)
            gated[rows, vv] = (_silu(g_all[:, vv]) * on).astype(BF16)
        z = DN_ALPHA * x + _dot(gated[rows, :], wout_ref[...])
        o_ref[rows, :] = _layer_norm(z, gain_ref[...], bias_ref[...])

    @pl.when(n == nstep - 1)
    def _():
        rout_ref[...] = state[...]


def _ret_mixer_short_kernel(lg_ref, x_ref, win_ref, inv_ref, r0_ref, wout_ref, gain_ref, bias_ref,
                            o_ref, rout_ref, state, decay, q_s, k_s, v_s, g_s, gated,
                            q_pad, k_pad, v_pad, *, lb, seq_len, n_seq, pos_base):
    b = pl.program_id(0)

    @pl.when(b == 0)
    def _():
        _build_decay(decay, lg_ref, lb, seq_len)
        for pad in (q_pad, k_pad, v_pad):
            pad[...] = jnp.zeros_like(pad)
        xb = x_ref[...].astype(BF16)
        row = lax.broadcasted_iota(jnp.int32, (xb.shape[0], 1), 0)
        ang = (pos_base + (row & (seq_len - 1))).astype(F32) * inv_ref[...]
        cos = jnp.cos(ang)
        sin = jnp.sin(ang)
        q_all = _dot(xb, win_ref[:, 0:RET_QK])
        k_all = _dot(xb, win_ref[:, RET_QK:2 * RET_QK])
        for h in range(RET_HEADS):
            qk = slice(h * RET_DK, (h + 1) * RET_DK)
            q_s[:, qk] = _rotary(q_all[:, qk], cos, sin).astype(BF16)
            k_s[:, qk] = _rotary(k_all[:, qk], cos, sin) * (RET_DK ** -0.5)
        v_s[...] = _dot(xb, win_ref[:, 2 * RET_QK:2 * RET_QK + RET_V]).astype(BF16)
        g_s[...] = _dot(xb, win_ref[:, 2 * RET_QK + RET_V:])

    state[...] = r0_ref[...]
    rows = pl.ds(pl.multiple_of(b * seq_len, seq_len), seq_len)
    q_pad[0:seq_len, :] = q_s[rows, :]
    k_pad[0:seq_len, :] = k_s[rows, :]
    v_pad[0:seq_len, :] = v_s[rows, :]
    for h in range(RET_HEADS):
        qk = slice(h * RET_DK, (h + 1) * RET_DK)
        vv = slice(h * RET_DV, (h + 1) * RET_DV)
        on = _retain_head(q_pad[:, qk], k_pad[:, qk], v_pad[:, vv], state, decay, h, lg_ref[h],
                          seq_len)
        gated[rows, vv] = (_silu(g_s[rows, vv]) * on).astype(BF16)
    rout_ref[...] = state[...]

    @pl.when(b == n_seq - 1)
    def _():
        z = DN_ALPHA * x_ref[...] + _dot(gated[...], wout_ref[...])
        o_ref[...] = _layer_norm(z, gain_ref[...], bias_ref[...])


def _ret_mixer(x, win_bf, inv, r0, log_g, wout_bf, gain, bias, cast=(),
               *, n_seq, seq_len, pos_base, lb, nsub):
    t = n_seq * seq_len
    st_shape = (None, RET_HEADS, RET_DK, RET_DV)
    weights = [_resident((D_MODEL, RET_IN)), _resident((1, ROPE_HALF))]
    tail = [_resident((RET_V, D_MODEL)), _resident((1, D_MODEL)), _resident((1, D_MODEL))]
    out_shape = [jax.ShapeDtypeStruct((t, D_MODEL), F32),
                 jax.ShapeDtypeStruct((n_seq, RET_HEADS, RET_DK, RET_DV), F32)]
    scratch = [pltpu.VMEM((RET_HEADS, RET_DK, RET_DV), F32), pltpu.VMEM((RET_HEADS, lb, lb), F32)]
    smem = pl.BlockSpec(memory_space=pltpu.SMEM)
    if seq_len < lb:
        assert seq_len <= CHUNK and seq_len & (seq_len - 1) == 0 and lb % LANES == 0
        assert not cast
        st = pl.BlockSpec(st_shape, lambda b: (b, 0, 0, 0))
        scratch += [pltpu.VMEM((t, RET_QK), BF16), pltpu.VMEM((t, RET_QK), F32),
                    pltpu.VMEM((t, RET_V), BF16), pltpu.VMEM((t, RET_V), F32),
                    pltpu.VMEM((t, RET_V), BF16), pltpu.VMEM((lb, RET_QK), BF16),
                    pltpu.VMEM((lb, RET_QK), F32), pltpu.VMEM((lb, RET_V), BF16)]
        y, r_new = pl.pallas_call(
            functools.partial(_ret_mixer_short_kernel, lb=lb, seq_len=seq_len, n_seq=n_seq,
                              pos_base=pos_base),
            name="ret_mixer_short",
            grid=(n_seq,),
            in_specs=[smem, _resident((t, D_MODEL))] + weights + [st] + tail,
            out_specs=[pl.BlockSpec((t, D_MODEL), lambda b: (0, 0)), st],
            out_shape=out_shape,
            scratch_shapes=scratch,
            compiler_params=_params("arbitrary"),
        )(log_g, x, win_bf, inv, r0, wout_bf, gain, bias)
        return y, r_new, ()
    assert seq_len % (nsub * lb) == 0 and lb % CHUNK == 0 and CHUNK & (CHUNK - 1) == 0
    nstep = seq_len // (nsub * lb)
    rows = pl.BlockSpec((nsub * lb, D_MODEL), lambda b, n: (b * nstep + n, 0))
    st = pl.BlockSpec(st_shape, lambda b, n: (b, 0, 0, 0))
    cast_specs = [pl.BlockSpec(shape, lambda b, n, at=at: at(b * nstep + n)) for shape, at in
                  _cast_blocks(cast, n_seq * nstep)]
    outs = pl.pallas_call(
        functools.partial(_ret_mixer_kernel, lb=lb, chunk=CHUNK, nsub=nsub, nstep=nstep,
                          pos_base=pos_base, n_cast=len(cast)),
        name="ret_mixer",
        grid=(n_seq, nstep),
        in_specs=[smem, rows] + weights + [st] + tail + cast_specs,
        out_specs=[rows, st] + cast_specs,
        out_shape=out_shape + [jax.ShapeDtypeStruct(a.shape, BF16) for a in cast],
        scratch_shapes=scratch + [pltpu.VMEM((nsub * lb, RET_V), BF16)],
        compiler_params=_params("arbitrary", "arbitrary"),
    )(log_g, x, win_bf, inv, r0, wout_bf, gain, bias, *cast)
    return outs[0], outs[1], tuple(outs[2:])


def _ffn_ln_kernel(x_ref, wgu_ref, wd_ref, gain_ref, bias_ref, o_ref, *, tm, mc):
    for c in range(tm // mc):
        rows = slice(c * mc, (c + 1) * mc)
        x = x_ref[rows, :]
        xb = x.astype(BF16)
        gate = _dot(xb, wgu_ref[:, 0:D_FF])
        up = _dot(xb, wgu_ref[:, D_FF:2 * D_FF])
        hid = (_silu(gate) * up).astype(BF16)
        z = DN_ALPHA * x + _dot(hid, wd_ref[...])
        o_ref[rows, :] = _layer_norm(z, gain_ref[...], bias_ref[...])


def _ffn_ln(x, wgu_bf, wd_bf, gain, bias, *, tm, mc):
    t = x.shape[0]
    assert t % tm == 0 and tm % mc == 0
    return pl.pallas_call(
        functools.partial(_ffn_ln_kernel, tm=tm, mc=mc),
        name="ffn_ln",
        grid=(t // tm,),
        in_specs=[pl.BlockSpec((tm, D_MODEL), lambda i: (i, 0)),
                  _resident((D_MODEL, 2 * D_FF)), _resident((D_FF, D_MODEL)),
                  _resident((1, D_MODEL)), _resident((1, D_MODEL))],
        out_specs=pl.BlockSpec((tm, D_MODEL), lambda i: (i, 0)),
        out_shape=jax.ShapeDtypeStruct((t, D_MODEL), F32),
        compiler_params=_params("parallel"),
    )(x, wgu_bf, wd_bf, gain, bias)


def _conv_ln_kernel(x_ref, win_ref, wc_ref, wout_ref, buf_ref, gain_ref, bias_ref, *rest,
                    tm, mc, seg, whole_seqs, tiles_per_seq, n_cast):
    cast_in = rest[:n_cast]
    o_ref, ob_ref, st_ref = rest[n_cast:n_cast + 3]
    cast_out = rest[n_cast + 3:2 * n_cast + 3]
    u_win, prev, mixed = rest[2 * n_cast + 3:]
    _cast_slabs(cast_in, cast_out)
    i = pl.program_id(0)
    w0 = wc_ref[0:1, :]
    w1 = wc_ref[1:2, :]
    w2 = wc_ref[2:3, :]
    tail = CONV_WIDTH - 1
    segs = mc // seg
    for c in range(tm // mc):
        rows = slice(c * mc, (c + 1) * mc)
        x = x_ref[rows, :]
        xb = x.astype(BF16)
        gate_b = _dot(xb, win_ref[:, 0:D_MODEL])
        u = _dot(xb, win_ref[:, D_MODEL:2 * D_MODEL]) * _dot(xb, win_ref[:, 2 * D_MODEL:])
        win = u_win.at[c]
        for s in range(segs):
            sg = c * segs + s
            if whole_seqs:
                prev[SUBLANES - tail:SUBLANES, :] = buf_ref[sg]
            elif sg == 0:
                carried = prev[SUBLANES - tail:SUBLANES, :]
                prev[SUBLANES - tail:SUBLANES, :] = jnp.where(i % tiles_per_seq == 0,
                                                              buf_ref[0], carried)
            win[0:SUBLANES, :] = prev[...]
            win[SUBLANES:SUBLANES + seg, :] = u[s * seg:(s + 1) * seg]
            conv = (w0 * win[SUBLANES - 2:SUBLANES - 2 + seg, :]
                    + w1 * win[SUBLANES - 1:SUBLANES - 1 + seg, :]
                    + w2 * win[SUBLANES:SUBLANES + seg, :])
            mixed[c * mc + s * seg:c * mc + (s + 1) * seg, :] = (
                gate_b[s * seg:(s + 1) * seg] * conv).astype(BF16)
            prev[...] = win[seg:seg + SUBLANES, :]
            if whole_seqs:
                st_ref[sg] = win[SUBLANES + seg - tail:SUBLANES + seg, :]
            elif sg == tm // seg - 1:
                st_ref[0] = win[SUBLANES + seg - tail:SUBLANES + seg, :]
        z = DN_ALPHA * x + _dot(mixed[rows, :], wout_ref[...])
        out = _layer_norm(z, gain_ref[...], bias_ref[...])
        o_ref[rows, :] = out
        ob_ref[rows, :] = out.astype(BF16)


def _conv_ln(x, win_bf, wconv, wout_bf, buf, gain, bias, cast=(), *, n_seq, seq_len, tm, mc):
    t = x.shape[0]
    seg = min(seq_len, mc)
    assert t % tm == 0 and tm % mc == 0 and mc % seg == 0 and seg % SUBLANES == 0
    assert seg == seq_len or seq_len % tm == 0
    tiles_per_seq = max(seq_len // tm, 1)
    seq_per_tile = tm // seg if seg == seq_len else 1
    tail = CONV_WIDTH - 1
    steps = t // tm
    st_spec = pl.BlockSpec((seq_per_tile, tail, D_MODEL), lambda i: (i // tiles_per_seq, 0, 0))
    cast_specs = [pl.BlockSpec(shape, lambda i, at=at: at(i)) for shape, at in
                  _cast_blocks(cast, steps)]
    outs = pl.pallas_call(
        functools.partial(_conv_ln_kernel, tm=tm, mc=mc, seg=seg, whole_seqs=seg == seq_len,
                          tiles_per_seq=tiles_per_seq, n_cast=len(cast)),
        name="conv_ln",
        grid=(steps,),
        in_specs=[pl.BlockSpec((tm, D_MODEL), lambda i: (i, 0)),
                  _resident((D_MODEL, 3 * D_MODEL)), _resident((CONV_WIDTH, D_MODEL)),
                  _resident((D_MODEL, D_MODEL)), st_spec,
                  _resident((1, D_MODEL)), _resident((1, D_MODEL))] + cast_specs,
        out_specs=[pl.BlockSpec((tm, D_MODEL), lambda i: (i, 0)),
                   pl.BlockSpec((tm, D_MODEL), lambda i: (i, 0)), st_spec] + cast_specs,
        out_shape=[jax.ShapeDtypeStruct((t, D_MODEL), F32),
                   jax.ShapeDtypeStruct((t, D_MODEL), BF16),
                   jax.ShapeDtypeStruct((n_seq, tail, D_MODEL), F32)]
                  + [jax.ShapeDtypeStruct(a.shape, BF16) for a in cast],
        scratch_shapes=[pltpu.VMEM((tm // mc, SUBLANES + seg, D_MODEL), F32),
                        pltpu.VMEM((SUBLANES, D_MODEL), F32),
                        pltpu.VMEM((tm, D_MODEL), BF16)],
        compiler_params=_params("arbitrary"),
    )(x, win_bf, wconv, wout_bf, buf, gain, bias, *cast)
    return outs[0], outs[1], outs[2], tuple(outs[3:])


def _split_bf16(a):
    hi = a.astype(BF16)
    lo = (a - hi.astype(F32)).astype(BF16)
    return hi, lo


def _router_kernel(x_ref, wr_ref, rank_ref, gate_ref, cnt_ref, *, tm):
    x_hi, x_lo = _split_bf16(x_ref[...])
    w_hi, w_lo = _split_bf16(wr_ref[...])
    hi = _dot(x_hi, jnp.concatenate([w_hi, w_lo], axis=1))
    logits = hi[:, 0:LANES] + (_dot(x_lo, w_hi) + hi[:, LANES:])
    lt = logits.T[0:N_EXPERTS, :]
    eidx = lax.broadcasted_iota(jnp.int32, (N_EXPERTS, tm), 0).astype(F32)
    neg = jnp.float32(-jnp.inf)
    m1 = jnp.max(lt, axis=0, keepdims=True)
    i1 = jnp.min(jnp.where(lt == m1, eidx, float(N_EXPERTS)), axis=0, keepdims=True)
    first = eidx == i1
    rest = jnp.where(first, neg, lt)
    m2 = jnp.max(rest, axis=0, keepdims=True)
    i2 = jnp.min(jnp.where(rest == m2, eidx, float(N_EXPERTS)), axis=0, keepdims=True)
    second = eidx == i2
    e2 = jnp.exp(m2 - m1)
    w1 = 1.0 / (1.0 + e2)
    w2 = e2 / (1.0 + e2)
    gate_ref[...] = jnp.where(first, w1, 0.0) + jnp.where(second, w2, 0.0)
    sel = first | second
    self32 = jnp.where(sel, 1.0, 0.0)

    c = CUMSUM_CHUNK if tm % CUMSUM_CHUNK == 0 else tm
    si = lax.broadcasted_iota(jnp.int32, (c, c), 0)
    ti = lax.broadcasted_iota(jnp.int32, (c, c), 1)
    tri = jnp.where(si < ti, 1.0, 0.0).astype(BF16)
    offset = jnp.zeros((N_EXPERTS, 1), F32)
    for j in range(tm // c):
        blk = self32[:, j * c:(j + 1) * c]
        rank = _dot(blk.astype(BF16), tri) + offset
        rank_ref[:, j * c:(j + 1) * c] = jnp.where(sel[:, j * c:(j + 1) * c],
                                                   rank.astype(jnp.int32), -1)
        offset = offset + jnp.sum(blk, axis=1, keepdims=True)
    cnt_ref[...] = jnp.broadcast_to(offset.astype(jnp.int32), (N_EXPERTS, LANES))


def _router(x, wr_pad, *, tm):
    t = x.shape[0]
    assert t % tm == 0
    nt = t // tm
    return pl.pallas_call(
        functools.partial(_router_kernel, tm=tm),
        name="router",
        grid=(nt,),
        in_specs=[pl.BlockSpec((tm, D_MODEL), lambda i: (i, 0)), _resident((D_MODEL, LANES))],
        out_specs=[pl.BlockSpec((N_EXPERTS, tm), lambda i: (0, i)),
                   pl.BlockSpec((N_EXPERTS, tm), lambda i: (0, i)),
                   pl.BlockSpec((None, N_EXPERTS, LANES), lambda i: (i, 0, 0))],
        out_shape=[jax.ShapeDtypeStruct((N_EXPERTS, t), jnp.int32),
                   jax.ShapeDtypeStruct((N_EXPERTS, t), F32),
                   jax.ShapeDtypeStruct((nt, N_EXPERTS, LANES), jnp.int32)],
        compiler_params=_params("parallel"),
    )(x, wr_pad)


def _moe_ln_kernel(cnt_ref, xb_ref, x_hbm, rank_ref, gate_ref, wgu_ref, wd_ref,
                   gain_ref, bias_ref, out_hbm, buf, rsem, wsem, *, tm, nt):
    e = pl.program_id(0)
    i = pl.program_id(1)
    ne = pl.num_programs(0)
    single_tile = nt == 1
    s = e * nt + i
    slot = 0 if single_tile else lax.rem(s, MOE_BUFFERS)
    o_ref = buf.at[slot]

    def tile_rows(hbm, tile):
        start = tile * tm
        if not isinstance(start, int):
            start = pl.multiple_of(start, tm)
        return hbm.at[pl.ds(start, tm), :]

    def read_x(tile, dst):
        return pltpu.make_async_copy(tile_rows(x_hbm, tile), buf.at[dst], rsem.at[dst])

    def read_sum(tile, dst):
        return pltpu.make_async_copy(tile_rows(out_hbm, tile), buf.at[dst], rsem.at[dst])

    def write_sum(tile, src):
        return pltpu.make_async_copy(buf.at[src], tile_rows(out_hbm, tile), wsem.at[src])

    if single_tile:
        @pl.when(e == 0)
        def _():
            first = read_x(0, 0)
            first.start()
            first.wait()
    else:
        next_slot = lax.rem(s + 1, MOE_BUFFERS)
        next_tile = jnp.where(i + 1 < nt, i + 1, 0)

        @pl.when(s == 0)
        def _():
            read_x(0, 0).start()

        @pl.when(s >= 2)
        def _():
            write_sum(jnp.where(i >= 2, i - 2, i - 2 + nt), next_slot).wait()

        @pl.when(s + 1 < nt)
        def _():
            read_x(next_tile, next_slot).start()

        @pl.when((s + 1 >= nt) & (s + 1 < ne * nt))
        def _():
            read_sum(next_tile, next_slot).start()

        @pl.when(e == 0)
        def _():
            read_x(i, slot).wait()

        @pl.when(e > 0)
        def _():
            read_sum(i, slot).wait()

    count = cnt_ref[i * N_EXPERTS + e]
    rank = rank_ref[pl.ds(e, 1), :]
    gate = gate_ref[pl.ds(e, 1), :]

    def expert_rows(row0, nrows):
        rows = row0 + lax.broadcasted_iota(jnp.int32, (nrows, 1), 0)
        hit = rank == rows
        onehot = jnp.where(hit, 1.0, 0.0)
        xg = _dot(onehot.astype(BF16), xb_ref[...]).astype(BF16)
        gcol = jnp.sum(jnp.where(hit, gate, 0.0), axis=1, keepdims=True) * (1.0 / DN_ALPHA)
        hid = (_silu(_dot(xg, wgu_ref[:, 0:D_FF])) * _dot(xg, wgu_ref[:, D_FF:])).astype(BF16)
        y = (_dot(hid, wd_ref[...]) * gcol).astype(BF16)
        o_ref[...] += _dot(onehot.T.astype(BF16), y)

    mean_rows = tm * TOP_K // N_EXPERTS
    single_rows = tuple(mean_rows + spare for spare in MOE_SPARE_ROWS)
    lo = 0
    for nrows in single_rows:
        @pl.when((count > lo) & (count <= nrows))
        def _(nrows=nrows):
            expert_rows(0, nrows)
        lo = nrows

    @pl.when(count > single_rows[-1])
    def _():
        def full_block(r, carry):
            expert_rows(r * MOE_ROWS, MOE_ROWS)
            return carry

        lax.fori_loop(0, lax.shift_right_logical(count + (MOE_ROWS - 1), MOE_ROWS_LOG2),
                      full_block, 0)

    @pl.when(e == ne - 1)
    def _():
        o_ref[...] = _layer_norm(DN_ALPHA * o_ref[...], gain_ref[...], bias_ref[...])

    if single_tile:
        @pl.when(e == ne - 1)
        def _():
            done = write_sum(0, 0)
            done.start()
            done.wait()
    else:
        write_sum(i, slot).start()

        @pl.when(s == ne * nt - 1)
        def _():
            write_sum(i - 1, lax.rem(s + MOE_BUFFERS - 1, MOE_BUFFERS)).wait()
            write_sum(i, slot).wait()


def _moe_vmem_bytes(tm, weight_buffers):
    rows = tm * TOP_K // N_EXPERTS + MOE_SPARE_ROWS[-1]
    weights = weight_buffers * 3 * D_MODEL * D_FF * 2
    tiles = tm * D_MODEL * (MOE_BUFFERS * 4 + 2 * 2)
    values = rows * D_FF * (4 + 4 + 2) + rows * tm * (4 + 2) * 2 + tm * D_MODEL * 4
    return weights + tiles + values


def _moe_ln(xb, x, rank, gate, counts, wgu_bf, wd_bf, gain, bias, *, tm):
    t = x.shape[0]
    assert t % tm == 0
    nt = t // tm
    assert nt == 1 or nt >= MOE_BUFFERS
    per_tile = pl.BlockSpec((N_EXPERTS, tm), lambda e, i, c: (0, i))
    const = lambda shape: pl.BlockSpec(shape, lambda e, i, c: (0,) * len(shape),
                                       pipeline_mode=pl.Buffered(1))
    weight_buffers = 2 if _moe_vmem_bytes(tm, 2) <= V7X_VMEM_LIMIT_BYTES else 1
    down_bytes = D_FF * D_MODEL * 2
    down_buffers = 2 if _moe_vmem_bytes(tm, 1) + down_bytes <= V7X_VMEM_LIMIT_BYTES else 1
    grid_spec = pltpu.PrefetchScalarGridSpec(
        num_scalar_prefetch=1,
        grid=(N_EXPERTS, nt),
        in_specs=[pl.BlockSpec((tm, D_MODEL), lambda e, i, c: (i, 0)),
                  pl.BlockSpec(memory_space=pl.ANY), per_tile, per_tile,
                  pl.BlockSpec((None, D_MODEL, 2 * D_FF), lambda e, i, c: (e, 0, 0),
                               pipeline_mode=pl.Buffered(weight_buffers)),
                  pl.BlockSpec((None, D_FF, D_MODEL), lambda e, i, c: (e, 0, 0),
                               pipeline_mode=pl.Buffered(down_buffers)),
                  const((1, D_MODEL)), const((1, D_MODEL))],
        out_specs=pl.BlockSpec(memory_space=pl.ANY),
        scratch_shapes=[pltpu.VMEM((1 if nt == 1 else MOE_BUFFERS, tm, D_MODEL), F32),
                        pltpu.SemaphoreType.DMA((MOE_BUFFERS,)),
                        pltpu.SemaphoreType.DMA((MOE_BUFFERS,))],
    )
    return pl.pallas_call(
        functools.partial(_moe_ln_kernel, tm=tm, nt=nt),
        name="moe_ln",
        grid_spec=grid_spec,
        out_shape=jax.ShapeDtypeStruct((t, D_MODEL), F32),
        compiler_params=_params("arbitrary", "arbitrary"),
    )(counts, xb, x, rank, gate, wgu_bf, wd_bf, gain, bias)


def _trunk(x, n_seq, seq_len, pos_base, r0, conv_buf, w, late_w, moe_w,
           *, tm, tm_ffn, mc, lb, nsub, tm_moe):
    cast = tuple(late_w) if late_w[0].dtype == F32 else ()
    x, r_new, cast_out = _ret_mixer(x, w["ret_in"], w["inv"], r0, w["log_g"], w["ret_out"],
                                    w["gain"][0], w["bias"][0], cast, n_seq=n_seq,
                                    seq_len=seq_len, pos_base=pos_base, lb=lb, nsub=nsub)
    late_w = cast_out if cast else late_w
    ffn_gu, ffn_down, conv_in, conv_out = (a[0] for a in late_w)
    x = _ffn_ln(x, ffn_gu, ffn_down, w["gain"][1], w["bias"][1], tm=tm_ffn, mc=mc)
    cast = tuple(moe_w) if moe_w[0].dtype == F32 else ()
    x, xb, conv_new, cast_out = _conv_ln(x, conv_in, w["conv_w"], conv_out, conv_buf,
                                         w["gain"][2], w["bias"][2], cast, n_seq=n_seq,
                                         seq_len=seq_len, tm=tm, mc=mc)
    moe_w = cast_out if cast else moe_w
    rank, gate, counts = _router(x, w["router"], tm=tm_moe)
    y = _moe_ln(xb, x, rank, gate, counts[:, :, 0].reshape(-1), moe_w[0], moe_w[1],
                w["gain"][3], w["bias"][3], tm=tm_moe)
    return y, r_new, conv_new, late_w, moe_w


def kernel(x_prompt, x_sample, state_ret, state_conv, ret_w_in, ret_w_out, conv_w_in, conv_w,
           conv_w_out, ffn_w_gu, ffn_w_down, moe_w_router, moe_w_gu, moe_w_down, ln_gain, ln_bias):
    batch, seq, _ = x_prompt.shape
    dec_batch, dec_seq, _ = x_sample.shape

    w = {
        "ret_in": ret_w_in[0].astype(BF16),
        "ret_out": ret_w_out[0].astype(BF16),
        "conv_w": conv_w[0],
        "router": jnp.pad(moe_w_router[0], ((0, 0), (0, LANES - N_EXPERTS))),
        "gain": ln_gain.reshape(2 * DEPTH, 1, D_MODEL),
        "bias": ln_bias.reshape(2 * DEPTH, 1, D_MODEL),
        "inv": (1.0 / (ROPE_BASE ** jnp.linspace(0.0, 1.0, ROPE_HALF, dtype=F32))).reshape(1, ROPE_HALF),
        "log_g": jnp.log1p(-jnp.exp2(-5.0 - jnp.arange(RET_HEADS, dtype=F32))),
    }

    zero_ret = jnp.zeros((batch, RET_HEADS, RET_DK, RET_DV), F32)
    zero_conv = jnp.zeros((batch, CONV_WIDTH - 1, D_MODEL), F32)
    y_p, ret_p, conv_p, late_bf, moe_bf = _trunk(
        x_prompt.reshape(batch * seq, D_MODEL), batch, seq, 0, zero_ret, zero_conv, w,
        (ffn_w_gu, ffn_w_down, conv_w_in, conv_w_out), (moe_w_gu[0], moe_w_down[0]),
        tm=512, tm_ffn=1024, mc=256, lb=256, nsub=2, tm_moe=1024)
    y_s, ret_s, conv_s, _, _ = _trunk(
        x_sample.reshape(dec_batch * dec_seq, D_MODEL), dec_batch, dec_seq, PAST_LEN,
        state_ret[0], state_conv[0], w, late_bf, moe_bf, tm=dec_batch * dec_seq,
        tm_ffn=dec_batch * dec_seq, mc=dec_batch * dec_seq,
        lb=LANES, nsub=1, tm_moe=dec_batch * dec_seq)
    return (y_p.reshape(batch, seq, D_MODEL), y_s.reshape(dec_batch, dec_seq, D_MODEL),
            ret_p[None], ret_s[None], conv_p[None], conv_s[None])
```
